```python
import jax, jax.numpy as jnp
from jax import lax
import numpy as np

D_MODEL = 1024
BATCH = 4
SEQ = 4096
DEPTH = 2
DEC_BATCH = 32
DEC_SEQ = 1
PAST_LEN = 8192
PAGE_SIZE = 128

MIX_WIDTH = D_MODEL
A_HEADS = 8
A_HEAD_DIM = 64
A_WIDTH = A_HEADS * A_HEAD_DIM
ROT_DIM = A_HEAD_DIM // 4
ROPE_THETA = 500000.0
MOBA_BLOCK = 256
MOBA_TOPK = 3
MOBA_Q_CHUNK = 32
G_HEADS = 4
G_KEY_DIM = 64
G_VAL_DIM = 128
G_KEY_WIDTH = G_HEADS * G_KEY_DIM
G_WIDTH = G_HEADS * G_VAL_DIM
G_GATE_RANK = 16
G_GATE_NORM = 16.0
G_CHUNK = 32
NORM_EPS = 1e-6
IN_WIDTHS = (A_WIDTH, A_WIDTH, A_WIDTH, A_WIDTH, G_KEY_WIDTH, G_KEY_WIDTH, G_WIDTH, G_WIDTH, G_GATE_RANK)
IN_WIDTH = 4 * A_WIDTH + 2 * G_KEY_WIDTH + 2 * G_WIDTH + G_GATE_RANK

kernel_name = 'hymba_moba_gla_decode_step'


def rms_norm(x, g):
    xf = x.astype(jnp.float32)
    y = xf * lax.rsqrt(jnp.mean(xf * xf, axis=-1, keepdims=True) + NORM_EPS)
    return (y * g.astype(jnp.float32)).astype(x.dtype)


def rope_partial(x, pos):
    half = ROT_DIM // 2
    inv = jnp.power(jnp.float32(ROPE_THETA), -jnp.arange(half, dtype=jnp.float32) * (2.0 / ROT_DIM))
    ang = pos.astype(jnp.float32)[:, None] * inv[None, :]
    cos = jnp.cos(ang)[None, :, None, :].astype(x.dtype)
    sin = jnp.sin(ang)[None, :, None, :].astype(x.dtype)
    x1, x2, rest = x[..., :half], x[..., half:ROT_DIM], x[..., ROT_DIM:]
    return jnp.concatenate([x1 * cos - x2 * sin, x2 * cos + x1 * sin, rest], axis=-1)


def moba_attention(q, k, v, q_pos):
    B, H, Q, dh = q.shape
    T = k.shape[2]
    Tp = -(-T // MOBA_BLOCK) * MOBA_BLOCK
    padk = ((0, 0), (0, 0), (0, Tp - T), (0, 0))
    k = jnp.pad(k, padk)
    v = jnp.pad(v, padk)
    nb = Tp // MOBA_BLOCK
    kb = k.reshape(B, H, nb, MOBA_BLOCK, dh)
    vb = v.reshape(B, H, nb, MOBA_BLOCK, dh)
    k_mean = jnp.mean(kb.astype(jnp.float32), axis=3)
    q_blk = q_pos // MOBA_BLOCK
    gate = jnp.einsum('bhqd,bhnd->bhqn', q.astype(jnp.float32), k_mean)
    past = jnp.arange(nb)[None, :] < q_blk[:, None]
    gate = jnp.where(past[None, None], gate, -jnp.inf)
    n_sel = min(MOBA_TOPK, nb)
    _, sel = lax.top_k(gate, n_sel)
    sel_ok = sel < q_blk[None, None, :, None]
    own = jnp.broadcast_to(q_blk[None, None, :, None], (B, H, Q, 1)).astype(sel.dtype)
    blk_idx = jnp.concatenate([sel, own], axis=-1)
    blk_ok = jnp.concatenate([sel_ok, jnp.ones((B, H, Q, 1), bool)], axis=-1)
    nk = n_sel + 1
    qc = min(MOBA_Q_CHUNK, Q)
    Qp = -(-Q // qc) * qc
    pq = Qp - Q
    qq = jnp.pad(q, ((0, 0), (0, 0), (0, pq), (0, 0)), mode='edge')
    bi_ = jnp.pad(blk_idx, ((0, 0), (0, 0), (0, pq), (0, 0)), mode='edge')
    bo_ = jnp.pad(blk_ok, ((0, 0), (0, 0), (0, pq), (0, 0)), mode='edge')
    qp_ = jnp.pad(q_pos, (0, pq), mode='edge')
    nC = Qp // qc
    q_ch = qq.reshape(B, H, nC, qc, dh).transpose(2, 0, 1, 3, 4)
    i_ch = bi_.reshape(B, H, nC, qc, nk).transpose(2, 0, 1, 3, 4)
    o_ch = bo_.reshape(B, H, nC, qc, nk).transpose(2, 0, 1, 3, 4)
    p_ch = qp_.reshape(nC, qc)
    b_ix = jnp.arange(B)[:, None, None, None]
    h_ix = jnp.arange(H)[None, :, None, None]
    offs = jnp.arange(MOBA_BLOCK)
    scale = A_HEAD_DIM ** -0.5

    def attend_block(args):
        qb, idx, ok, qp = args
        kg = kb[b_ix, h_ix, idx]
        vg = vb[b_ix, h_ix, idx]
        s = jnp.einsum('bhqd,bhqnkd->bhqnk', qb, kg).astype(jnp.float32) * scale
        kpos = idx[..., None] * MOBA_BLOCK + offs
        valid = ok[..., None] & (kpos <= qp[None, None, :, None, None])
        s = jnp.where(valid, s, -jnp.inf)
        p = jax.nn.softmax(s.reshape(B, H, qc, nk * MOBA_BLOCK), axis=-1)
        p = p.reshape(B, H, qc, nk, MOBA_BLOCK).astype(vg.dtype)
        return jnp.einsum('bhqnk,bhqnkd->bhqd', p, vg)

    o = lax.map(attend_block, (q_ch, i_ch, o_ch, p_ch))
    return o.transpose(1, 2, 0, 3, 4).reshape(B, H, Qp, dh)[:, :, :Q]


def gla_chunked(q, k, v, g, s0):
    B, H, L, dk = q.shape
    dv = v.shape[-1]
    out_dtype = v.dtype
    C = min(G_CHUNK, L)
    Lp = -(-L // C) * C
    pad = ((0, 0), (0, 0), (0, Lp - L), (0, 0))
    nC = Lp // C

    def blocks(t):
        t = jnp.pad(t.astype(jnp.float32), pad)
        return t.reshape(B, H, nC, C, t.shape[-1])

    q, k, v, g = blocks(q), blocks(k), blocks(v), blocks(g)
    b = jnp.cumsum(g, axis=3)
    q_t = q * jnp.exp(b) * (dk ** -0.5)
    k_t = k * jnp.exp(-b)
    causal = jnp.tril(jnp.ones((C, C), bool))
    a = jnp.where(causal, jnp.einsum('bhctk,bhcsk->bhcts', q_t, k_t), 0.0)
    o_intra = jnp.einsum('bhcts,bhcsv->bhctv', a, v)
    b_last = b[:, :, :, -1, :]
    kv = jnp.einsum('bhcsk,bhcsv->bhckv', k * jnp.exp(b_last[:, :, :, None, :] - b), v)

    def step(S, xs):
        dec, kv_c = xs
        return dec[..., None] * S + kv_c, S

    s_fin, s_start = lax.scan(step, s0.astype(jnp.float32),
                              (jnp.exp(b_last).transpose(2, 0, 1, 3), kv.transpose(2, 0, 1, 3, 4)))
    o_inter = jnp.einsum('bhctk,bhckv->bhctv', q_t, s_start.transpose(1, 2, 0, 3, 4))
    o = (o_intra + o_inter).reshape(B, H, Lp, dv)[:, :, :L]
    return o.astype(out_dtype), s_fin.astype(s0.dtype)


def hybrid_layer(x, pos, k_past, v_past, s0, norm_g, w_in, w_a2, b_a2, gla_norm_g, w_out):
    B, L, _ = x.shape
    h = rms_norm(x, norm_g)
    z = jnp.einsum('bld,de->ble', h, w_in)
    points = np.cumsum(IN_WIDTHS)[:-1].tolist()
    qa, ka, va, ga, qg, kg, vg, gg, a1 = jnp.split(z, points, axis=-1)
    qa = rope_partial(qa.reshape(B, L, A_HEADS, A_HEAD_DIM), pos)
    ka = rope_partial(ka.reshape(B, L, A_HEADS, A_HEAD_DIM), pos)
    va = va.reshape(B, L, A_HEADS, A_HEAD_DIM)
    if k_past is None:
        k_all, v_all = ka, va
    else:
        k_all = jnp.concatenate([k_past.astype(ka.dtype), ka], axis=1)
        v_all = jnp.concatenate([v_past.astype(va.dtype), va], axis=1)
    oa = moba_attention(qa.transpose(0, 2, 1, 3), k_all.transpose(0, 2, 1, 3),
                        v_all.transpose(0, 2, 1, 3), pos)
    oa = oa.transpose(0, 2, 1, 3).reshape(B, L, A_WIDTH) * jax.nn.silu(ga)
    def heads(t, d):
        return t.reshape(B, L, G_HEADS, d).transpose(0, 2, 1, 3)
    log_f = jax.nn.log_sigmoid((jnp.einsum('blr,rk->blk', a1, w_a2) + b_a2).astype(jnp.float32)) / G_GATE_NORM
    og, s_new = gla_chunked(heads(qg, G_KEY_DIM), heads(kg, G_KEY_DIM), heads(vg, G_VAL_DIM),
                            heads(log_f, G_KEY_DIM), s0)
    og = rms_norm(og.transpose(0, 2, 1, 3), gla_norm_g).reshape(B, L, G_WIDTH) * jax.nn.silu(gg)
    y = jnp.einsum('ble,ed->bld', jnp.concatenate([oa, og], axis=-1), w_out)
    return x + y, ka, va, s_new


def setup_inputs(seed: int = 0) -> dict:
    key = jax.random.key(seed)
    ks = jax.random.split(key, 13)
    n_pages = PAST_LEN // PAGE_SIZE
    n_used = DEC_BATCH * n_pages
    n_phys = n_used + max(1, n_used // 4)
    f32 = jnp.float32
    x_prompt = jax.random.normal(ks[0], (BATCH, SEQ, D_MODEL), f32)
    x_sample = jax.random.normal(ks[1], (DEC_BATCH, DEC_SEQ, D_MODEL), f32)
    cache_k = jax.random.normal(ks[2], (DEPTH, n_phys, PAGE_SIZE, A_HEADS, A_HEAD_DIM), f32)
    cache_v = jax.random.normal(ks[3], (DEPTH, n_phys, PAGE_SIZE, A_HEADS, A_HEAD_DIM), f32)
    state_gla = jax.random.normal(ks[4], (DEPTH, DEC_BATCH, G_HEADS, G_KEY_DIM, G_VAL_DIM), f32)
    page_table = jax.random.permutation(ks[5], n_phys)[:n_used].reshape(DEC_BATCH, n_pages).astype(jnp.int32)
    norm_g = 1.0 + 0.02 * jax.random.normal(ks[6], (DEPTH, D_MODEL), f32)
    w_in = jax.random.normal(ks[7], (DEPTH, D_MODEL, IN_WIDTH), f32) * D_MODEL ** -0.5
    w_a2 = jax.random.normal(ks[8], (DEPTH, G_GATE_RANK, G_KEY_WIDTH), f32) * G_GATE_RANK ** -0.5
    b_a2 = 0.1 * jax.random.normal(ks[9], (DEPTH, G_KEY_WIDTH), f32)
    gla_norm_g = 1.0 + 0.02 * jax.random.normal(ks[10], (DEPTH, G_VAL_DIM), f32)
    w_out = jax.random.normal(ks[11], (DEPTH, MIX_WIDTH, D_MODEL), f32) * MIX_WIDTH ** -0.5
    final_norm_g = 1.0 + 0.02 * jax.random.normal(ks[12], (D_MODEL,), f32)
    return {'x_prompt': x_prompt, 'x_sample': x_sample, 'cache_k': cache_k, 'cache_v': cache_v,
            'state_gla': state_gla, 'page_table': page_table, 'norm_g': norm_g, 'w_in': w_in,
            'w_a2': w_a2, 'b_a2': b_a2, 'gla_norm_g': gla_norm_g, 'w_out': w_out,
            'final_norm_g': final_norm_g}


def reference(x_prompt, x_sample, cache_k, cache_v, state_gla, page_table, norm_g, w_in, w_a2,
              b_a2, gla_norm_g, w_out, final_norm_g):
    Bp, Lp, _ = x_prompt.shape
    Bs, Ls, _ = x_sample.shape
    past_len = page_table.shape[1] * PAGE_SIZE
    pos_p = jnp.arange(Lp, dtype=jnp.int32)
    pos_s = past_len + jnp.arange(Ls, dtype=jnp.int32)
    s0_p = jnp.zeros((Bp, G_HEADS, G_KEY_DIM, G_VAL_DIM), x_prompt.dtype)
    hp, hs = x_prompt, x_sample
    kp_l, vp_l, sp_l, ks_l, vs_l, ss_l = [], [], [], [], [], []
    for l in range(DEPTH):
        hp, kp, vp, sp = hybrid_layer(hp, pos_p, None, None, s0_p, norm_g[l], w_in[l], w_a2[l],
                                      b_a2[l], gla_norm_g[l], w_out[l])
        k_past = cache_k[l][page_table].reshape(Bs, past_len, A_HEADS, A_HEAD_DIM)
        v_past = cache_v[l][page_table].reshape(Bs, past_len, A_HEADS, A_HEAD_DIM)
        hs, kn, vn, sn = hybrid_layer(hs, pos_s, k_past, v_past, state_gla[l], norm_g[l], w_in[l],
                                      w_a2[l], b_a2[l], gla_norm_g[l], w_out[l])
        kp_l.append(kp); vp_l.append(vp); sp_l.append(sp)
        ks_l.append(kn); vs_l.append(vn); ss_l.append(sn)
    y_prompt = rms_norm(hp, final_norm_g)
    y_sample = rms_norm(hs, final_norm_g)
    return (y_prompt, y_sample, jnp.stack(kp_l), jnp.stack(vp_l), jnp.stack(sp_l),
            jnp.stack(ks_l), jnp.stack(vs_l), jnp.stack(ss_l))
```

```python
import functools

import jax
import jax.numpy as jnp
from jax import lax
from jax.experimental import pallas as pl
from jax.experimental.pallas import tpu as pltpu

A_HEADS = 8
A_HEAD_DIM = 64
A_WIDTH = A_HEADS * A_HEAD_DIM
ROT_DIM = A_HEAD_DIM // 4
ROT_HALF = ROT_DIM // 2
ROPE_THETA = 500000.0
MOBA_BLOCK = 256
MOBA_TOPK = 3
G_HEADS = 4
G_KEY_DIM = 64
G_VAL_DIM = 128
G_KEY_WIDTH = G_HEADS * G_KEY_DIM
G_WIDTH = G_HEADS * G_VAL_DIM
G_GATE_RANK = 16
G_GATE_NORM = 16.0
G_CHUNK = 32
NORM_EPS = 1e-6
PAGE_SIZE = 128
PAGES_PER_BLOCK = MOBA_BLOCK // PAGE_SIZE

LANES = 128
VMEM_LIMIT_BYTES = 56 * 1024 * 1024

F32 = jnp.float32
BF16 = jnp.bfloat16
NEG_INF = float("-inf")

_NT = (((1,), (1,)), ((), ()))


def _dot(a, b):
    return jnp.dot(a, b, preferred_element_type=F32)


def _dot_nt(a, b):
    return lax.dot_general(a, b, _NT, preferred_element_type=F32)


def _split3(x):
    x1 = x.astype(BF16)
    r1 = x - x1.astype(F32)
    x2 = r1.astype(BF16)
    x3 = (r1 - x2.astype(F32)).astype(BF16)
    return x1, x2, x3


def _proj_in_kernel(x_ref, g_ref, w_ref, wkv_ref, wa2_ref, ba2_ref, c_ref, sa_ref, sb_ref, ct_ref, st_ref,
                    q_ref, kt_ref, vt_ref, sga_ref, qg_ref, kg_ref, vg_ref, sgg_ref, lf_ref):
    x = x_ref[...]
    ms = jnp.mean(x * x, axis=-1, keepdims=True)
    h = (x * lax.rsqrt(ms + NORM_EPS) * g_ref[...]).astype(BF16)

    def proj(lo, hi):
        return _dot(h, w_ref[:, lo:hi])

    zq = proj(0, A_WIDTH)
    c, sa, sb = c_ref[...], sa_ref[...], sb_ref[...]
    for grp in range(A_WIDTH // LANES):
        z = zq[:, grp * LANES:(grp + 1) * LANES]
        r = z * c + pltpu.roll(z, LANES - ROT_HALF, 1) * sa + pltpu.roll(z, ROT_HALF, 1) * sb
        q_ref[:, grp * LANES:(grp + 1) * LANES] = r.astype(q_ref.dtype)

    o = A_WIDTH
    zg = proj(o, o + A_WIDTH)
    sga_ref[...] = (zg * jax.nn.sigmoid(zg)).astype(sga_ref.dtype)
    o += A_WIDTH
    qg_ref[...] = proj(o, o + G_KEY_WIDTH).astype(qg_ref.dtype)
    o += G_KEY_WIDTH
    kg_ref[...] = proj(o, o + G_KEY_WIDTH).astype(kg_ref.dtype)
    o += G_KEY_WIDTH
    vg_ref[...] = proj(o, o + G_WIDTH).astype(vg_ref.dtype)
    o += G_WIDTH
    zg = proj(o, o + G_WIDTH)
    sgg_ref[...] = (zg * jax.nn.sigmoid(zg)).astype(sgg_ref.dtype)
    o += G_WIDTH
    a1 = proj(o, o + LANES)
    la = _dot(a1.astype(BF16), wa2_ref[...]) + ba2_ref[...]
    lf_ref[...] = -(jnp.maximum(-la, 0.0) + jnp.log1p(jnp.exp(-jnp.abs(la)))) * (1.0 / G_GATE_NORM)

    zt = _dot_nt(wkv_ref[...], h)
    ct, st = ct_ref[...], st_ref[...]
    for hd in range(A_HEADS):
        b0 = hd * A_HEAD_DIM
        x1 = zt[b0:b0 + ROT_HALF]
        x2 = zt[b0 + ROT_HALF:b0 + ROT_DIM]
        kt_ref[0, b0:b0 + ROT_HALF, :] = x1 * ct - x2 * st
        kt_ref[0, b0 + ROT_HALF:b0 + ROT_DIM, :] = x2 * ct + x1 * st
        kt_ref[0, b0 + ROT_DIM:b0 + A_HEAD_DIM, :] = zt[b0 + ROT_DIM:b0 + A_HEAD_DIM]
    vt_ref[0] = zt[A_WIDTH:2 * A_WIDTH]


def _proj_in(x, norm_g, w_row, w_kvt, wa2, ba2, tabs, n_batch, tm):
    n, d = x.shape
    seq = n // n_batch
    nl = seq // tm
    c_tab, sa_tab, sb_tab, ct_tab, st_tab = tabs
    wcols = w_row.shape[1]
    row = lambda i: (i, 0)
    const = lambda i: (0, 0)
    out_shapes = (
        jax.ShapeDtypeStruct((n, A_WIDTH), BF16),
        jax.ShapeDtypeStruct((n_batch, A_WIDTH, seq), F32),
        jax.ShapeDtypeStruct((n_batch, A_WIDTH, seq), F32),
        jax.ShapeDtypeStruct((n, A_WIDTH), BF16),
        jax.ShapeDtypeStruct((n, G_KEY_WIDTH), BF16),
        jax.ShapeDtypeStruct((n, G_KEY_WIDTH), BF16),
        jax.ShapeDtypeStruct((n, G_WIDTH), BF16),
        jax.ShapeDtypeStruct((n, G_WIDTH), BF16),
        jax.ShapeDtypeStruct((n, G_KEY_WIDTH), F32),
    )
    kv_spec = pl.BlockSpec((1, A_WIDTH, tm), lambda i: (i // nl, 0, i % nl))
    return pl.pallas_call(
        _proj_in_kernel,
        grid=(n // tm,),
        in_specs=[
            pl.BlockSpec((tm, d), row),
            pl.BlockSpec((1, d), const),
            pl.BlockSpec((d, wcols), const),
            pl.BlockSpec((2 * A_WIDTH, d), const),
            pl.BlockSpec((LANES, G_KEY_WIDTH), const),
            pl.BlockSpec((1, G_KEY_WIDTH), const),
            pl.BlockSpec((tm, LANES), lambda i: (i % nl, 0)),
            pl.BlockSpec((tm, LANES), lambda i: (i % nl, 0)),
            pl.BlockSpec((tm, LANES), lambda i: (i % nl, 0)),
            pl.BlockSpec((ROT_HALF, tm), lambda i: (0, i % nl)),
            pl.BlockSpec((ROT_HALF, tm), lambda i: (0, i % nl)),
        ],
        out_specs=(
            pl.BlockSpec((tm, A_WIDTH), row), kv_spec, kv_spec,
            pl.BlockSpec((tm, A_WIDTH), row),
            pl.BlockSpec((tm, G_KEY_WIDTH), row), pl.BlockSpec((tm, G_KEY_WIDTH), row),
            pl.BlockSpec((tm, G_WIDTH), row), pl.BlockSpec((tm, G_WIDTH), row),
            pl.BlockSpec((tm, G_KEY_WIDTH), row),
        ),
        out_shape=out_shapes,
        compiler_params=pltpu.CompilerParams(dimension_semantics=("arbitrary",), vmem_limit_bytes=VMEM_LIMIT_BYTES),
        name="proj_in",
    )(x, norm_g, w_row, w_kvt, wa2, ba2, c_tab, sa_tab, sb_tab, ct_tab, st_tab)


def _proj_out_kernel(a_ref, g_ref, x_ref, w_ref, fg_ref, o_ref, *, final):
    half = a_ref.shape[1]
    y = _dot(a_ref[...], w_ref[0:half, :]) + _dot(g_ref[...], w_ref[half:, :])
    xo = x_ref[...] + y
    if final:
        ms = jnp.mean(xo * xo, axis=-1, keepdims=True)
        xo = xo * lax.rsqrt(ms + NORM_EPS) * fg_ref[...]
    o_ref[...] = xo


def _proj_out(a, g, x, w, fg, tm, final):
    n, d = x.shape
    row = lambda i: (i, 0)
    const = lambda i: (0, 0)
    return pl.pallas_call(
        functools.partial(_proj_out_kernel, final=final),
        grid=(n // tm,),
        in_specs=[
            pl.BlockSpec((tm, a.shape[1]), row),
            pl.BlockSpec((tm, g.shape[1]), row),
            pl.BlockSpec((tm, d), row),
            pl.BlockSpec(w.shape, const),
            pl.BlockSpec((1, d), const),
        ],
        out_specs=pl.BlockSpec((tm, d), row),
        out_shape=jax.ShapeDtypeStruct((n, d), F32),
        compiler_params=pltpu.CompilerParams(dimension_semantics=("arbitrary",), vmem_limit_bytes=VMEM_LIMIT_BYTES),
        name="proj_out",
    )(a, g, x, w, fg)


def _moba_prompt_kernel(q_ref, kt_ref, vt_ref, sg_ref, o_ref, kb_ref, vb_ref, km_ref, sel_ref, *, nb, nsel):
    qi = pl.program_id(2)
    blk = MOBA_BLOCK
    dh = A_HEAD_DIM
    heads = LANES // dh

    @pl.when(qi == 0)
    def _prepare():
        for j in range(nb):
            kj = kt_ref[0, :, j * blk:(j + 1) * blk].T
            kb_ref[j * blk:(j + 1) * blk, :] = kj.astype(BF16)
            km_ref[j:j + 1, :] = jnp.mean(kj, axis=0, keepdims=True)
            vb_ref[j] = vt_ref[0, :, j * blk:(j + 1) * blk].astype(BF16)

    scale = dh ** -0.5
    rowf = lax.broadcasted_iota(jnp.int32, (nb, blk), 0).astype(F32)
    past = rowf < qi.astype(F32)
    kio = lax.broadcasted_iota(jnp.int32, (blk, blk), 0)
    qio = lax.broadcasted_iota(jnp.int32, (blk, blk), 1)
    causal = kio <= qio

    q2 = q_ref[0]
    qh = [q2[:, hh * dh:(hh + 1) * dh] for hh in range(heads)]

    for hh in range(heads):
        km = km_ref[:, hh * dh:(hh + 1) * dh]
        km1 = km.astype(BF16)
        km2 = (km - km1.astype(F32)).astype(BF16)
        gate = _dot_nt(km1, qh[hh]) + _dot_nt(km2, qh[hh])
        g = jnp.where(past, gate, NEG_INF)
        sel = jnp.zeros((nb, blk), F32)
        for _ in range(nsel):
            m = jnp.max(g, axis=0, keepdims=True)
            idx = jnp.min(jnp.where(g == m, rowf, float(nb)), axis=0, keepdims=True)
            pick = jnp.logical_and(rowf == idx, m > NEG_INF)
            sel = jnp.where(pick, 1.0, sel)
            g = jnp.where(pick, NEG_INF, g)
        sel_ref[hh] = sel

    def scores(j, hh):
        r0 = pl.multiple_of(j * blk, blk)
        kj = kb_ref[pl.ds(r0, blk), hh * dh:(hh + 1) * dh]
        return _dot_nt(kj, qh[hh]) * scale

    carry = []
    for hh in range(heads):
        s = jnp.where(causal, scores(qi, hh), NEG_INF)
        m = jnp.max(s, axis=0, keepdims=True)
        p = jnp.exp(s - m)
        l = jnp.sum(p, axis=0, keepdims=True)
        acc = _dot(vb_ref[qi, hh * dh:(hh + 1) * dh, :], p.astype(BF16))
        carry += [m, l, acc]

    def body(j, carry):
        out = []
        for hh in range(heads):
            m, l, acc = carry[3 * hh:3 * hh + 3]
            chosen = sel_ref[hh, pl.ds(j, 1), :] > 0.0
            s = jnp.where(chosen, scores(j, hh), NEG_INF)
            m_new = jnp.maximum(m, jnp.max(s, axis=0, keepdims=True))
            alpha = jnp.exp(m - m_new)
            p = jnp.exp(s - m_new)
            l = alpha * l + jnp.sum(p, axis=0, keepdims=True)
            acc = alpha * acc + _dot(vb_ref[j, hh * dh:(hh + 1) * dh, :], p.astype(BF16))
            out += [m_new, l, acc]
        return tuple(out)

    carry = lax.fori_loop(0, qi, body, tuple(carry))
    ot = jnp.concatenate([carry[3 * hh + 2] / carry[3 * hh + 1] for hh in range(heads)], axis=0)
    o_ref[0] = (ot.T * sg_ref[0].astype(F32)).astype(o_ref.dtype)


def _moba_prompt(q, kt, vt, sg):
    b, seq, width = q.shape
    nb = seq // MOBA_BLOCK
    nsel = min(MOBA_TOPK, nb)
    tok = pl.BlockSpec((1, MOBA_BLOCK, LANES), lambda bi, hp, qi: (bi, qi, hp))
    feat = pl.BlockSpec((1, LANES, seq), lambda bi, hp, qi: (bi, hp, 0))
    return pl.pallas_call(
        functools.partial(_moba_prompt_kernel, nb=nb, nsel=nsel),
        grid=(b, width // LANES, nb),
        in_specs=[tok, feat, feat, tok],
        out_specs=tok,
        out_shape=jax.ShapeDtypeStruct((b, seq, width), BF16),
        scratch_shapes=[
            pltpu.VMEM((seq, LANES), BF16),
            pltpu.VMEM((nb, LANES, MOBA_BLOCK), BF16),
            pltpu.VMEM((nb, LANES), F32),
            pltpu.VMEM((LANES // A_HEAD_DIM, nb, MOBA_BLOCK), F32),
        ],
        compiler_params=pltpu.CompilerParams(
            dimension_semantics=("arbitrary", "arbitrary", "arbitrary"), vmem_limit_bytes=VMEM_LIMIT_BYTES),
        name="moba_prompt",
    )(q, kt, vt, sg)


def _gla_prompt_kernel(q_ref, k_ref, v_ref, lf_ref, sg_ref, gn_ref, o_ref, s_ref, st_ref, *, nt):
    t = pl.program_id(1)
    tt = q_ref.shape[1]
    ck = G_CHUNK
    dk, dv = G_KEY_DIM, G_VAL_DIM
    pairs = G_KEY_WIDTH // LANES

    @pl.when(t == 0)
    def _init():
        st_ref[...] = jnp.zeros_like(st_ref)

    g1, g2, g3 = _split3(lf_ref[0])
    row = lax.broadcasted_iota(jnp.int32, (tt, tt), 0)
    col = lax.broadcasted_iota(jnp.int32, (tt, tt), 1)
    ck_shift = ck.bit_length() - 1
    dk_shift = dk.bit_length() - 1
    same = lax.shift_right_logical(row, ck_shift) == lax.shift_right_logical(col, ck_shift)
    lower = jnp.where(jnp.logical_and(same, col <= row), 1.0, 0.0).astype(BF16)
    upper = jnp.where(jnp.logical_and(same, col > row), 1.0, 0.0).astype(BF16)
    bcum = _dot(lower, g1) + _dot(lower, g2) + _dot(lower, g3)
    rest = _dot(upper, g1) + _dot(upper, g2) + _dot(upper, g3)

    qf = q_ref[0].astype(F32)
    kf = k_ref[0].astype(F32)
    qt = qf * jnp.exp(bcum) * (dk ** -0.5)
    kt = kf * jnp.exp(-bcum)
    kd = kf * jnp.exp(rest)

    vts = [v_ref[0, :, hd * dv:(hd + 1) * dv].astype(F32).T for hd in range(G_HEADS)]

    lane_c = lax.broadcasted_iota(jnp.int32, (ck, LANES), 1)
    lane_s = lax.broadcasted_iota(jnp.int32, (LANES, LANES), 1)
    tri = lax.broadcasted_iota(jnp.int32, (ck, ck), 1) <= lax.broadcasted_iota(jnp.int32, (ck, ck), 0)
    gn = gn_ref[...]
    states = [st_ref[p] for p in range(pairs)]

    for c in range(tt // ck):
        r0, r1 = c * ck, (c + 1) * ck
        dec = jnp.exp(bcum[r1 - 1:r1, :])
        for p in range(pairs):
            l0, l1 = p * LANES, (p + 1) * LANES
            qt_p = qt[r0:r1, l0:l1]
            kt_p = kt[r0:r1, l0:l1].astype(BF16)
            kd_p = kd[r0:r1, l0:l1].astype(BF16)
            st = states[p]
            st_b = st.astype(BF16)
            kvs = []
            for hh in range(LANES // dk):
                hd = p * (LANES // dk) + hh
                qm = jnp.where(lax.shift_right_logical(lane_c, dk_shift) == hh, qt_p, 0.0).astype(BF16)
                a = jnp.where(tri, _dot_nt(qm, kt_p), 0.0)
                v_h = v_ref[0, r0:r1, hd * dv:(hd + 1) * dv]
                o = _dot(a.astype(BF16), v_h) + _dot_nt(qm, st_b)
                ms = jnp.mean(o * o, axis=-1, keepdims=True)
                on = o * lax.rsqrt(ms + NORM_EPS) * gn
                sg = sg_ref[0, r0:r1, hd * dv:(hd + 1) * dv].astype(F32)
                o_ref[0, r0:r1, hd * dv:(hd + 1) * dv] = (on * sg).astype(o_ref.dtype)
                kvs.append(_dot(vts[hd][:, r0:r1].astype(BF16), kd_p))
            kv = jnp.where(lane_s < dk, kvs[0], kvs[1])
            states[p] = st * dec[:, l0:l1] + kv

    for p in range(pairs):
        st_ref[p] = states[p]

    @pl.when(t == nt - 1)
    def _emit():
        for p in range(pairs):
            s_pair = states[p].T
            for hh in range(LANES // dk):
                s_ref[0, p * (LANES // dk) + hh] = s_pair[hh * dk:(hh + 1) * dk, :]


def _gla_prompt(qg, kg, vg, lf, sg, gn, tt):
    b, seq, _ = qg.shape
    nt = seq // tt
    kspec = pl.BlockSpec((1, tt, G_KEY_WIDTH), lambda bi, ti: (bi, ti, 0))
    vspec = pl.BlockSpec((1, tt, G_WIDTH), lambda bi, ti: (bi, ti, 0))
    return pl.pallas_call(
        functools.partial(_gla_prompt_kernel, nt=nt),
        grid=(b, nt),
        in_specs=[kspec, kspec, vspec, kspec, vspec, pl.BlockSpec((1, G_VAL_DIM), lambda bi, ti: (0, 0))],
        out_specs=(vspec, pl.BlockSpec((1, G_HEADS, G_KEY_DIM, G_VAL_DIM), lambda bi, ti: (bi, 0, 0, 0))),
        out_shape=(jax.ShapeDtypeStruct((b, seq, G_WIDTH), BF16),
                   jax.ShapeDtypeStruct((b, G_HEADS, G_KEY_DIM, G_VAL_DIM), F32)),
        scratch_shapes=[pltpu.VMEM((G_KEY_WIDTH // LANES, G_VAL_DIM, LANES), F32)],
        compiler_params=pltpu.CompilerParams(
            dimension_semantics=("arbitrary", "arbitrary"), vmem_limit_bytes=VMEM_LIMIT_BYTES),
        name="gla_prompt",
    )(qg, kg, vg, lf, sg, gn)


def _sample_score_kernel(pt_ref, qcol_ref, kcol_ref, kc_ref, pn_ref, idx_ref, pown_ref,
                         buf_ref, sem_ref, qb_ref, sc_ref, *, layer, npages, nbuf, bs, nsel):
    b = pl.program_id(0)
    total = bs * npages
    nblk = npages // PAGES_PER_BLOCK
    dh = A_HEAD_DIM
    scale = dh ** -0.5

    def page_copy(g, slot):
        bb = g // npages
        return pltpu.make_async_copy(kc_ref.at[layer, pt_ref[bb, g - bb * npages]], buf_ref.at[slot], sem_ref.at[slot])

    @pl.when(b == 0)
    def _prime():
        for s in range(nbuf):
            page_copy(s, s).start()

    for h in range(A_HEADS):
        qb_ref[h] = jnp.broadcast_to(qcol_ref[0, h], (dh, PAGE_SIZE))

    def page_body(p, _):
        g = b * npages + p
        slot = lax.rem(g, nbuf)
        page_copy(g, slot).wait()
        for h in range(A_HEADS):
            sc_ref[p, h:h + 1, :] = jnp.sum(buf_ref[slot, h] * qb_ref[h], axis=0, keepdims=True)

        @pl.when(g + nbuf < total)
        def _next():
            page_copy(g + nbuf, slot).start()

        return 0

    lax.fori_loop(0, npages, page_body, 0)

    lane = lax.broadcasted_iota(jnp.int32, (A_HEADS, LANES), 1).astype(F32)
    sub = lax.broadcasted_iota(jnp.int32, (A_HEADS, LANES), 0)

    gate = jnp.full((A_HEADS, LANES), NEG_INF, F32)
    for j in range(nblk):
        sblk = sc_ref[PAGES_PER_BLOCK * j]
        for pp in range(1, PAGES_PER_BLOCK):
            sblk = sblk + sc_ref[PAGES_PER_BLOCK * j + pp]
        gj = jnp.sum(sblk, axis=1, keepdims=True) * (1.0 / MOBA_BLOCK)
        gate = jnp.where(lane == float(j), gj, gate)

    g = gate
    sel = jnp.zeros((A_HEADS, LANES), F32)
    idx_out = jnp.full((A_HEADS, LANES), -1.0, F32)
    for r in range(nsel):
        m = jnp.max(g, axis=1, keepdims=True)
        idx = jnp.min(jnp.where(g == m, lane, float(LANES)), axis=1, keepdims=True)
        pick = jnp.logical_and(lane == idx, m > NEG_INF)
        sel = jnp.where(pick, 1.0, sel)
        g = jnp.where(pick, NEG_INF, g)
        idx_out = jnp.where(lane == float(r), jnp.where(m > NEG_INF, idx, -1.0), idx_out)
    idx_ref[0] = idx_out.astype(jnp.int32)

    s_own = jnp.zeros((A_HEADS, LANES), F32)
    for h in range(A_HEADS):
        so = jnp.sum(qcol_ref[0, h] * kcol_ref[0, h], axis=0, keepdims=True)
        s_own = jnp.where(sub == h, jnp.broadcast_to(so, (A_HEADS, LANES)), s_own)
    s_own = s_own * scale

    masks = [jnp.broadcast_to(sel[:, j:j + 1], (A_HEADS, LANES)) > 0.0 for j in range(nblk)]
    mx = s_own
    for pg in range(npages):
        mx = jnp.maximum(mx, jnp.where(masks[pg // PAGES_PER_BLOCK], sc_ref[pg] * scale, NEG_INF))
    m = jnp.max(mx, axis=1, keepdims=True)
    lsum = jnp.zeros((A_HEADS, LANES), F32)
    for pg in range(npages):
        p = jnp.where(masks[pg // PAGES_PER_BLOCK], jnp.exp(sc_ref[pg] * scale - m), 0.0)
        pn_ref[0, pg] = p
        lsum = lsum + p
    p_own = jnp.exp(s_own - m)
    inv = 1.0 / (jnp.sum(lsum, axis=1, keepdims=True) + p_own)
    for pg in range(npages):
        pn_ref[0, pg] = pn_ref[0, pg] * inv
    pown_ref[0] = p_own * inv


def _sample_score(page_table, qcol, kcol, cache_t, layer, nbuf=8):
    bs, npages = page_table.shape
    nblk = npages // PAGES_PER_BLOCK
    nsel = min(MOBA_TOPK, nblk + 1)
    col = pl.BlockSpec((1, A_HEADS, A_HEAD_DIM, 1), lambda b, pt: (b, 0, 0, 0))
    grid_spec = pltpu.PrefetchScalarGridSpec(
        num_scalar_prefetch=1,
        grid=(bs,),
        in_specs=[col, col, pl.BlockSpec(memory_space=pl.ANY)],
        out_specs=(
            pl.BlockSpec((1, npages, A_HEADS, PAGE_SIZE), lambda b, pt: (b, 0, 0, 0)),
            pl.BlockSpec((1, A_HEADS, LANES), lambda b, pt: (b, 0, 0)),
            pl.BlockSpec((1, A_HEADS, LANES), lambda b, pt: (b, 0, 0)),
        ),
        scratch_shapes=[
            pltpu.VMEM((nbuf, A_HEADS, A_HEAD_DIM, PAGE_SIZE), F32),
            pltpu.SemaphoreType.DMA((nbuf,)),
            pltpu.VMEM((A_HEADS, A_HEAD_DIM, PAGE_SIZE), F32),
            pltpu.VMEM((npages, A_HEADS, PAGE_SIZE), F32),
        ],
    )
    return pl.pallas_call(
        functools.partial(_sample_score_kernel, layer=layer, npages=npages, nbuf=min(nbuf, npages), bs=bs, nsel=nsel),
        grid_spec=grid_spec,
        out_shape=(
            jax.ShapeDtypeStruct((bs, npages, A_HEADS, PAGE_SIZE), F32),
            jax.ShapeDtypeStruct((bs, A_HEADS, LANES), jnp.int32),
            jax.ShapeDtypeStruct((bs, A_HEADS, LANES), F32),
        ),
        compiler_params=pltpu.CompilerParams(dimension_semantics=("arbitrary",), vmem_limit_bytes=VMEM_LIMIT_BYTES),
        name="sample_score",
    )(page_table, qcol, kcol, cache_t)


def _sample_pv_kernel(pt_ref, idx_ref, pn_ref, pown_ref, vcol_ref, sgcol_ref, vc_ref, o_ref,
                      vbuf_ref, sem_ref, *, layer, nsel, bs):
    b = pl.program_id(0)
    slot = lax.rem(b, 2)

    def issue(bb, sl):
        for h in range(A_HEADS):
            for r in range(nsel):
                j = jnp.maximum(idx_ref[bb, h * nsel + r], 0)
                for pp in range(PAGES_PER_BLOCK):
                    pg = pt_ref[bb, PAGES_PER_BLOCK * j + pp]
                    pltpu.make_async_copy(vc_ref.at[layer, pg, h], vbuf_ref.at[sl, h, r, pp], sem_ref.at[sl]).start()

    @pl.when(b == 0)
    def _first():
        issue(0, 0)

    @pl.when(b + 1 < bs)
    def _prefetch():
        issue(b + 1, 1 - slot)

    for _ in range(A_HEADS * nsel * PAGES_PER_BLOCK):
        pltpu.make_async_copy(vc_ref.at[layer, 0, 0], vbuf_ref.at[slot, 0, 0, 0], sem_ref.at[slot]).wait()

    for h in range(A_HEADS):
        acc = jnp.zeros((A_HEAD_DIM, PAGE_SIZE), F32)
        for r in range(nsel):
            jraw = idx_ref[b, h * nsel + r]
            j = jnp.maximum(jraw, 0)
            w = jnp.where(jraw >= 0, 1.0, 0.0)
            for pp in range(PAGES_PER_BLOCK):
                prow = pn_ref[0, PAGES_PER_BLOCK * j + pp, h:h + 1, :] * w
                acc = acc + vbuf_ref[slot, h, r, pp] * prow
        ocol = jnp.sum(acc, axis=1, keepdims=True) + pown_ref[0, h:h + 1, 0:1] * vcol_ref[0, h]
        o_ref[0, h] = ocol * sgcol_ref[0, h]


def _sample_pv(page_table, idx, pn, pown, vcol, sgcol, cache_t, layer):
    bs, npages = page_table.shape
    nsel = idx.shape[1] // A_HEADS
    col = pl.BlockSpec((1, A_HEADS, A_HEAD_DIM, 1), lambda b, pt, ix: (b, 0, 0, 0))
    grid_spec = pltpu.PrefetchScalarGridSpec(
        num_scalar_prefetch=2,
        grid=(bs,),
        in_specs=[
            pl.BlockSpec((1, npages, A_HEADS, PAGE_SIZE), lambda b, pt, ix: (b, 0, 0, 0)),
            pl.BlockSpec((1, A_HEADS, LANES), lambda b, pt, ix: (b, 0, 0)),
            col, col,
            pl.BlockSpec(memory_space=pl.ANY),
        ],
        out_specs=col,
        scratch_shapes=[
            pltpu.VMEM((2, A_HEADS, nsel, PAGES_PER_BLOCK, A_HEAD_DIM, PAGE_SIZE), F32),
            pltpu.SemaphoreType.DMA((2,)),
        ],
    )
    return pl.pallas_call(
        functools.partial(_sample_pv_kernel, layer=layer, nsel=nsel, bs=bs),
        grid_spec=grid_spec,
        out_shape=jax.ShapeDtypeStruct((bs, A_HEADS, A_HEAD_DIM, 1), F32),
        compiler_params=pltpu.CompilerParams(dimension_semantics=("arbitrary",), vmem_limit_bytes=VMEM_LIMIT_BYTES),
        name="sample_pv",
    )(page_table, idx, pn, pown, vcol, sgcol, cache_t)


def _gla_sample_kernel(q_ref, k_ref, g_ref, v_ref, sg_ref, s0_ref, gn_ref, s_ref, o_ref):
    for h in range(G_HEADS):
        s_new = jnp.exp(g_ref[0, h]) * s0_ref[0, h] + k_ref[0, h] * v_ref[0, h]
        s_ref[0, h] = s_new
        o = jnp.sum(q_ref[0, h] * s_new, axis=0, keepdims=True) * (G_KEY_DIM ** -0.5)
        ms = jnp.mean(o * o, axis=-1, keepdims=True)
        o_ref[0, h] = o * lax.rsqrt(ms + NORM_EPS) * gn_ref[...] * sg_ref[0, h]


def _gla_sample(qcol, kcol, gcol, vrow, sgrow, s0, gn):
    bs = s0.shape[0]
    col = pl.BlockSpec((1, G_HEADS, G_KEY_DIM, 1), lambda b: (b, 0, 0, 0))
    rowspec = pl.BlockSpec((1, G_HEADS, 1, G_VAL_DIM), lambda b: (b, 0, 0, 0))
    sspec = pl.BlockSpec((1, G_HEADS, G_KEY_DIM, G_VAL_DIM), lambda b: (b, 0, 0, 0))
    return pl.pallas_call(
        _gla_sample_kernel,
        grid=(bs,),
        in_specs=[col, col, col, rowspec, rowspec, sspec, pl.BlockSpec((1, G_VAL_DIM), lambda b: (0, 0))],
        out_specs=(sspec, rowspec),
        out_shape=(jax.ShapeDtypeStruct(s0.shape, F32), jax.ShapeDtypeStruct((bs, G_HEADS, 1, G_VAL_DIM), F32)),
        compiler_params=pltpu.CompilerParams(dimension_semantics=("arbitrary",)),
        name="gla_sample",
    )(qcol, kcol, gcol, vrow, sgrow, s0, gn)


def _rope_tables(pos):
    inv = jnp.power(jnp.float32(ROPE_THETA), -jnp.arange(ROT_HALF, dtype=F32) * (2.0 / ROT_DIM))
    ang = pos.astype(F32)[:, None] * inv[None, :]
    cos, sin = jnp.cos(ang), jnp.sin(ang)
    d = jnp.arange(LANES) % A_HEAD_DIM
    f = d % ROT_HALF
    c_tab = jnp.where(d[None, :] < ROT_DIM, cos[:, f], 1.0)
    sa_tab = jnp.where(d[None, :] < ROT_HALF, -sin[:, f], 0.0)
    sb_tab = jnp.where((d[None, :] >= ROT_HALF) & (d[None, :] < ROT_DIM), sin[:, f], 0.0)
    return c_tab, sa_tab, sb_tab, cos.T, sin.T


def _pick_tile(n, pref):
    t = min(n, pref)
    while n % t:
        t //= 2
    return t


def kernel(x_prompt, x_sample, cache_k, cache_v, state_gla, page_table, norm_g, w_in, w_a2, b_a2, gla_norm_g, w_out, final_norm_g):
    bp, lp, d = x_prompt.shape
    bs, ls, _ = x_sample.shape
    depth = norm_g.shape[0]
    past_len = page_table.shape[1] * PAGE_SIZE
    assert ls == 1 and lp % MOBA_BLOCK == 0 and past_len % MOBA_BLOCK == 0
    assert cache_k.shape[2:] == (PAGE_SIZE, A_HEADS, A_HEAD_DIM)

    tabs_p = _rope_tables(jnp.arange(lp, dtype=jnp.int32))
    tabs_s = _rope_tables(jnp.full((bs,), past_len, dtype=jnp.int32))
    kc_t = jnp.transpose(cache_k, (0, 1, 3, 4, 2))
    vc_t = jnp.transpose(cache_v, (0, 1, 3, 4, 2))

    tm_p = _pick_tile(lp, 512)
    tt = _pick_tile(lp, 256)
    hp = x_prompt.reshape(bp * lp, d)
    hs = x_sample.reshape(bs, d)
    fg = final_norm_g.reshape(1, d)
    kp_l, vp_l, sp_l, ks_l, vs_l, ss_l = [], [], [], [], [], []
    rows = A_WIDTH
    for l in range(depth):
        wt = jnp.transpose(w_in[l])
        w_kvt = wt[rows:3 * rows].astype(BF16)
        a1_t = jnp.pad(wt[4 * rows + 2 * G_KEY_WIDTH + 2 * G_WIDTH:], ((0, LANES - G_GATE_RANK), (0, 0)))
        w_row = jnp.concatenate([wt[0:rows], wt[3 * rows:4 * rows + 2 * G_KEY_WIDTH + 2 * G_WIDTH], a1_t], axis=0)
        w_row = jnp.transpose(w_row).astype(BF16)
        wa2 = jnp.pad(w_a2[l], ((0, LANES - G_GATE_RANK), (0, 0))).astype(BF16)
        ba2 = b_a2[l].reshape(1, G_KEY_WIDTH)
        ng = norm_g[l].reshape(1, d)
        gn = gla_norm_g[l].reshape(1, G_VAL_DIM)
        wo = w_out[l].astype(BF16)
        last = l == depth - 1

        q, kt, vt, sga, qg, kg, vg, sgg, lf = _proj_in(hp, ng, w_row, w_kvt, wa2, ba2, tabs_p, bp, tm_p)
        oa = _moba_prompt(q.reshape(bp, lp, A_WIDTH), kt, vt, sga.reshape(bp, lp, A_WIDTH))
        og, s_fin = _gla_prompt(qg.reshape(bp, lp, -1), kg.reshape(bp, lp, -1), vg.reshape(bp, lp, -1),
                                lf.reshape(bp, lp, -1), sgg.reshape(bp, lp, -1), gn, tt)
        hp = _proj_out(oa.reshape(bp * lp, A_WIDTH), og.reshape(bp * lp, G_WIDTH), hp, wo, fg, tm_p, last)
        kp_l.append(jnp.transpose(kt.reshape(bp, A_HEADS, A_HEAD_DIM, lp), (0, 3, 1, 2)))
        vp_l.append(jnp.transpose(vt.reshape(bp, A_HEADS, A_HEAD_DIM, lp), (0, 3, 1, 2)))
        sp_l.append(s_fin)

        q, kt, vt, sga, qg, kg, vg, sgg, lf = _proj_in(hs, ng, w_row, w_kvt, wa2, ba2, tabs_s, 1, bs)
        k_new = jnp.transpose(kt[0])
        v_new = jnp.transpose(vt[0])
        as_col = lambda a: a.astype(F32).reshape(bs, A_HEADS, A_HEAD_DIM, 1)
        pn, idx, pown = _sample_score(page_table, as_col(q), as_col(k_new), kc_t, l)
        nsel = min(MOBA_TOPK, page_table.shape[1] // PAGES_PER_BLOCK + 1)
        idx_s = idx[:, :, :nsel].reshape(bs, A_HEADS * nsel)
        oa_s = _sample_pv(page_table, idx_s, pn, pown, as_col(v_new), as_col(sga), vc_t, l)
        g_col = lambda a: a.astype(F32).reshape(bs, G_HEADS, G_KEY_DIM, 1)
        g_row = lambda a: a.astype(F32).reshape(bs, G_HEADS, 1, G_VAL_DIM)
        s_new, og_s = _gla_sample(g_col(qg), g_col(kg), g_col(lf), g_row(vg), g_row(sgg), state_gla[l], gn)
        hs = _proj_out(oa_s.reshape(bs, A_WIDTH).astype(BF16), og_s.reshape(bs, G_WIDTH).astype(BF16),
                       hs, wo, fg, bs, last)
        ks_l.append(k_new.reshape(bs, 1, A_HEADS, A_HEAD_DIM))
        vs_l.append(v_new.reshape(bs, 1, A_HEADS, A_HEAD_DIM))
        ss_l.append(s_new)

    return (hp.reshape(bp, lp, d), hs.reshape(bs, ls, d), jnp.stack(kp_l), jnp.stack(vp_l), jnp.stack(sp_l),
            jnp.stack(ks_l), jnp.stack(vs_l), jnp.stack(ss_l))
```

```python
import functools

import jax
import jax.numpy as jnp
from jax import lax
from jax.experimental import pallas as pl
from jax.experimental.pallas import tpu as pltpu

A_HEADS = 8
A_HEAD_DIM = 64
A_WIDTH = A_HEADS * A_HEAD_DIM
ROT_DIM = A_HEAD_DIM // 4
ROT_HALF = ROT_DIM // 2
ROPE_THETA = 500000.0
MOBA_BLOCK = 256
MOBA_TOPK = 3
G_HEADS = 4
G_KEY_DIM = 64
G_VAL_DIM = 128
G_KEY_WIDTH = G_HEADS * G_KEY_DIM
G_WIDTH = G_HEADS * G_VAL_DIM
G_GATE_RANK = 16
G_GATE_NORM = 16.0
G_CHUNK = 32
NORM_EPS = 1e-6
PAGE_SIZE = 128
PAGES_PER_BLOCK = MOBA_BLOCK // PAGE_SIZE

LANES = 128
VMEM_LIMIT_BYTES = 56 * 1024 * 1024

F32 = jnp.float32
BF16 = jnp.bfloat16
NEG_INF = float("-inf")
MASK_BIAS = -1e30
MOBA_HEADS_PER_STEP = 4
SAMPLE_PAGES_IN_FLIGHT = 24
V_PAD_ROWS = 16

_NT = (((1,), (1,)), ((), ()))


def _dot(a, b):
    return jnp.dot(a, b, preferred_element_type=F32)


def _dot_nt(a, b):
    return lax.dot_general(a, b, _NT, preferred_element_type=F32)


def _split3(x):
    x1 = x.astype(BF16)
    r1 = x - x1.astype(F32)
    x2 = r1.astype(BF16)
    x3 = (r1 - x2.astype(F32)).astype(BF16)
    return x1, x2, x3


def _proj_in_kernel(x_ref, g_ref, w_ref, wqkv_ref, wa2_ref, ba2_ref, ct_ref, st_ref,
                    qt_ref, kt_ref, vt_ref, sga_ref, qg_ref, kg_ref, vg_ref, sgg_ref, lf_ref):
    x = x_ref[...]
    ms = jnp.mean(x * x, axis=-1, keepdims=True)
    h = (x * lax.rsqrt(ms + NORM_EPS) * g_ref[...]).astype(BF16)

    def proj(lo, hi):
        return _dot(h, w_ref[:, lo:hi])

    o = 0
    zg = proj(o, o + A_WIDTH)
    sga_ref[...] = (zg * jax.nn.sigmoid(zg)).astype(sga_ref.dtype)
    o += A_WIDTH
    qg_ref[...] = proj(o, o + G_KEY_WIDTH).astype(qg_ref.dtype)
    o += G_KEY_WIDTH
    kg_ref[...] = proj(o, o + G_KEY_WIDTH).astype(kg_ref.dtype)
    o += G_KEY_WIDTH
    vg_ref[...] = proj(o, o + G_WIDTH).astype(vg_ref.dtype)
    o += G_WIDTH
    zg = proj(o, o + G_WIDTH)
    sgg_ref[...] = (zg * jax.nn.sigmoid(zg)).astype(sgg_ref.dtype)
    o += G_WIDTH
    a1 = proj(o, o + LANES)
    la = _dot(a1.astype(BF16), wa2_ref[...]) + ba2_ref[...]
    lf_ref[...] = -(jnp.maximum(-la, 0.0) + jnp.log1p(jnp.exp(-jnp.abs(la)))) * (1.0 / G_GATE_NORM)

    zt = _dot_nt(wqkv_ref[...], h)
    ct, st = ct_ref[...], st_ref[...]
    q_scale = A_HEAD_DIM ** -0.5
    for hd in range(A_HEADS):
        b0 = hd * A_HEAD_DIM
        x1 = zt[b0:b0 + ROT_HALF]
        x2 = zt[b0 + ROT_HALF:b0 + ROT_DIM]
        qh = jnp.concatenate([x1 * ct - x2 * st, x2 * ct + x1 * st, zt[b0 + ROT_DIM:b0 + A_HEAD_DIM]], axis=0)
        qt_ref[0, b0:b0 + A_HEAD_DIM, :] = (qh * q_scale).astype(qt_ref.dtype)
        b0 += A_WIDTH
        x1 = zt[b0:b0 + ROT_HALF]
        x2 = zt[b0 + ROT_HALF:b0 + ROT_DIM]
        k0 = hd * A_HEAD_DIM
        kt_ref[0, k0:k0 + ROT_HALF, :] = x1 * ct - x2 * st
        kt_ref[0, k0 + ROT_HALF:k0 + ROT_DIM, :] = x2 * ct + x1 * st
        kt_ref[0, k0 + ROT_DIM:k0 + A_HEAD_DIM, :] = zt[b0 + ROT_DIM:b0 + A_HEAD_DIM]
    vt_ref[0] = zt[2 * A_WIDTH:3 * A_WIDTH]


def _proj_in(x, norm_g, w_row, w_qkvt, wa2, ba2, tabs, n_batch, tm):
    n, d = x.shape
    seq = n // n_batch
    nl = seq // tm
    ct_tab, st_tab = tabs
    wcols = w_row.shape[1]
    row = lambda i: (i, 0)
    const = lambda i: (0, 0)
    out_shapes = (
        jax.ShapeDtypeStruct((n_batch, A_WIDTH, seq), BF16),
        jax.ShapeDtypeStruct((n_batch, A_WIDTH, seq), F32),
        jax.ShapeDtypeStruct((n_batch, A_WIDTH, seq), F32),
        jax.ShapeDtypeStruct((n, A_WIDTH), BF16),
        jax.ShapeDtypeStruct((n, G_KEY_WIDTH), BF16),
        jax.ShapeDtypeStruct((n, G_KEY_WIDTH), BF16),
        jax.ShapeDtypeStruct((n, G_WIDTH), BF16),
        jax.ShapeDtypeStruct((n, G_WIDTH), BF16),
        jax.ShapeDtypeStruct((n, G_KEY_WIDTH), F32),
    )
    kv_spec = pl.BlockSpec((1, A_WIDTH, tm), lambda i: (i // nl, 0, i % nl))
    return pl.pallas_call(
        _proj_in_kernel,
        grid=(n // tm,),
        in_specs=[
            pl.BlockSpec((tm, d), row),
            pl.BlockSpec((1, d), const),
            pl.BlockSpec((d, wcols), const),
            pl.BlockSpec((3 * A_WIDTH, d), const),
            pl.BlockSpec((LANES, G_KEY_WIDTH), const),
            pl.BlockSpec((1, G_KEY_WIDTH), const),
            pl.BlockSpec((ROT_HALF, tm), lambda i: (0, i % nl)),
            pl.BlockSpec((ROT_HALF, tm), lambda i: (0, i % nl)),
        ],
        out_specs=(
            kv_spec, kv_spec, kv_spec,
            pl.BlockSpec((tm, A_WIDTH), row),
            pl.BlockSpec((tm, G_KEY_WIDTH), row), pl.BlockSpec((tm, G_KEY_WIDTH), row),
            pl.BlockSpec((tm, G_WIDTH), row), pl.BlockSpec((tm, G_WIDTH), row),
            pl.BlockSpec((tm, G_KEY_WIDTH), row),
        ),
        out_shape=out_shapes,
        compiler_params=pltpu.CompilerParams(dimension_semantics=("arbitrary",), vmem_limit_bytes=VMEM_LIMIT_BYTES),
        name="proj_in",
    )(x, norm_g, w_row, w_qkvt, wa2, ba2, ct_tab, st_tab)


def _proj_out_kernel(a_ref, g_ref, x_ref, w_ref, fg_ref, o_ref, *, final):
    half = a_ref.shape[1]
    y = _dot(a_ref[...], w_ref[0:half, :]) + _dot(g_ref[...], w_ref[half:, :])
    xo = x_ref[...] + y
    if final:
        ms = jnp.mean(xo * xo, axis=-1, keepdims=True)
        xo = xo * lax.rsqrt(ms + NORM_EPS) * fg_ref[...]
    o_ref[...] = xo


def _proj_out(a, g, x, w, fg, tm, final):
    n, d = x.shape
    row = lambda i: (i, 0)
    const = lambda i: (0, 0)
    return pl.pallas_call(
        functools.partial(_proj_out_kernel, final=final),
        grid=(n // tm,),
        in_specs=[
            pl.BlockSpec((tm, a.shape[1]), row),
            pl.BlockSpec((tm, g.shape[1]), row),
            pl.BlockSpec((tm, d), row),
            pl.BlockSpec(w.shape, const),
            pl.BlockSpec((1, d), const),
        ],
        out_specs=pl.BlockSpec((tm, d), row),
        out_shape=jax.ShapeDtypeStruct((n, d), F32),
        compiler_params=pltpu.CompilerParams(dimension_semantics=("arbitrary",), vmem_limit_bytes=VMEM_LIMIT_BYTES),
        name="proj_out",
    )(a, g, x, w, fg)


def _moba_prompt_kernel(qt_ref, kt_ref, vt_ref, sg_ref, o_ref, ka_ref, va_ref, km_ref, qa_ref, s_ref, acc_ref,
                        *, nb, nsel, unroll):
    qi = pl.program_id(2)
    blk = MOBA_BLOCK
    dh = A_HEAD_DIM
    heads = qt_ref.shape[1] // dh
    per_group = LANES // dh

    @pl.when(qi == 0)
    def _prepare():
        lane = lax.broadcasted_iota(jnp.int32, (blk, LANES), 1)
        ones_rows = jnp.where(lax.broadcasted_iota(jnp.int32, (V_PAD_ROWS, blk), 0) == 0, 1.0, 0.0)
        for j in range(nb):
            kj = kt_ref[0, :, j * blk:(j + 1) * blk].T
            km_ref[j:j + 1, :] = jnp.mean(kj, axis=0, keepdims=True)
            onehot = jnp.where(lane == dh + j, 1.0, 0.0)
            for hh in range(heads):
                grp, sub = divmod(hh, per_group)
                kh = kj[:, grp * LANES:(grp + 1) * LANES]
                if sub:
                    kh = pltpu.roll(kh, LANES - sub * dh, 1)
                ka_ref[hh, j * blk:(j + 1) * blk, :] = jnp.where(lane < dh, kh, onehot).astype(BF16)
                vh = vt_ref[0, hh * dh:(hh + 1) * dh, j * blk:(j + 1) * blk]
                va_ref[j, hh] = jnp.concatenate([vh, ones_rows], axis=0).astype(BF16)

    rowf = lax.broadcasted_iota(jnp.int32, (nb, blk), 0).astype(F32)
    past = rowf < qi.astype(F32)
    causal = lax.broadcasted_iota(jnp.int32, (blk, blk), 0) <= lax.broadcasted_iota(jnp.int32, (blk, blk), 1)
    pad = jnp.zeros((LANES - dh - nb, blk), BF16)
    r_own = pl.multiple_of(qi * blk, blk)

    m_run, qd = [], []
    for hh in range(heads):
        qh = qt_ref[0, hh * dh:(hh + 1) * dh, :]
        km = km_ref[:, hh * dh:(hh + 1) * dh]
        km1 = km.astype(BF16)
        km2 = (km - km1.astype(F32)).astype(BF16)
        g = jnp.where(past, _dot(km1, qh) + _dot(km2, qh), NEG_INF)
        sel = jnp.zeros((nb, blk), F32)
        for _ in range(nsel):
            m = jnp.max(g, axis=0, keepdims=True)
            idx = jnp.min(jnp.where(g == m, rowf, float(nb)), axis=0, keepdims=True)
            pick = jnp.logical_and(rowf == idx, m > NEG_INF)
            sel = jnp.where(pick, 1.0, sel)
            g = jnp.where(pick, NEG_INF, g)
        bias = jnp.where(sel > 0.0, 0.0, MASK_BIAS).astype(BF16)
        qa_ref[hh] = jnp.concatenate([qh, bias, pad], axis=0)
        qd.append(jnp.concatenate([qh, jnp.zeros((LANES - dh, blk), BF16)], axis=0))

    for hh in range(heads):
        s = jnp.where(causal, _dot(ka_ref[hh, pl.ds(r_own, blk), :], qd[hh]), MASK_BIAS)
        s_ref[hh, nb] = s
        m_run.append(jnp.max(s, axis=0, keepdims=True))

    groups = lax.shift_right_logical(qi + (unroll - 1), unroll.bit_length() - 1)

    def pass1(gi, ms):
        ms = list(ms)
        for u in range(unroll):
            j = gi * unroll + u
            r0 = pl.multiple_of(j * blk, blk)
            for hh in range(heads):
                s = _dot(ka_ref[hh, pl.ds(r0, blk), :], qa_ref[hh])
                s_ref[hh, j] = s
                ms[hh] = jnp.maximum(ms[hh], jnp.max(s, axis=0, keepdims=True))
        return tuple(ms)

    m_fin = lax.fori_loop(0, groups, pass1, tuple(m_run))

    for hh in range(heads):
        acc_ref[hh] = _dot(va_ref[qi, hh], jnp.exp(s_ref[hh, nb] - m_fin[hh]).astype(BF16))

    def pass2(gi, carry):
        for hh in range(heads):
            part = None
            for u in range(unroll):
                j = gi * unroll + u
                p = jnp.exp(s_ref[hh, j] - m_fin[hh]).astype(BF16)
                t = _dot(va_ref[j, hh], p)
                part = t if part is None else part + t
            acc_ref[hh] += part
        return carry

    lax.fori_loop(0, groups, pass2, 0)
    ot = jnp.concatenate([acc_ref[hh, 0:dh, :] / acc_ref[hh, dh:dh + 1, :] for hh in range(heads)], axis=0)
    o_ref[0] = (ot.T * sg_ref[0].astype(F32)).astype(o_ref.dtype)


def _moba_prompt(qt, kt, vt, sg):
    b, width, seq = qt.shape
    nb = seq // MOBA_BLOCK
    nsel = min(MOBA_TOPK, nb)
    heads = MOBA_HEADS_PER_STEP
    hw = heads * A_HEAD_DIM
    assert A_HEAD_DIM + nb <= LANES and width % hw == 0
    unroll = 2 if nb % 2 == 0 else 1
    tok = pl.BlockSpec((1, MOBA_BLOCK, hw), lambda bi, hp, qi: (bi, qi, hp))
    feat = pl.BlockSpec((1, hw, seq), lambda bi, hp, qi: (bi, hp, 0))
    return pl.pallas_call(
        functools.partial(_moba_prompt_kernel, nb=nb, nsel=nsel, unroll=unroll),
        grid=(b, width // hw, nb),
        in_specs=[pl.BlockSpec((1, hw, MOBA_BLOCK), lambda bi, hp, qi: (bi, hp, qi)), feat, feat, tok],
        out_specs=tok,
        out_shape=jax.ShapeDtypeStruct((b, seq, width), BF16),
        scratch_shapes=[
            pltpu.VMEM((heads, seq, LANES), BF16),
            pltpu.VMEM((nb, heads, A_HEAD_DIM + V_PAD_ROWS, MOBA_BLOCK), BF16),
            pltpu.VMEM((nb, hw), F32),
            pltpu.VMEM((heads, LANES, MOBA_BLOCK), BF16),
            pltpu.VMEM((heads, nb + 1, MOBA_BLOCK, MOBA_BLOCK), F32),
            pltpu.VMEM((heads, A_HEAD_DIM + V_PAD_ROWS, MOBA_BLOCK), F32),
        ],
        compiler_params=pltpu.CompilerParams(
            dimension_semantics=("arbitrary", "arbitrary", "arbitrary"), vmem_limit_bytes=VMEM_LIMIT_BYTES),
        name="moba_prompt",
    )(qt, kt, vt, sg)


def _gla_prompt_kernel(q_ref, k_ref, v_ref, lf_ref, sg_ref, gn_ref, o_ref, s_ref, st_ref, *, nt):
    t = pl.program_id(1)
    tt = q_ref.shape[1]
    ck = G_CHUNK
    dk, dv = G_KEY_DIM, G_VAL_DIM
    nh = G_HEADS

    @pl.when(t == 0)
    def _init():
        st_ref[...] = jnp.zeros_like(st_ref)

    g1, g2, g3 = _split3(lf_ref[0])
    row = lax.broadcasted_iota(jnp.int32, (tt, tt), 0)
    col = lax.broadcasted_iota(jnp.int32, (tt, tt), 1)
    ck_shift = ck.bit_length() - 1
    dk_shift = dk.bit_length() - 1
    same = lax.shift_right_logical(row, ck_shift) == lax.shift_right_logical(col, ck_shift)
    lower = jnp.where(jnp.logical_and(same, col <= row), 1.0, 0.0).astype(BF16)
    upper = jnp.where(jnp.logical_and(same, col > row), 1.0, 0.0).astype(BF16)
    bcum = _dot(lower, g1) + _dot(lower, g2) + _dot(lower, g3)
    rest = _dot(upper, g1) + _dot(upper, g2) + _dot(upper, g3)

    qf = q_ref[0].astype(F32)
    kf = k_ref[0].astype(F32)
    qt = qf * jnp.exp(bcum) * (dk ** -0.5)
    kt = kf * jnp.exp(-bcum)
    kd = kf * jnp.exp(rest)

    head_of_lane = lax.shift_right_logical(lax.broadcasted_iota(jnp.int32, (ck, nh * dk), 1), dk_shift)
    own = [head_of_lane == hd for hd in range(nh)]
    ri = lax.broadcasted_iota(jnp.int32, (nh * ck, nh * ck), 0)
    ci = lax.broadcasted_iota(jnp.int32, (nh * ck, nh * ck), 1)
    causal = jnp.bitwise_and(ri, ck - 1) >= jnp.bitwise_and(ci, ck - 1)
    gn = gn_ref[...]
    st = st_ref[...]

    def stack(x):
        return jnp.concatenate([jnp.where(own[hd], x, 0.0) for hd in range(nh)], axis=0).astype(BF16)

    for c in range(tt // ck):
        r0, r1 = c * ck, (c + 1) * ck
        q4, k4, kd4 = stack(qt[r0:r1]), stack(kt[r0:r1]), stack(kd[r0:r1])
        v4 = jnp.concatenate([v_ref[0, r0:r1, hd * dv:(hd + 1) * dv] for hd in range(nh)], axis=0)
        a = jnp.where(causal, _dot_nt(q4, k4), 0.0)
        o = _dot(a.astype(BF16), v4) + _dot_nt(q4, st.astype(BF16))
        ms = jnp.mean(o * o, axis=-1, keepdims=True)
        on = o * lax.rsqrt(ms + NORM_EPS) * gn
        for hd in range(nh):
            sg = sg_ref[0, r0:r1, hd * dv:(hd + 1) * dv].astype(F32)
            o_ref[0, r0:r1, hd * dv:(hd + 1) * dv] = (on[hd * ck:(hd + 1) * ck] * sg).astype(o_ref.dtype)
        kv = lax.dot_general(v4, kd4, (((0,), (0,)), ((), ())), preferred_element_type=F32)
        st = st * jnp.exp(bcum[r1 - 1:r1, :]) + kv

    st_ref[...] = st

    @pl.when(t == nt - 1)
    def _emit():
        per_group = LANES // dk
        for grp in range(nh // per_group):
            s_grp = st[:, grp * LANES:(grp + 1) * LANES].T
            for hh in range(per_group):
                s_ref[0, grp * per_group + hh] = s_grp[hh * dk:(hh + 1) * dk, :]


def _gla_prompt(qg, kg, vg, lf, sg, gn, tt):
    b, seq, _ = qg.shape
    nt = seq // tt
    kspec = pl.BlockSpec((1, tt, G_KEY_WIDTH), lambda bi, ti: (bi, ti, 0))
    vspec = pl.BlockSpec((1, tt, G_WIDTH), lambda bi, ti: (bi, ti, 0))
    return pl.pallas_call(
        functools.partial(_gla_prompt_kernel, nt=nt),
        grid=(b, nt),
        in_specs=[kspec, kspec, vspec, kspec, vspec, pl.BlockSpec((1, G_VAL_DIM), lambda bi, ti: (0, 0))],
        out_specs=(vspec, pl.BlockSpec((1, G_HEADS, G_KEY_DIM, G_VAL_DIM), lambda bi, ti: (bi, 0, 0, 0))),
        out_shape=(jax.ShapeDtypeStruct((b, seq, G_WIDTH), BF16),
                   jax.ShapeDtypeStruct((b, G_HEADS, G_KEY_DIM, G_VAL_DIM), F32)),
        scratch_shapes=[pltpu.VMEM((G_VAL_DIM, G_KEY_WIDTH), F32)],
        compiler_params=pltpu.CompilerParams(
            dimension_semantics=("arbitrary", "arbitrary"), vmem_limit_bytes=VMEM_LIMIT_BYTES),
        name="gla_prompt",
    )(qg, kg, vg, lf, sg, gn)


def _sample_score_kernel(pt_ref, qcol_ref, kcol_ref, kc_ref, pn_ref, idx_ref, pown_ref,
                         buf_ref, sem_ref, qb_ref, sc_ref, *, layer, npages, nbuf, bs, nsel):
    b = pl.program_id(0)
    total = bs * npages
    nblk = npages // PAGES_PER_BLOCK
    dh = A_HEAD_DIM

    def page_copy(g, slot):
        bb = g // npages
        return pltpu.make_async_copy(kc_ref.at[layer, pt_ref[bb, g - bb * npages]], buf_ref.at[slot], sem_ref.at[slot])

    @pl.when(b == 0)
    def _prime():
        for s in range(nbuf):
            page_copy(s, s).start()

    for h in range(A_HEADS):
        qb_ref[h] = jnp.broadcast_to(qcol_ref[0, h], (dh, PAGE_SIZE))

    def page_body(p, _):
        g = b * npages + p
        slot = lax.rem(g, nbuf)
        page_copy(g, slot).wait()
        for h in range(A_HEADS):
            sc_ref[p, h:h + 1, :] = jnp.sum(buf_ref[slot, h] * qb_ref[h], axis=0, keepdims=True)

        @pl.when(g + nbuf < total)
        def _next():
            page_copy(g + nbuf, slot).start()

        return 0

    lax.fori_loop(0, npages, page_body, 0)

    lane = lax.broadcasted_iota(jnp.int32, (A_HEADS, LANES), 1).astype(F32)
    sub = lax.broadcasted_iota(jnp.int32, (A_HEADS, LANES), 0)

    gate = jnp.full((A_HEADS, LANES), NEG_INF, F32)
    for j in range(nblk):
        sblk = sc_ref[PAGES_PER_BLOCK * j]
        for pp in range(1, PAGES_PER_BLOCK):
            sblk = sblk + sc_ref[PAGES_PER_BLOCK * j + pp]
        gj = jnp.sum(sblk, axis=1, keepdims=True) * (1.0 / MOBA_BLOCK)
        gate = jnp.where(lane == float(j), gj, gate)

    g = gate
    sel = jnp.zeros((A_HEADS, LANES), F32)
    idx_out = jnp.full((A_HEADS, LANES), -1.0, F32)
    for r in range(nsel):
        m = jnp.max(g, axis=1, keepdims=True)
        idx = jnp.min(jnp.where(g == m, lane, float(LANES)), axis=1, keepdims=True)
        pick = jnp.logical_and(lane == idx, m > NEG_INF)
        sel = jnp.where(pick, 1.0, sel)
        g = jnp.where(pick, NEG_INF, g)
        idx_out = jnp.where(lane == float(r), jnp.where(m > NEG_INF, idx, -1.0), idx_out)
    idx_ref[0] = idx_out.astype(jnp.int32)

    s_own = jnp.zeros((A_HEADS, LANES), F32)
    for h in range(A_HEADS):
        so = jnp.sum(qcol_ref[0, h] * kcol_ref[0, h], axis=0, keepdims=True)
        s_own = jnp.where(sub == h, jnp.broadcast_to(so, (A_HEADS, LANES)), s_own)

    masks = [jnp.broadcast_to(sel[:, j:j + 1], (A_HEADS, LANES)) > 0.0 for j in range(nblk)]
    mx = s_own
    for pg in range(npages):
        mx = jnp.maximum(mx, jnp.where(masks[pg // PAGES_PER_BLOCK], sc_ref[pg], NEG_INF))
    m = jnp.max(mx, axis=1, keepdims=True)
    lsum = jnp.zeros((A_HEADS, LANES), F32)
    for pg in range(npages):
        p = jnp.where(masks[pg // PAGES_PER_BLOCK], jnp.exp(sc_ref[pg] - m), 0.0)
        pn_ref[0, pg] = p
        lsum = lsum + p
    p_own = jnp.exp(s_own - m)
    inv = 1.0 / (jnp.sum(lsum, axis=1, keepdims=True) + p_own)
    for pg in range(npages):
        pn_ref[0, pg] = pn_ref[0, pg] * inv
    pown_ref[0] = p_own * inv


def _sample_score(page_table, qcol, kcol, cache_t, layer, nbuf=SAMPLE_PAGES_IN_FLIGHT):
    bs, npages = page_table.shape
    nblk = npages // PAGES_PER_BLOCK
    nsel = min(MOBA_TOPK, nblk + 1)
    col = pl.BlockSpec((1, A_HEADS, A_HEAD_DIM, 1), lambda b, pt: (b, 0, 0, 0))
    grid_spec = pltpu.PrefetchScalarGridSpec(
        num_scalar_prefetch=1,
        grid=(bs,),
        in_specs=[col, col, pl.BlockSpec(memory_space=pl.ANY)],
        out_specs=(
            pl.BlockSpec((1, npages, A_HEADS, PAGE_SIZE), lambda b, pt: (b, 0, 0, 0)),
            pl.BlockSpec((1, A_HEADS, LANES), lambda b, pt: (b, 0, 0)),
            pl.BlockSpec((1, A_HEADS, LANES), lambda b, pt: (b, 0, 0)),
        ),
        scratch_shapes=[
            pltpu.VMEM((nbuf, A_HEADS, A_HEAD_DIM, PAGE_SIZE), F32),
            pltpu.SemaphoreType.DMA((nbuf,)),
            pltpu.VMEM((A_HEADS, A_HEAD_DIM, PAGE_SIZE), F32),
            pltpu.VMEM((npages, A_HEADS, PAGE_SIZE), F32),
        ],
    )
    return pl.pallas_call(
        functools.partial(_sample_score_kernel, layer=layer, npages=npages, nbuf=min(nbuf, npages), bs=bs, nsel=nsel),
        grid_spec=grid_spec,
        out_shape=(
            jax.ShapeDtypeStruct((bs, npages, A_HEADS, PAGE_SIZE), F32),
            jax.ShapeDtypeStruct((bs, A_HEADS, LANES), jnp.int32),
            jax.ShapeDtypeStruct((bs, A_HEADS, LANES), F32),
        ),
        compiler_params=pltpu.CompilerParams(dimension_semantics=("arbitrary",), vmem_limit_bytes=VMEM_LIMIT_BYTES),
        name="sample_score",
    )(page_table, qcol, kcol, cache_t)


def _sample_pv_kernel(pt_ref, idx_ref, pn_ref, pown_ref, vcol_ref, sgcol_ref, vc_ref, o_ref,
                      vbuf_ref, sem_ref, *, layer, nsel, bs):
    b = pl.program_id(0)
    slot = lax.rem(b, 2)

    def issue(bb, sl):
        for h in range(A_HEADS):
            for r in range(nsel):
                j = jnp.maximum(idx_ref[bb, h * nsel + r], 0)
                for pp in range(PAGES_PER_BLOCK):
                    pg = pt_ref[bb, PAGES_PER_BLOCK * j + pp]
                    pltpu.make_async_copy(vc_ref.at[layer, pg, h], vbuf_ref.at[sl, h, r, pp], sem_ref.at[sl]).start()

    @pl.when(b == 0)
    def _first():
        issue(0, 0)

    @pl.when(b + 1 < bs)
    def _prefetch():
        issue(b + 1, 1 - slot)

    for _ in range(A_HEADS * nsel * PAGES_PER_BLOCK):
        pltpu.make_async_copy(vc_ref.at[layer, 0, 0], vbuf_ref.at[slot, 0, 0, 0], sem_ref.at[slot]).wait()

    for h in range(A_HEADS):
        acc = jnp.zeros((A_HEAD_DIM, PAGE_SIZE), F32)
        for r in range(nsel):
            jraw = idx_ref[b, h * nsel + r]
            j = jnp.maximum(jraw, 0)
            w = jnp.where(jraw >= 0, 1.0, 0.0)
            for pp in range(PAGES_PER_BLOCK):
                prow = pn_ref[0, PAGES_PER_BLOCK * j + pp, h:h + 1, :] * w
                acc = acc + vbuf_ref[slot, h, r, pp] * prow
        ocol = jnp.sum(acc, axis=1, keepdims=True) + pown_ref[0, h:h + 1, 0:1] * vcol_ref[0, h]
        o_ref[0, h] = ocol * sgcol_ref[0, h]


def _sample_pv(page_table, idx, pn, pown, vcol, sgcol, cache_t, layer):
    bs, npages = page_table.shape
    nsel = idx.shape[1] // A_HEADS
    col = pl.BlockSpec((1, A_HEADS, A_HEAD_DIM, 1), lambda b, pt, ix: (b, 0, 0, 0))
    grid_spec = pltpu.PrefetchScalarGridSpec(
        num_scalar_prefetch=2,
        grid=(bs,),
        in_specs=[
            pl.BlockSpec((1, npages, A_HEADS, PAGE_SIZE), lambda b, pt, ix: (b, 0, 0, 0)),
            pl.BlockSpec((1, A_HEADS, LANES), lambda b, pt, ix: (b, 0, 0)),
            col, col,
            pl.BlockSpec(memory_space=pl.ANY),
        ],
        out_specs=col,
        scratch_shapes=[
            pltpu.VMEM((2, A_HEADS, nsel, PAGES_PER_BLOCK, A_HEAD_DIM, PAGE_SIZE), F32),
            pltpu.SemaphoreType.DMA((2,)),
        ],
    )
    return pl.pallas_call(
        functools.partial(_sample_pv_kernel, layer=layer, nsel=nsel, bs=bs),
        grid_spec=grid_spec,
        out_shape=jax.ShapeDtypeStruct((bs, A_HEADS, A_HEAD_DIM, 1), F32),
        compiler_params=pltpu.CompilerParams(dimension_semantics=("arbitrary",), vmem_limit_bytes=VMEM_LIMIT_BYTES),
        name="sample_pv",
    )(page_table, idx, pn, pown, vcol, sgcol, cache_t)


def _gla_sample_kernel(q_ref, k_ref, g_ref, v_ref, sg_ref, s0_ref, gn_ref, s_ref, o_ref):
    for h in range(G_HEADS):
        s_new = jnp.exp(g_ref[0, h]) * s0_ref[0, h] + k_ref[0, h] * v_ref[0, h]
        s_ref[0, h] = s_new
        o = jnp.sum(q_ref[0, h] * s_new, axis=0, keepdims=True) * (G_KEY_DIM ** -0.5)
        ms = jnp.mean(o * o, axis=-1, keepdims=True)
        o_ref[0, h] = o * lax.rsqrt(ms + NORM_EPS) * gn_ref[...] * sg_ref[0, h]


def _gla_sample(qcol, kcol, gcol, vrow, sgrow, s0, gn):
    bs = s0.shape[0]
    col = pl.BlockSpec((1, G_HEADS, G_KEY_DIM, 1), lambda b: (b, 0, 0, 0))
    rowspec = pl.BlockSpec((1, G_HEADS, 1, G_VAL_DIM), lambda b: (b, 0, 0, 0))
    sspec = pl.BlockSpec((1, G_HEADS, G_KEY_DIM, G_VAL_DIM), lambda b: (b, 0, 0, 0))
    return pl.pallas_call(
        _gla_sample_kernel,
        grid=(bs,),
        in_specs=[col, col, col, rowspec, rowspec, sspec, pl.BlockSpec((1, G_VAL_DIM), lambda b: (0, 0))],
        out_specs=(sspec, rowspec),
        out_shape=(jax.ShapeDtypeStruct(s0.shape, F32), jax.ShapeDtypeStruct((bs, G_HEADS, 1, G_VAL_DIM), F32)),
        compiler_params=pltpu.CompilerParams(dimension_semantics=("arbitrary",)),
        name="gla_sample",
    )(qcol, kcol, gcol, vrow, sgrow, s0, gn)


def _rope_tables(pos):
    inv = jnp.power(jnp.float32(ROPE_THETA), -jnp.arange(ROT_HALF, dtype=F32) * (2.0 / ROT_DIM))
    ang = pos.astype(F32)[:, None] * inv[None, :]
    return jnp.cos(ang).T, jnp.sin(ang).T


def _pick_tile(n, pref):
    t = min(n, pref)
    while n % t:
        t //= 2
    return t


def kernel(x_prompt, x_sample, cache_k, cache_v, state_gla, page_table, norm_g, w_in, w_a2, b_a2, gla_norm_g, w_out, final_norm_g):
    bp, lp, d = x_prompt.shape
    bs, ls, _ = x_sample.shape
    depth = norm_g.shape[0]
    past_len = page_table.shape[1] * PAGE_SIZE
    assert ls == 1 and lp % MOBA_BLOCK == 0 and past_len % MOBA_BLOCK == 0
    assert cache_k.shape[2:] == (PAGE_SIZE, A_HEADS, A_HEAD_DIM)

    tabs_p = _rope_tables(jnp.arange(lp, dtype=jnp.int32))
    tabs_s = _rope_tables(jnp.full((bs,), past_len, dtype=jnp.int32))
    kc_t = jnp.transpose(cache_k, (0, 1, 3, 4, 2))
    vc_t = jnp.transpose(cache_v, (0, 1, 3, 4, 2))

    tm_p = _pick_tile(lp, 512)
    tt = _pick_tile(lp, 256)
    hp = x_prompt.reshape(bp * lp, d)
    hs = x_sample.reshape(bs, d)
    fg = final_norm_g.reshape(1, d)
    kp_l, vp_l, sp_l, ks_l, vs_l, ss_l = [], [], [], [], [], []
    rows = A_WIDTH
    for l in range(depth):
        wt = jnp.transpose(w_in[l])
        w_qkvt = wt[0:3 * rows].astype(BF16)
        a1_t = jnp.pad(wt[4 * rows + 2 * G_KEY_WIDTH + 2 * G_WIDTH:], ((0, LANES - G_GATE_RANK), (0, 0)))
        w_row = jnp.concatenate([wt[3 * rows:4 * rows + 2 * G_KEY_WIDTH + 2 * G_WIDTH], a1_t], axis=0)
        w_row = jnp.transpose(w_row).astype(BF16)
        wa2 = jnp.pad(w_a2[l], ((0, LANES - G_GATE_RANK), (0, 0))).astype(BF16)
        ba2 = b_a2[l].reshape(1, G_KEY_WIDTH)
        ng = norm_g[l].reshape(1, d)
        gn = gla_norm_g[l].reshape(1, G_VAL_DIM)
        wo = w_out[l].astype(BF16)
        last = l == depth - 1

        qt, kt, vt, sga, qg, kg, vg, sgg, lf = _proj_in(hp, ng, w_row, w_qkvt, wa2, ba2, tabs_p, bp, tm_p)
        oa = _moba_prompt(qt, kt, vt, sga.reshape(bp, lp, A_WIDTH))
        og, s_fin = _gla_prompt(qg.reshape(bp, lp, -1), kg.reshape(bp, lp, -1), vg.reshape(bp, lp, -1),
                                lf.reshape(bp, lp, -1), sgg.reshape(bp, lp, -1), gn, tt)
        hp = _proj_out(oa.reshape(bp * lp, A_WIDTH), og.reshape(bp * lp, G_WIDTH), hp, wo, fg, tm_p, last)
        kp_l.append(jnp.transpose(kt.reshape(bp, A_HEADS, A_HEAD_DIM, lp), (0, 3, 1, 2)))
        vp_l.append(jnp.transpose(vt.reshape(bp, A_HEADS, A_HEAD_DIM, lp), (0, 3, 1, 2)))
        sp_l.append(s_fin)

        qt, kt, vt, sga, qg, kg, vg, sgg, lf = _proj_in(hs, ng, w_row, w_qkvt, wa2, ba2, tabs_s, 1, bs)
        q_new = jnp.transpose(qt[0])
        k_new = jnp.transpose(kt[0])
        v_new = jnp.transpose(vt[0])
        as_col = lambda a: a.astype(F32).reshape(bs, A_HEADS, A_HEAD_DIM, 1)
        pn, idx, pown = _sample_score(page_table, as_col(q_new), as_col(k_new), kc_t, l)
        nsel = min(MOBA_TOPK, page_table.shape[1] // PAGES_PER_BLOCK + 1)
        idx_s = idx[:, :, :nsel].reshape(bs, A_HEADS * nsel)
        oa_s = _sample_pv(page_table, idx_s, pn, pown, as_col(v_new), as_col(sga), vc_t, l)
        g_col = lambda a: a.astype(F32).reshape(bs, G_HEADS, G_KEY_DIM, 1)
        g_row = lambda a: a.astype(F32).reshape(bs, G_HEADS, 1, G_VAL_DIM)
        s_new, og_s = _gla_sample(g_col(qg), g_col(kg), g_col(lf), g_row(vg), g_row(sgg), state_gla[l], gn)
        hs = _proj_out(oa_s.reshape(bs, A_WIDTH).astype(BF16), og_s.reshape(bs, G_WIDTH).astype(BF16),
                       hs, wo, fg, bs, last)
        ks_l.append(k_new.reshape(bs, 1, A_HEADS, A_HEAD_DIM))
        vs_l.append(v_new.reshape(bs, 1, A_HEADS, A_HEAD_DIM))
        ss_l.append(s_new)

    return (hp.reshape(bp, lp, d), hs.reshape(bs, ls, d), jnp.stack(kp_l), jnp.stack(vp_l), jnp.stack(sp_l),
            jnp.stack(ks_l), jnp.stack(vs_l), jnp.stack(ss_l))
```

```python
import functools

import jax
import jax.numpy as jnp
from jax import lax
from jax.experimental import pallas as pl
from jax.experimental.pallas import tpu as pltpu

A_HEADS = 8
A_HEAD_DIM = 64
A_WIDTH = A_HEADS * A_HEAD_DIM
ROT_DIM = A_HEAD_DIM // 4
ROT_HALF = ROT_DIM // 2
ROPE_THETA = 500000.0
MOBA_BLOCK = 256
MOBA_TOPK = 3
G_HEADS = 4
G_KEY_DIM = 64
G_VAL_DIM = 128
G_KEY_WIDTH = G_HEADS * G_KEY_DIM
G_WIDTH = G_HEADS * G_VAL_DIM
G_GATE_RANK = 16
G_GATE_NORM = 16.0
G_CHUNK = 32
NORM_EPS = 1e-6
PAGE_SIZE = 128
PAGES_PER_BLOCK = MOBA_BLOCK // PAGE_SIZE

LANES = 128
VMEM_LIMIT_BYTES = 56 * 1024 * 1024

F32 = jnp.float32
BF16 = jnp.bfloat16
NEG_INF = float("-inf")
MASK_BIAS = -1e30
MOBA_HEADS_PER_STEP = 4
SAMPLE_PAGES_IN_FLIGHT = 24
V_PAD_ROWS = 16

_NT = (((1,), (1,)), ((), ()))


def _dot(a, b):
    return jnp.dot(a, b, preferred_element_type=F32)


def _dot_nt(a, b):
    return lax.dot_general(a, b, _NT, preferred_element_type=F32)


def _split3(x):
    x1 = x.astype(BF16)
    r1 = x - x1.astype(F32)
    x2 = r1.astype(BF16)
    x3 = (r1 - x2.astype(F32)).astype(BF16)
    return x1, x2, x3


def _proj_in_kernel(x_ref, g_ref, w_ref, wqkv_ref, wa2_ref, ba2_ref, ct_ref, st_ref,
                    qt_ref, kt_ref, vt_ref, sga_ref, qg_ref, kg_ref, vg_ref, sgg_ref, lf_ref):
    x = x_ref[...]
    ms = jnp.mean(x * x, axis=-1, keepdims=True)
    h = (x * lax.rsqrt(ms + NORM_EPS) * g_ref[...]).astype(BF16)

    def proj(lo, hi):
        return _dot(h, w_ref[:, lo:hi])

    o = 0
    zg = proj(o, o + A_WIDTH)
    sga_ref[...] = (zg * jax.nn.sigmoid(zg)).astype(sga_ref.dtype)
    o += A_WIDTH
    qg_ref[...] = proj(o, o + G_KEY_WIDTH).astype(qg_ref.dtype)
    o += G_KEY_WIDTH
    kg_ref[...] = proj(o, o + G_KEY_WIDTH).astype(kg_ref.dtype)
    o += G_KEY_WIDTH
    vg_ref[...] = proj(o, o + G_WIDTH).astype(vg_ref.dtype)
    o += G_WIDTH
    zg = proj(o, o + G_WIDTH)
    sgg_ref[...] = (zg * jax.nn.sigmoid(zg)).astype(sgg_ref.dtype)
    o += G_WIDTH
    a1 = proj(o, o + LANES)
    la = _dot(a1.astype(BF16), wa2_ref[...]) + ba2_ref[...]
    lf_ref[...] = -(jnp.maximum(-la, 0.0) + jnp.log1p(jnp.exp(-jnp.abs(la)))) * (1.0 / G_GATE_NORM)

    zt = _dot_nt(wqkv_ref[...], h)
    ct, st = ct_ref[...], st_ref[...]
    q_scale = A_HEAD_DIM ** -0.5
    for hd in range(A_HEADS):
        b0 = hd * A_HEAD_DIM
        x1 = zt[b0:b0 + ROT_HALF]
        x2 = zt[b0 + ROT_HALF:b0 + ROT_DIM]
        qh = jnp.concatenate([x1 * ct - x2 * st, x2 * ct + x1 * st, zt[b0 + ROT_DIM:b0 + A_HEAD_DIM]], axis=0)
        qt_ref[0, b0:b0 + A_HEAD_DIM, :] = (qh * q_scale).astype(qt_ref.dtype)
        b0 += A_WIDTH
        x1 = zt[b0:b0 + ROT_HALF]
        x2 = zt[b0 + ROT_HALF:b0 + ROT_DIM]
        k0 = hd * A_HEAD_DIM
        kt_ref[0, k0:k0 + ROT_HALF, :] = x1 * ct - x2 * st
        kt_ref[0, k0 + ROT_HALF:k0 + ROT_DIM, :] = x2 * ct + x1 * st
        kt_ref[0, k0 + ROT_DIM:k0 + A_HEAD_DIM, :] = zt[b0 + ROT_DIM:b0 + A_HEAD_DIM]
    vt_ref[0] = zt[2 * A_WIDTH:3 * A_WIDTH]


def _proj_in(x, norm_g, w_row, w_qkvt, wa2, ba2, tabs, n_batch, tm):
    n, d = x.shape
    seq = n // n_batch
    nl = seq // tm
    ct_tab, st_tab = tabs
    wcols = w_row.shape[1]
    row = lambda i: (i, 0)
    const = lambda i: (0, 0)
    out_shapes = (
        jax.ShapeDtypeStruct((n_batch, A_WIDTH, seq), BF16),
        jax.ShapeDtypeStruct((n_batch, A_WIDTH, seq), F32),
        jax.ShapeDtypeStruct((n_batch, A_WIDTH, seq), F32),
        jax.ShapeDtypeStruct((n, A_WIDTH), BF16),
        jax.ShapeDtypeStruct((n, G_KEY_WIDTH), BF16),
        jax.ShapeDtypeStruct((n, G_KEY_WIDTH), BF16),
        jax.ShapeDtypeStruct((n, G_WIDTH), BF16),
        jax.ShapeDtypeStruct((n, G_WIDTH), BF16),
        jax.ShapeDtypeStruct((n, G_KEY_WIDTH), F32),
    )
    kv_spec = pl.BlockSpec((1, A_WIDTH, tm), lambda i: (i // nl, 0, i % nl))
    return pl.pallas_call(
        _proj_in_kernel,
        grid=(n // tm,),
        in_specs=[
            pl.BlockSpec((tm, d), row),
            pl.BlockSpec((1, d), const),
            pl.BlockSpec((d, wcols), const),
            pl.BlockSpec((3 * A_WIDTH, d), const),
            pl.BlockSpec((LANES, G_KEY_WIDTH), const),
            pl.BlockSpec((1, G_KEY_WIDTH), const),
            pl.BlockSpec((ROT_HALF, tm), lambda i: (0, i % nl)),
            pl.BlockSpec((ROT_HALF, tm), lambda i: (0, i % nl)),
        ],
        out_specs=(
            kv_spec, kv_spec, kv_spec,
            pl.BlockSpec((tm, A_WIDTH), row),
            pl.BlockSpec((tm, G_KEY_WIDTH), row), pl.BlockSpec((tm, G_KEY_WIDTH), row),
            pl.BlockSpec((tm, G_WIDTH), row), pl.BlockSpec((tm, G_WIDTH), row),
            pl.BlockSpec((tm, G_KEY_WIDTH), row),
        ),
        out_shape=out_shapes,
        compiler_params=pltpu.CompilerParams(dimension_semantics=("arbitrary",), vmem_limit_bytes=VMEM_LIMIT_BYTES),
        name="proj_in",
    )(x, norm_g, w_row, w_qkvt, wa2, ba2, ct_tab, st_tab)


def _proj_out_kernel(a_ref, g_ref, x_ref, w_ref, fg_ref, o_ref, *, final):
    half = a_ref.shape[1]
    y = _dot(a_ref[...], w_ref[0:half, :]) + _dot(g_ref[...], w_ref[half:, :])
    xo = x_ref[...] + y
    if final:
        ms = jnp.mean(xo * xo, axis=-1, keepdims=True)
        xo = xo * lax.rsqrt(ms + NORM_EPS) * fg_ref[...]
    o_ref[...] = xo


def _proj_out(a, g, x, w, fg, tm, final):
    n, d = x.shape
    row = lambda i: (i, 0)
    const = lambda i: (0, 0)
    return pl.pallas_call(
        functools.partial(_proj_out_kernel, final=final),
        grid=(n // tm,),
        in_specs=[
            pl.BlockSpec((tm, a.shape[1]), row),
            pl.BlockSpec((tm, g.shape[1]), row),
            pl.BlockSpec((tm, d), row),
            pl.BlockSpec(w.shape, const),
            pl.BlockSpec((1, d), const),
        ],
        out_specs=pl.BlockSpec((tm, d), row),
        out_shape=jax.ShapeDtypeStruct((n, d), F32),
        compiler_params=pltpu.CompilerParams(dimension_semantics=("arbitrary",), vmem_limit_bytes=VMEM_LIMIT_BYTES),
        name="proj_out",
    )(a, g, x, w, fg)


def _moba_prompt_kernel(qt_ref, kt_ref, vt_ref, sg_ref, o_ref, ka_ref, va_ref, km_ref, qa_ref, s_ref, acc_ref, m_ref,
                        *, nb, nsel, unroll):
    qi = pl.program_id(2)
    blk = MOBA_BLOCK
    dh = A_HEAD_DIM
    heads = qt_ref.shape[1] // dh
    per_group = LANES // dh

    @pl.when(qi == 0)
    def _prepare():
        lane = lax.broadcasted_iota(jnp.int32, (blk, LANES), 1)
        ones_rows = jnp.where(lax.broadcasted_iota(jnp.int32, (V_PAD_ROWS, blk), 0) == 0, 1.0, 0.0)
        for j in range(nb):
            kj = kt_ref[0, :, j * blk:(j + 1) * blk].T
            km_ref[j:j + 1, :] = jnp.mean(kj, axis=0, keepdims=True)
            onehot = jnp.where(lane == dh + j, 1.0, 0.0)
            for hh in range(heads):
                grp, sub = divmod(hh, per_group)
                kh = kj[:, grp * LANES:(grp + 1) * LANES]
                if sub:
                    kh = pltpu.roll(kh, LANES - sub * dh, 1)
                ka_ref[hh, j * blk:(j + 1) * blk, :] = jnp.where(lane < dh, kh, onehot).astype(BF16)
                vh = vt_ref[0, hh * dh:(hh + 1) * dh, j * blk:(j + 1) * blk]
                va_ref[j, hh] = jnp.concatenate([vh, ones_rows], axis=0).astype(BF16)
        s_ref[...] = jnp.zeros_like(s_ref)
        m_ref[...] = jnp.zeros_like(m_ref)

    qb = jnp.minimum(qi, nb - 1)
    prev = jnp.maximum(qi - 1, 0)
    m_old = [m_ref[hh] for hh in range(heads)]

    for hh in range(heads):
        acc_ref[hh] = _dot(va_ref[prev, hh], jnp.exp(s_ref[hh, nb] - m_old[hh]).astype(BF16))

    rowf = lax.broadcasted_iota(jnp.int32, (nb, blk), 0).astype(F32)
    past = rowf < qb.astype(F32)
    causal = lax.broadcasted_iota(jnp.int32, (blk, blk), 0) <= lax.broadcasted_iota(jnp.int32, (blk, blk), 1)
    pad = jnp.zeros((LANES - dh - nb, blk), BF16)
    r_own = pl.multiple_of(qb * blk, blk)

    m_run, qd = [], []
    for hh in range(heads):
        qh = qt_ref[0, hh * dh:(hh + 1) * dh, :]
        km = km_ref[:, hh * dh:(hh + 1) * dh]
        km1 = km.astype(BF16)
        km2 = (km - km1.astype(F32)).astype(BF16)
        g = jnp.where(past, _dot(km1, qh) + _dot(km2, qh), NEG_INF)
        sel = jnp.zeros((nb, blk), F32)
        for _ in range(nsel):
            m = jnp.max(g, axis=0, keepdims=True)
            idx = jnp.min(jnp.where(g == m, rowf, float(nb)), axis=0, keepdims=True)
            pick = jnp.logical_and(rowf == idx, m > NEG_INF)
            sel = jnp.where(pick, 1.0, sel)
            g = jnp.where(pick, NEG_INF, g)
        bias = jnp.where(sel > 0.0, 0.0, MASK_BIAS).astype(BF16)
        qa_ref[hh] = jnp.concatenate([qh, bias, pad], axis=0)
        qd.append(jnp.concatenate([qh, jnp.zeros((LANES - dh, blk), BF16)], axis=0))

    for hh in range(heads):
        s = jnp.where(causal, _dot(ka_ref[hh, pl.ds(r_own, blk), :], qd[hh]), MASK_BIAS)
        s_ref[hh, nb] = s
        m_run.append(jnp.max(s, axis=0, keepdims=True))

    groups = lax.shift_right_logical(qi + (unroll - 1), unroll.bit_length() - 1)

    def body(gi, ms):
        ms = list(ms)
        for hh in range(heads):
            part = None
            for u in range(unroll):
                j = gi * unroll + u
                p = jnp.where(j < prev, jnp.exp(s_ref[hh, j] - m_old[hh]), 0.0).astype(BF16)
                t = _dot(va_ref[j, hh], p)
                part = t if part is None else part + t
                r0 = pl.multiple_of(j * blk, blk)
                s = _dot(ka_ref[hh, pl.ds(r0, blk), :], qa_ref[hh])
                s_ref[hh, j] = s
                ms[hh] = jnp.maximum(ms[hh], jnp.max(s, axis=0, keepdims=True))
            acc_ref[hh] += part
        return tuple(ms)

    m_fin = lax.fori_loop(0, groups, body, tuple(m_run))
    for hh in range(heads):
        m_ref[hh] = m_fin[hh]

    ot = jnp.concatenate([acc_ref[hh, 0:dh, :] / acc_ref[hh, dh:dh + 1, :] for hh in range(heads)], axis=0)
    o_ref[0] = (ot.T * sg_ref[0].astype(F32)).astype(o_ref.dtype)


def _moba_prompt(qt, kt, vt, sg):
    b, width, seq = qt.shape
    nb = seq // MOBA_BLOCK
    nsel = min(MOBA_TOPK, nb)
    heads = MOBA_HEADS_PER_STEP
    hw = heads * A_HEAD_DIM
    assert A_HEAD_DIM + nb <= LANES and width % hw == 0
    unroll = 2 if nb % 2 == 0 else 1
    tok = pl.BlockSpec((1, MOBA_BLOCK, hw), lambda bi, hp, qi: (bi, jnp.maximum(qi - 1, 0), hp))
    feat = pl.BlockSpec((1, hw, seq), lambda bi, hp, qi: (bi, hp, 0))
    return pl.pallas_call(
        functools.partial(_moba_prompt_kernel, nb=nb, nsel=nsel, unroll=unroll),
        grid=(b, width // hw, nb + 1),
        in_specs=[pl.BlockSpec((1, hw, MOBA_BLOCK), lambda bi, hp, qi: (bi, hp, jnp.minimum(qi, nb - 1))),
                  feat, feat, tok],
        out_specs=tok,
        out_shape=jax.ShapeDtypeStruct((b, seq, width), BF16),
        scratch_shapes=[
            pltpu.VMEM((heads, seq, LANES), BF16),
            pltpu.VMEM((nb, heads, A_HEAD_DIM + V_PAD_ROWS, MOBA_BLOCK), BF16),
            pltpu.VMEM((nb, hw), F32),
            pltpu.VMEM((heads, LANES, MOBA_BLOCK), BF16),
            pltpu.VMEM((heads, nb + 1, MOBA_BLOCK, MOBA_BLOCK), F32),
            pltpu.VMEM((heads, A_HEAD_DIM + V_PAD_ROWS, MOBA_BLOCK), F32),
            pltpu.VMEM((heads, 1, MOBA_BLOCK), F32),
        ],
        compiler_params=pltpu.CompilerParams(
            dimension_semantics=("arbitrary", "arbitrary", "arbitrary"), vmem_limit_bytes=VMEM_LIMIT_BYTES),
        name="moba_prompt",
    )(qt, kt, vt, sg)


def _gla_prompt_kernel(q_ref, k_ref, v_ref, lf_ref, sg_ref, gn_ref, o_ref, s_ref, st_ref, *, nt):
    t = pl.program_id(1)
    tt = q_ref.shape[1]
    ck = G_CHUNK
    dk, dv = G_KEY_DIM, G_VAL_DIM
    nh = G_HEADS

    @pl.when(t == 0)
    def _init():
        st_ref[...] = jnp.zeros_like(st_ref)

    g1, g2, g3 = _split3(lf_ref[0])
    row = lax.broadcasted_iota(jnp.int32, (tt, tt), 0)
    col = lax.broadcasted_iota(jnp.int32, (tt, tt), 1)
    ck_shift = ck.bit_length() - 1
    dk_shift = dk.bit_length() - 1
    same = lax.shift_right_logical(row, ck_shift) == lax.shift_right_logical(col, ck_shift)
    lower = jnp.where(jnp.logical_and(same, col <= row), 1.0, 0.0).astype(BF16)
    upper = jnp.where(jnp.logical_and(same, col > row), 1.0, 0.0).astype(BF16)
    bcum = _dot(lower, g1) + _dot(lower, g2) + _dot(lower, g3)
    rest = _dot(upper, g1) + _dot(upper, g2) + _dot(upper, g3)

    qf = q_ref[0].astype(F32)
    kf = k_ref[0].astype(F32)
    qt = qf * jnp.exp(bcum) * (dk ** -0.5)
    kt = kf * jnp.exp(-bcum)
    kd = kf * jnp.exp(rest)

    head_of_lane = lax.shift_right_logical(lax.broadcasted_iota(jnp.int32, (ck, nh * dk), 1), dk_shift)
    own = [head_of_lane == hd for hd in range(nh)]
    ri = lax.broadcasted_iota(jnp.int32, (nh * ck, nh * ck), 0)
    ci = lax.broadcasted_iota(jnp.int32, (nh * ck, nh * ck), 1)
    causal = jnp.bitwise_and(ri, ck - 1) >= jnp.bitwise_and(ci, ck - 1)
    gn = gn_ref[...]
    st = st_ref[...]

    def stack(x):
        return jnp.concatenate([jnp.where(own[hd], x, 0.0) for hd in range(nh)], axis=0).astype(BF16)

    for c in range(tt // ck):
        r0, r1 = c * ck, (c + 1) * ck
        q4, k4, kd4 = stack(qt[r0:r1]), stack(kt[r0:r1]), stack(kd[r0:r1])
        v4 = jnp.concatenate([v_ref[0, r0:r1, hd * dv:(hd + 1) * dv] for hd in range(nh)], axis=0)
        a = jnp.where(causal, _dot_nt(q4, k4), 0.0)
        o = _dot(a.astype(BF16), v4) + _dot_nt(q4, st.astype(BF16))
        ms = jnp.mean(o * o, axis=-1, keepdims=True)
        on = o * lax.rsqrt(ms + NORM_EPS) * gn
        for hd in range(nh):
            sg = sg_ref[0, r0:r1, hd * dv:(hd + 1) * dv].astype(F32)
            o_ref[0, r0:r1, hd * dv:(hd + 1) * dv] = (on[hd * ck:(hd + 1) * ck] * sg).astype(o_ref.dtype)
        kv = lax.dot_general(v4, kd4, (((0,), (0,)), ((), ())), preferred_element_type=F32)
        st = st * jnp.exp(bcum[r1 - 1:r1, :]) + kv

    st_ref[...] = st

    @pl.when(t == nt - 1)
    def _emit():
        per_group = LANES // dk
        for grp in range(nh // per_group):
            s_grp = st[:, grp * LANES:(grp + 1) * LANES].T
            for hh in range(per_group):
                s_ref[0, grp * per_group + hh] = s_grp[hh * dk:(hh + 1) * dk, :]


def _gla_prompt(qg, kg, vg, lf, sg, gn, tt):
    b, seq, _ = qg.shape
    nt = seq // tt
    kspec = pl.BlockSpec((1, tt, G_KEY_WIDTH), lambda bi, ti: (bi, ti, 0))
    vspec = pl.BlockSpec((1, tt, G_WIDTH), lambda bi, ti: (bi, ti, 0))
    return pl.pallas_call(
        functools.partial(_gla_prompt_kernel, nt=nt),
        grid=(b, nt),
        in_specs=[kspec, kspec, vspec, kspec, vspec, pl.BlockSpec((1, G_VAL_DIM), lambda bi, ti: (0, 0))],
        out_specs=(vspec, pl.BlockSpec((1, G_HEADS, G_KEY_DIM, G_VAL_DIM), lambda bi, ti: (bi, 0, 0, 0))),
        out_shape=(jax.ShapeDtypeStruct((b, seq, G_WIDTH), BF16),
                   jax.ShapeDtypeStruct((b, G_HEADS, G_KEY_DIM, G_VAL_DIM), F32)),
        scratch_shapes=[pltpu.VMEM((G_VAL_DIM, G_KEY_WIDTH), F32)],
        compiler_params=pltpu.CompilerParams(
            dimension_semantics=("arbitrary", "arbitrary"), vmem_limit_bytes=VMEM_LIMIT_BYTES),
        name="gla_prompt",
    )(qg, kg, vg, lf, sg, gn)


def _sample_score_kernel(pt_ref, qcol_ref, kcol_ref, kc_ref, pn_ref, idx_ref, pown_ref,
                         buf_ref, sem_ref, qb_ref, sc_ref, *, layer, npages, nbuf, bs, nsel):
    b = pl.program_id(0)
    total = bs * npages
    nblk = npages // PAGES_PER_BLOCK
    dh = A_HEAD_DIM

    def page_copy(g, slot):
        bb = g // npages
        return pltpu.make_async_copy(kc_ref.at[layer, pt_ref[bb, g - bb * npages]], buf_ref.at[slot], sem_ref.at[slot])

    @pl.when(b == 0)
    def _prime():
        for s in range(nbuf):
            page_copy(s, s).start()

    for h in range(A_HEADS):
        qb_ref[h] = jnp.broadcast_to(qcol_ref[0, h], (dh, PAGE_SIZE))

    def page_body(p, _):
        g = b * npages + p
        slot = lax.rem(g, nbuf)
        page_copy(g, slot).wait()
        for h in range(A_HEADS):
            sc_ref[p, h:h + 1, :] = jnp.sum(buf_ref[slot, h] * qb_ref[h], axis=0, keepdims=True)

        @pl.when(g + nbuf < total)
        def _next():
            page_copy(g + nbuf, slot).start()

        return 0

    lax.fori_loop(0, npages, page_body, 0)

    lane = lax.broadcasted_iota(jnp.int32, (A_HEADS, LANES), 1).astype(F32)
    sub = lax.broadcasted_iota(jnp.int32, (A_HEADS, LANES), 0)

    gate = jnp.full((A_HEADS, LANES), NEG_INF, F32)
    for j in range(nblk):
        sblk = sc_ref[PAGES_PER_BLOCK * j]
        for pp in range(1, PAGES_PER_BLOCK):
            sblk = sblk + sc_ref[PAGES_PER_BLOCK * j + pp]
        gj = jnp.sum(sblk, axis=1, keepdims=True) * (1.0 / MOBA_BLOCK)
        gate = jnp.where(lane == float(j), gj, gate)

    g = gate
    sel = jnp.zeros((A_HEADS, LANES), F32)
    idx_out = jnp.full((A_HEADS, LANES), -1.0, F32)
    for r in range(nsel):
        m = jnp.max(g, axis=1, keepdims=True)
        idx = jnp.min(jnp.where(g == m, lane, float(LANES)), axis=1, keepdims=True)
        pick = jnp.logical_and(lane == idx, m > NEG_INF)
        sel = jnp.where(pick, 1.0, sel)
        g = jnp.where(pick, NEG_INF, g)
        idx_out = jnp.where(lane == float(r), jnp.where(m > NEG_INF, idx, -1.0), idx_out)
    idx_ref[0] = idx_out.astype(jnp.int32)

    s_own = jnp.zeros((A_HEADS, LANES), F32)
    for h in range(A_HEADS):
        so = jnp.sum(qcol_ref[0, h] * kcol_ref[0, h], axis=0, keepdims=True)
        s_own = jnp.where(sub == h, jnp.broadcast_to(so, (A_HEADS, LANES)), s_own)

    masks = [jnp.broadcast_to(sel[:, j:j + 1], (A_HEADS, LANES)) > 0.0 for j in range(nblk)]
    mx = s_own
    for pg in range(npages):
        mx = jnp.maximum(mx, jnp.where(masks[pg // PAGES_PER_BLOCK], sc_ref[pg], NEG_INF))
    m = jnp.max(mx, axis=1, keepdims=True)
    lsum = jnp.zeros((A_HEADS, LANES), F32)
    for pg in range(npages):
        p = jnp.where(masks[pg // PAGES_PER_BLOCK], jnp.exp(sc_ref[pg] - m), 0.0)
        pn_ref[0, pg] = p
        lsum = lsum + p
    p_own = jnp.exp(s_own - m)
    inv = 1.0 / (jnp.sum(lsum, axis=1, keepdims=True) + p_own)
    for pg in range(npages):
        pn_ref[0, pg] = pn_ref[0, pg] * inv
    pown_ref[0] = p_own * inv


def _sample_score(page_table, qcol, kcol, cache_t, layer, nbuf=SAMPLE_PAGES_IN_FLIGHT):
    bs, npages = page_table.shape
    nblk = npages // PAGES_PER_BLOCK
    nsel = min(MOBA_TOPK, nblk + 1)
    col = pl.BlockSpec((1, A_HEADS, A_HEAD_DIM, 1), lambda b, pt: (b, 0, 0, 0))
    grid_spec = pltpu.PrefetchScalarGridSpec(
        num_scalar_prefetch=1,
        grid=(bs,),
        in_specs=[col, col, pl.BlockSpec(memory_space=pl.ANY)],
        out_specs=(
            pl.BlockSpec((1, npages, A_HEADS, PAGE_SIZE), lambda b, pt: (b, 0, 0, 0)),
            pl.BlockSpec((1, A_HEADS, LANES), lambda b, pt: (b, 0, 0)),
            pl.BlockSpec((1, A_HEADS, LANES), lambda b, pt: (b, 0, 0)),
        ),
        scratch_shapes=[
            pltpu.VMEM((nbuf, A_HEADS, A_HEAD_DIM, PAGE_SIZE), F32),
            pltpu.SemaphoreType.DMA((nbuf,)),
            pltpu.VMEM((A_HEADS, A_HEAD_DIM, PAGE_SIZE), F32),
            pltpu.VMEM((npages, A_HEADS, PAGE_SIZE), F32),
        ],
    )
    return pl.pallas_call(
        functools.partial(_sample_score_kernel, layer=layer, npages=npages, nbuf=min(nbuf, npages), bs=bs, nsel=nsel),
        grid_spec=grid_spec,
        out_shape=(
            jax.ShapeDtypeStruct((bs, npages, A_HEADS, PAGE_SIZE), F32),
            jax.ShapeDtypeStruct((bs, A_HEADS, LANES), jnp.int32),
            jax.ShapeDtypeStruct((bs, A_HEADS, LANES), F32),
        ),
        compiler_params=pltpu.CompilerParams(dimension_semantics=("arbitrary",), vmem_limit_bytes=VMEM_LIMIT_BYTES),
        name="sample_score",
    )(page_table, qcol, kcol, cache_t)


def _sample_pv_kernel(pt_ref, idx_ref, pn_ref, pown_ref, vcol_ref, sgcol_ref, vc_ref, o_ref,
                      vbuf_ref, sem_ref, *, layer, nsel, bs):
    b = pl.program_id(0)
    slot = lax.rem(b, 2)

    def issue(bb, sl):
        for h in range(A_HEADS):
            for r in range(nsel):
                j = jnp.maximum(idx_ref[bb, h * nsel + r], 0)
                for pp in range(PAGES_PER_BLOCK):
                    pg = pt_ref[bb, PAGES_PER_BLOCK * j + pp]
                    pltpu.make_async_copy(vc_ref.at[layer, pg, h], vbuf_ref.at[sl, h, r, pp], sem_ref.at[sl]).start()

    @pl.when(b == 0)
    def _first():
        issue(0, 0)

    @pl.when(b + 1 < bs)
    def _prefetch():
        issue(b + 1, 1 - slot)

    for _ in range(A_HEADS * nsel * PAGES_PER_BLOCK):
        pltpu.make_async_copy(vc_ref.at[layer, 0, 0], vbuf_ref.at[slot, 0, 0, 0], sem_ref.at[slot]).wait()

    for h in range(A_HEADS):
        acc = jnp.zeros((A_HEAD_DIM, PAGE_SIZE), F32)
        for r in range(nsel):
            jraw = idx_ref[b, h * nsel + r]
            j = jnp.maximum(jraw, 0)
            w = jnp.where(jraw >= 0, 1.0, 0.0)
            for pp in range(PAGES_PER_BLOCK):
                prow = pn_ref[0, PAGES_PER_BLOCK * j + pp, h:h + 1, :] * w
                acc = acc + vbuf_ref[slot, h, r, pp] * prow
        ocol = jnp.sum(acc, axis=1, keepdims=True) + pown_ref[0, h:h + 1, 0:1] * vcol_ref[0, h]
        o_ref[0, h] = ocol * sgcol_ref[0, h]


def _sample_pv(page_table, idx, pn, pown, vcol, sgcol, cache_t, layer):
    bs, npages = page_table.shape
    nsel = idx.shape[1] // A_HEADS
    col = pl.BlockSpec((1, A_HEADS, A_HEAD_DIM, 1), lambda b, pt, ix: (b, 0, 0, 0))
    grid_spec = pltpu.PrefetchScalarGridSpec(
        num_scalar_prefetch=2,
        grid=(bs,),
        in_specs=[
            pl.BlockSpec((1, npages, A_HEADS, PAGE_SIZE), lambda b, pt, ix: (b, 0, 0, 0)),
            pl.BlockSpec((1, A_HEADS, LANES), lambda b, pt, ix: (b, 0, 0)),
            col, col,
            pl.BlockSpec(memory_space=pl.ANY),
        ],
        out_specs=col,
        scratch_shapes=[
            pltpu.VMEM((2, A_HEADS, nsel, PAGES_PER_BLOCK, A_HEAD_DIM, PAGE_SIZE), F32),
            pltpu.SemaphoreType.DMA((2,)),
        ],
    )
    return pl.pallas_call(
        functools.partial(_sample_pv_kernel, layer=layer, nsel=nsel, bs=bs),
        grid_spec=grid_spec,
        out_shape=jax.ShapeDtypeStruct((bs, A_HEADS, A_HEAD_DIM, 1), F32),
        compiler_params=pltpu.CompilerParams(dimension_semantics=("arbitrary",), vmem_limit_bytes=VMEM_LIMIT_BYTES),
        name="sample_pv",
    )(page_table, idx, pn, pown, vcol, sgcol, cache_t)


def _gla_sample_kernel(q_ref, k_ref, g_ref, v_ref, sg_ref, s0_ref, gn_ref, s_ref, o_ref):
    for h in range(G_HEADS):
        s_new = jnp.exp(g_ref[0, h]) * s0_ref[0, h] + k_ref[0, h] * v_ref[0, h]
        s_ref[0, h] = s_new
        o = jnp.sum(q_ref[0, h] * s_new, axis=0, keepdims=True) * (G_KEY_DIM ** -0.5)
        ms = jnp.mean(o * o, axis=-1, keepdims=True)
        o_ref[0, h] = o * lax.rsqrt(ms + NORM_EPS) * gn_ref[...] * sg_ref[0, h]


def _gla_sample(qcol, kcol, gcol, vrow, sgrow, s0, gn):
    bs = s0.shape[0]
    col = pl.BlockSpec((1, G_HEADS, G_KEY_DIM, 1), lambda b: (b, 0, 0, 0))
    rowspec = pl.BlockSpec((1, G_HEADS, 1, G_VAL_DIM), lambda b: (b, 0, 0, 0))
    sspec = pl.BlockSpec((1, G_HEADS, G_KEY_DIM, G_VAL_DIM), lambda b: (b, 0, 0, 0))
    return pl.pallas_call(
        _gla_sample_kernel,
        grid=(bs,),
        in_specs=[col, col, col, rowspec, rowspec, sspec, pl.BlockSpec((1, G_VAL_DIM), lambda b: (0, 0))],
        out_specs=(sspec, rowspec),
        out_shape=(jax.ShapeDtypeStruct(s0.shape, F32), jax.ShapeDtypeStruct((bs, G_HEADS, 1, G_VAL_DIM), F32)),
        compiler_params=pltpu.CompilerParams(dimension_semantics=("arbitrary",)),
        name="gla_sample",
    )(qcol, kcol, gcol, vrow, sgrow, s0, gn)


def _rope_tables(pos):
    inv = jnp.power(jnp.float32(ROPE_THETA), -jnp.arange(ROT_HALF, dtype=F32) * (2.0 / ROT_DIM))
    ang = pos.astype(F32)[:, None] * inv[None, :]
    return jnp.cos(ang).T, jnp.sin(ang).T


def _pick_tile(n, pref):
    t = min(n, pref)
    while n % t:
        t //= 2
    return t


def kernel(x_prompt, x_sample, cache_k, cache_v, state_gla, page_table, norm_g, w_in, w_a2, b_a2, gla_norm_g, w_out, final_norm_g):
    bp, lp, d = x_prompt.shape
    bs, ls, _ = x_sample.shape
    depth = norm_g.shape[0]
    past_len = page_table.shape[1] * PAGE_SIZE
    assert ls == 1 and lp % MOBA_BLOCK == 0 and past_len % MOBA_BLOCK == 0
    assert cache_k.shape[2:] == (PAGE_SIZE, A_HEADS, A_HEAD_DIM)

    tabs_p = _rope_tables(jnp.arange(lp, dtype=jnp.int32))
    tabs_s = _rope_tables(jnp.full((bs,), past_len, dtype=jnp.int32))
    kc_t = jnp.transpose(cache_k, (0, 1, 3, 4, 2))
    vc_t = jnp.transpose(cache_v, (0, 1, 3, 4, 2))

    tm_p = _pick_tile(lp, 512)
    tt = _pick_tile(lp, 256)
    hp = x_prompt.reshape(bp * lp, d)
    hs = x_sample.reshape(bs, d)
    fg = final_norm_g.reshape(1, d)
    kp_l, vp_l, sp_l, ks_l, vs_l, ss_l = [], [], [], [], [], []
    rows = A_WIDTH
    for l in range(depth):
        wt = jnp.transpose(w_in[l])
        w_qkvt = wt[0:3 * rows].astype(BF16)
        a1_t = jnp.pad(wt[4 * rows + 2 * G_KEY_WIDTH + 2 * G_WIDTH:], ((0, LANES - G_GATE_RANK), (0, 0)))
        w_row = jnp.concatenate([wt[3 * rows:4 * rows + 2 * G_KEY_WIDTH + 2 * G_WIDTH], a1_t], axis=0)
        w_row = jnp.transpose(w_row).astype(BF16)
        wa2 = jnp.pad(w_a2[l], ((0, LANES - G_GATE_RANK), (0, 0))).astype(BF16)
        ba2 = b_a2[l].reshape(1, G_KEY_WIDTH)
        ng = norm_g[l].reshape(1, d)
        gn = gla_norm_g[l].reshape(1, G_VAL_DIM)
        wo = w_out[l].astype(BF16)
        last = l == depth - 1

        qt, kt, vt, sga, qg, kg, vg, sgg, lf = _proj_in(hp, ng, w_row, w_qkvt, wa2, ba2, tabs_p, bp, tm_p)
        oa = _moba_prompt(qt, kt, vt, sga.reshape(bp, lp, A_WIDTH))
        og, s_fin = _gla_prompt(qg.reshape(bp, lp, -1), kg.reshape(bp, lp, -1), vg.reshape(bp, lp, -1),
                                lf.reshape(bp, lp, -1), sgg.reshape(bp, lp, -1), gn, tt)
        hp = _proj_out(oa.reshape(bp * lp, A_WIDTH), og.reshape(bp * lp, G_WIDTH), hp, wo, fg, tm_p, last)
        kp_l.append(jnp.transpose(kt.reshape(bp, A_HEADS, A_HEAD_DIM, lp), (0, 3, 1, 2)))
        vp_l.append(jnp.transpose(vt.reshape(bp, A_HEADS, A_HEAD_DIM, lp), (0, 3, 1, 2)))
        sp_l.append(s_fin)

        qt, kt, vt, sga, qg, kg, vg, sgg, lf = _proj_in(hs, ng, w_row, w_qkvt, wa2, ba2, tabs_s, 1, bs)
        q_new = jnp.transpose(qt[0])
        k_new = jnp.transpose(kt[0])
        v_new = jnp.transpose(vt[0])
        as_col = lambda a: a.astype(F32).reshape(bs, A_HEADS, A_HEAD_DIM, 1)
        pn, idx, pown = _sample_score(page_table, as_col(q_new), as_col(k_new), kc_t, l)
        nsel = min(MOBA_TOPK, page_table.shape[1] // PAGES_PER_BLOCK + 1)
        idx_s = idx[:, :, :nsel].reshape(bs, A_HEADS * nsel)
        oa_s = _sample_pv(page_table, idx_s, pn, pown, as_col(v_new), as_col(sga), vc_t, l)
        g_col = lambda a: a.astype(F32).reshape(bs, G_HEADS, G_KEY_DIM, 1)
        g_row = lambda a: a.astype(F32).reshape(bs, G_HEADS, 1, G_VAL_DIM)
        s_new, og_s = _gla_sample(g_col(qg), g_col(kg), g_col(lf), g_row(vg), g_row(sgg), state_gla[l], gn)
        hs = _proj_out(oa_s.reshape(bs, A_WIDTH).astype(BF16), og_s.reshape(bs, G_WIDTH).astype(BF16),
                       hs, wo, fg, bs, last)
        ks_l.append(k_new.reshape(bs, 1, A_HEADS, A_HEAD_DIM))
        vs_l.append(v_new.reshape(bs, 1, A_HEADS, A_HEAD_DIM))
        ss_l.append(s_new)

    return (hp.reshape(bp, lp, d), hs.reshape(bs, ls, d), jnp.stack(kp_l), jnp.stack(vp_l), jnp.stack(sp_l),
            jnp.stack(ks_l), jnp.stack(vs_l), jnp.stack(ss_l))
```

```python
import functools

import jax
import jax.numpy as jnp
from jax import lax
from jax.experimental import pallas as pl
from jax.experimental.pallas import tpu as pltpu

A_HEADS = 8
A_HEAD_DIM = 64
A_WIDTH = A_HEADS * A_HEAD_DIM
ROT_DIM = A_HEAD_DIM // 4
ROT_HALF = ROT_DIM // 2
ROPE_THETA = 500000.0
MOBA_BLOCK = 256
MOBA_TOPK = 3
G_HEADS = 4
G_KEY_DIM = 64
G_VAL_DIM = 128
G_KEY_WIDTH = G_HEADS * G_KEY_DIM
G_WIDTH = G_HEADS * G_VAL_DIM
G_GATE_RANK = 16
G_GATE_NORM = 16.0
G_CHUNK = 32
NORM_EPS = 1e-6
PAGE_SIZE = 128
PAGES_PER_BLOCK = MOBA_BLOCK // PAGE_SIZE

LANES = 128
VMEM_LIMIT_BYTES = 56 * 1024 * 1024

F32 = jnp.float32
BF16 = jnp.bfloat16
NEG_INF = float("-inf")
MASK_BIAS = -1e30
MOBA_HEADS_PER_STEP = 4
SAMPLE_PAGES_IN_FLIGHT = 24
V_PAD_ROWS = 16

_NT = (((1,), (1,)), ((), ()))
_TN = (((0,), (0,)), ((), ()))


def _dot(a, b):
    return jnp.dot(a, b, preferred_element_type=F32)


def _dot_nt(a, b):
    return lax.dot_general(a, b, _NT, preferred_element_type=F32)


def _dot_tn(a, b):
    return lax.dot_general(a, b, _TN, preferred_element_type=F32)


def _split3(x):
    x1 = x.astype(BF16)
    r1 = x - x1.astype(F32)
    x2 = r1.astype(BF16)
    x3 = (r1 - x2.astype(F32)).astype(BF16)
    return x1, x2, x3


def _lane_column(x, lane):
    pick = lax.broadcasted_iota(jnp.int32, (1,) * (x.ndim - 1) + (x.shape[-1],), x.ndim - 1) == lane
    return jnp.sum(jnp.where(pick, x.astype(F32), 0.0), axis=-1, keepdims=True)


def _proj_in_kernel(*refs, n_alias):
    x_ref, g_ref, w_ref, wt_ref, wa2_ref, ba2_ref, ct_ref, st_ref = refs[:8]
    qt_ref, kt_ref, vt_ref, sgt_ref, qg_ref, kg_ref, vg_ref, sgg_ref, lf_ref = refs[8 + n_alias:]
    x = x_ref[...]
    ms = jnp.mean(x * x, axis=-1, keepdims=True)
    h = (x * lax.rsqrt(ms + NORM_EPS) * g_ref[...]).astype(BF16)

    def proj(lo, hi):
        return _dot(h, w_ref[:, lo:hi])

    o = 0
    qg_ref[...] = proj(o, o + G_KEY_WIDTH).astype(qg_ref.dtype)
    o += G_KEY_WIDTH
    kg_ref[...] = proj(o, o + G_KEY_WIDTH).astype(kg_ref.dtype)
    o += G_KEY_WIDTH
    vg_ref[...] = proj(o, o + G_WIDTH).astype(vg_ref.dtype)
    o += G_WIDTH
    zg = proj(o, o + G_WIDTH)
    sgg_ref[...] = (zg * jax.nn.sigmoid(zg)).astype(sgg_ref.dtype)
    o += G_WIDTH
    a1 = proj(o, o + LANES)
    la = _dot(a1.astype(BF16), wa2_ref[...]) + ba2_ref[...]
    lf_ref[...] = -(jnp.maximum(-la, 0.0) + jnp.log1p(jnp.exp(-jnp.abs(la)))) * (1.0 / G_GATE_NORM)

    zt = _dot_nt(wt_ref[...], h)
    ct, st = ct_ref[...], st_ref[...]
    q_scale = A_HEAD_DIM ** -0.5
    for hd in range(A_HEADS):
        b0 = hd * A_HEAD_DIM
        x1 = zt[b0:b0 + ROT_HALF]
        x2 = zt[b0 + ROT_HALF:b0 + ROT_DIM]
        qh = jnp.concatenate([x1 * ct - x2 * st, x2 * ct + x1 * st, zt[b0 + ROT_DIM:b0 + A_HEAD_DIM]], axis=0)
        qt_ref[b0:b0 + A_HEAD_DIM, :] = (qh * q_scale).astype(qt_ref.dtype)
        k0 = b0 + A_WIDTH
        x1 = zt[k0:k0 + ROT_HALF]
        x2 = zt[k0 + ROT_HALF:k0 + ROT_DIM]
        kt_ref[b0:b0 + ROT_HALF, :] = x1 * ct - x2 * st
        kt_ref[b0 + ROT_HALF:b0 + ROT_DIM, :] = x2 * ct + x1 * st
        kt_ref[b0 + ROT_DIM:b0 + A_HEAD_DIM, :] = zt[k0 + ROT_DIM:k0 + A_HEAD_DIM]
    vt_ref[...] = zt[2 * A_WIDTH:3 * A_WIDTH]
    zg = zt[3 * A_WIDTH:4 * A_WIDTH]
    sgt_ref[...] = (zg * jax.nn.sigmoid(zg)).astype(sgt_ref.dtype)


def _proj_in(x, norm_g, w_row, w_t, wa2, ba2, tabs, n_batch, tm, layer, depth, kv_prev=None):
    n, d = x.shape
    seq = n // n_batch
    nl = seq // tm
    ct_tab, st_tab = tabs
    row = lambda i: (i, 0)
    const = lambda i: (0, 0)
    feat_shape = jax.ShapeDtypeStruct((n_batch, A_WIDTH, seq), BF16)
    kv_shape = jax.ShapeDtypeStruct((depth, n_batch, A_WIDTH, seq), F32)
    out_shapes = (
        feat_shape,
        kv_shape, kv_shape,
        feat_shape,
        jax.ShapeDtypeStruct((n, G_KEY_WIDTH), BF16),
        jax.ShapeDtypeStruct((n, G_KEY_WIDTH), BF16),
        jax.ShapeDtypeStruct((n, G_WIDTH), BF16),
        jax.ShapeDtypeStruct((n, G_WIDTH), BF16),
        jax.ShapeDtypeStruct((n, G_KEY_WIDTH), F32),
    )
    feat_spec = pl.BlockSpec((None, A_WIDTH, tm), lambda i: (i // nl, 0, i % nl))
    kv_spec = pl.BlockSpec((None, None, A_WIDTH, tm), lambda i: (layer, i // nl, 0, i % nl))
    in_specs = [
        pl.BlockSpec((tm, d), row),
        pl.BlockSpec((1, d), const),
        pl.BlockSpec(w_row.shape, const),
        pl.BlockSpec(w_t.shape, const),
        pl.BlockSpec((LANES, G_KEY_WIDTH), const),
        pl.BlockSpec((1, G_KEY_WIDTH), const),
        pl.BlockSpec((ROT_HALF, tm), lambda i: (0, i % nl)),
        pl.BlockSpec((ROT_HALF, tm), lambda i: (0, i % nl)),
    ]
    args = [x, norm_g, w_row, w_t, wa2, ba2, ct_tab, st_tab]
    aliases = {}
    if kv_prev is not None:
        in_specs += [pl.BlockSpec(memory_space=pl.ANY)] * 2
        aliases = {len(args): 1, len(args) + 1: 2}
        args += list(kv_prev)
    return pl.pallas_call(
        functools.partial(_proj_in_kernel, n_alias=len(aliases)),
        grid=(n // tm,),
        in_specs=in_specs,
        out_specs=(
            feat_spec, kv_spec, kv_spec, feat_spec,
            pl.BlockSpec((tm, G_KEY_WIDTH), row), pl.BlockSpec((tm, G_KEY_WIDTH), row),
            pl.BlockSpec((tm, G_WIDTH), row), pl.BlockSpec((tm, G_WIDTH), row),
            pl.BlockSpec((tm, G_KEY_WIDTH), row),
        ),
        out_shape=out_shapes,
        input_output_aliases=aliases,
        compiler_params=pltpu.CompilerParams(dimension_semantics=("arbitrary",), vmem_limit_bytes=VMEM_LIMIT_BYTES),
        name="proj_in",
    )(*args)


def _proj_out_kernel(a_ref, g_ref, x_ref, w_ref, fg_ref, o_ref, *, final, a_feature_major):
    half = g_ref.shape[1]
    a = a_ref[...].astype(BF16)
    ya = _dot_tn(a, w_ref[0:half, :]) if a_feature_major else _dot(a, w_ref[0:half, :])
    xo = x_ref[...] + (ya + _dot(g_ref[...].astype(BF16), w_ref[half:, :]))
    if final:
        ms = jnp.mean(xo * xo, axis=-1, keepdims=True)
        xo = xo * lax.rsqrt(ms + NORM_EPS) * fg_ref[...]
    o_ref[...] = xo


def _proj_out(a, g, x, w, fg, tm, final, a_feature_major=False):
    n, d = x.shape
    row = lambda i: (i, 0)
    const = lambda i: (0, 0)
    a_spec = pl.BlockSpec((a.shape[0], tm), lambda i: (0, i)) if a_feature_major else pl.BlockSpec((tm, a.shape[1]), row)
    return pl.pallas_call(
        functools.partial(_proj_out_kernel, final=final, a_feature_major=a_feature_major),
        grid=(n // tm,),
        in_specs=[
            a_spec,
            pl.BlockSpec((tm, g.shape[1]), row),
            pl.BlockSpec((tm, d), row),
            pl.BlockSpec(w.shape, const),
            pl.BlockSpec((1, d), const),
        ],
        out_specs=pl.BlockSpec((tm, d), row),
        out_shape=jax.ShapeDtypeStruct((n, d), F32),
        compiler_params=pltpu.CompilerParams(dimension_semantics=("arbitrary",), vmem_limit_bytes=VMEM_LIMIT_BYTES),
        name="proj_out",
    )(a, g, x, w, fg)


def _moba_prompt_kernel(qt_ref, kt_ref, vt_ref, sgt_ref, o_ref, ka_ref, va_ref, km_ref, qa_ref, s_ref, acc_ref, m_ref,
                        *, nb, nsel, unroll):
    qi = pl.program_id(2)
    blk = MOBA_BLOCK
    dh = A_HEAD_DIM
    heads = qt_ref.shape[0] // dh
    per_group = LANES // dh

    @pl.when(qi == 0)
    def _prepare():
        lane = lax.broadcasted_iota(jnp.int32, (blk, LANES), 1)
        ones_rows = jnp.where(lax.broadcasted_iota(jnp.int32, (V_PAD_ROWS, blk), 0) == 0, 1.0, 0.0)
        for j in range(nb):
            kj = kt_ref[:, j * blk:(j + 1) * blk].T
            km_ref[j:j + 1, :] = jnp.mean(kj, axis=0, keepdims=True)
            onehot = jnp.where(lane == dh + j, 1.0, 0.0)
            for hh in range(heads):
                grp, sub = divmod(hh, per_group)
                kh = kj[:, grp * LANES:(grp + 1) * LANES]
                if sub:
                    kh = pltpu.roll(kh, LANES - sub * dh, 1)
                ka_ref[hh, j * blk:(j + 1) * blk, :] = jnp.where(lane < dh, kh, onehot).astype(BF16)
                vh = vt_ref[hh * dh:(hh + 1) * dh, j * blk:(j + 1) * blk]
                va_ref[j, hh] = jnp.concatenate([vh, ones_rows], axis=0).astype(BF16)
        s_ref[...] = jnp.zeros_like(s_ref)
        m_ref[...] = jnp.zeros_like(m_ref)

    qb = jnp.minimum(qi, nb - 1)
    prev = jnp.maximum(qi - 1, 0)
    m_old = [m_ref[hh] for hh in range(heads)]

    for hh in range(heads):
        acc_ref[hh] = _dot(va_ref[prev, hh], jnp.exp(s_ref[hh, nb] - m_old[hh]).astype(BF16))

    rowf = lax.broadcasted_iota(jnp.int32, (nb, blk), 0).astype(F32)
    past = rowf < qb.astype(F32)
    causal = lax.broadcasted_iota(jnp.int32, (blk, blk), 0) <= lax.broadcasted_iota(jnp.int32, (blk, blk), 1)
    pad = jnp.zeros((LANES - dh - nb, blk), BF16)
    r_own = pl.multiple_of(qb * blk, blk)

    m_run, qd = [], []
    for hh in range(heads):
        qh = qt_ref[hh * dh:(hh + 1) * dh, :]
        km = km_ref[:, hh * dh:(hh + 1) * dh]
        km1 = km.astype(BF16)
        km2 = (km - km1.astype(F32)).astype(BF16)
        g = jnp.where(past, _dot(km1, qh) + _dot(km2, qh), NEG_INF)
        sel = jnp.zeros((nb, blk), F32)
        for _ in range(nsel):
            m = jnp.max(g, axis=0, keepdims=True)
            idx = jnp.min(jnp.where(g == m, rowf, float(nb)), axis=0, keepdims=True)
            pick = jnp.logical_and(rowf == idx, m > NEG_INF)
            sel = jnp.where(pick, 1.0, sel)
            g = jnp.where(pick, NEG_INF, g)
        bias = jnp.where(sel > 0.0, 0.0, MASK_BIAS).astype(BF16)
        qa_ref[hh] = jnp.concatenate([qh, bias, pad], axis=0)
        qd.append(jnp.concatenate([qh, jnp.zeros((LANES - dh, blk), BF16)], axis=0))

    for hh in range(heads):
        s = jnp.where(causal, _dot(ka_ref[hh, pl.ds(r_own, blk), :], qd[hh]), MASK_BIAS)
        s_ref[hh, nb] = s
        m_run.append(jnp.max(s, axis=0, keepdims=True))

    groups = lax.shift_right_logical(qi + (unroll - 1), unroll.bit_length() - 1)

    def body(gi, ms):
        ms = list(ms)
        for hh in range(heads):
            part = None
            for u in range(unroll):
                j = gi * unroll + u
                p = jnp.where(j < prev, jnp.exp(s_ref[hh, j] - m_old[hh]), 0.0).astype(BF16)
                t = _dot(va_ref[j, hh], p)
                part = t if part is None else part + t
                r0 = pl.multiple_of(j * blk, blk)
                s = _dot(ka_ref[hh, pl.ds(r0, blk), :], qa_ref[hh])
                s_ref[hh, j] = s
                ms[hh] = jnp.maximum(ms[hh], jnp.max(s, axis=0, keepdims=True))
            acc_ref[hh] += part
        return tuple(ms)

    m_fin = lax.fori_loop(0, groups, body, tuple(m_run))
    for hh in range(heads):
        m_ref[hh] = m_fin[hh]

    ot = jnp.concatenate([acc_ref[hh, 0:dh, :] / acc_ref[hh, dh:dh + 1, :] for hh in range(heads)], axis=0)
    o_ref[...] = (ot * sgt_ref[...].astype(F32)).T.astype(o_ref.dtype)


def _moba_prompt(qt, kt_all, vt_all, sgt, layer):
    b, width, seq = qt.shape
    nb = seq // MOBA_BLOCK
    nsel = min(MOBA_TOPK, nb)
    heads = MOBA_HEADS_PER_STEP
    hw = heads * A_HEAD_DIM
    assert A_HEAD_DIM + nb <= LANES and width % hw == 0
    unroll = 2 if nb % 2 == 0 else 1
    q_spec = pl.BlockSpec((None, hw, MOBA_BLOCK), lambda bi, hp, qi: (bi, hp, jnp.minimum(qi, nb - 1)))
    g_spec = pl.BlockSpec((None, hw, MOBA_BLOCK), lambda bi, hp, qi: (bi, hp, jnp.maximum(qi - 1, 0)))
    kv_spec = pl.BlockSpec((None, None, hw, seq), lambda bi, hp, qi: (layer, bi, hp, 0))
    return pl.pallas_call(
        functools.partial(_moba_prompt_kernel, nb=nb, nsel=nsel, unroll=unroll),
        grid=(b, width // hw, nb + 1),
        in_specs=[q_spec, kv_spec, kv_spec, g_spec],
        out_specs=pl.BlockSpec((None, MOBA_BLOCK, hw), lambda bi, hp, qi: (bi, jnp.maximum(qi - 1, 0), hp)),
        out_shape=jax.ShapeDtypeStruct((b, seq, width), BF16),
        scratch_shapes=[
            pltpu.VMEM((heads, seq, LANES), BF16),
            pltpu.VMEM((nb, heads, A_HEAD_DIM + V_PAD_ROWS, MOBA_BLOCK), BF16),
            pltpu.VMEM((nb, hw), F32),
            pltpu.VMEM((heads, LANES, MOBA_BLOCK), BF16),
            pltpu.VMEM((heads, nb + 1, MOBA_BLOCK, MOBA_BLOCK), F32),
            pltpu.VMEM((heads, A_HEAD_DIM + V_PAD_ROWS, MOBA_BLOCK), F32),
            pltpu.VMEM((heads, 1, MOBA_BLOCK), F32),
        ],
        compiler_params=pltpu.CompilerParams(
            dimension_semantics=("arbitrary", "arbitrary", "arbitrary"), vmem_limit_bytes=VMEM_LIMIT_BYTES),
        name="moba_prompt",
    )(qt, kt_all, vt_all, sgt)


def _gla_prompt_kernel(q_ref, k_ref, v_ref, lf_ref, sg_ref, gn_ref, o_ref, s_ref, st_ref, *, nt):
    t = pl.program_id(1)
    tt = q_ref.shape[1]
    ck = G_CHUNK
    dk, dv = G_KEY_DIM, G_VAL_DIM
    nh = G_HEADS

    @pl.when(t == 0)
    def _init():
        st_ref[...] = jnp.zeros_like(st_ref)

    g1, g2, g3 = _split3(lf_ref[0])
    row = lax.broadcasted_iota(jnp.int32, (tt, tt), 0)
    col = lax.broadcasted_iota(jnp.int32, (tt, tt), 1)
    ck_shift = ck.bit_length() - 1
    dk_shift = dk.bit_length() - 1
    same = lax.shift_right_logical(row, ck_shift) == lax.shift_right_logical(col, ck_shift)
    lower = jnp.where(jnp.logical_and(same, col <= row), 1.0, 0.0).astype(BF16)
    upper = jnp.where(jnp.logical_and(same, col > row), 1.0, 0.0).astype(BF16)
    bcum = _dot(lower, g1) + _dot(lower, g2) + _dot(lower, g3)
    rest = _dot(upper, g1) + _dot(upper, g2) + _dot(upper, g3)

    qf = q_ref[0].astype(F32)
    kf = k_ref[0].astype(F32)
    qt = qf * jnp.exp(bcum) * (dk ** -0.5)
    kt = kf * jnp.exp(-bcum)
    kd = kf * jnp.exp(rest)

    head_of_lane = lax.shift_right_logical(lax.broadcasted_iota(jnp.int32, (ck, nh * dk), 1), dk_shift)
    own = [head_of_lane == hd for hd in range(nh)]
    ri = lax.broadcasted_iota(jnp.int32, (nh * ck, nh * ck), 0)
    ci = lax.broadcasted_iota(jnp.int32, (nh * ck, nh * ck), 1)
    causal = jnp.bitwise_and(ri, ck - 1) >= jnp.bitwise_and(ci, ck - 1)
    gn = gn_ref[...]
    st = st_ref[...]

    def stack(x):
        return jnp.concatenate([jnp.where(own[hd], x, 0.0) for hd in range(nh)], axis=0).astype(BF16)

    nc = tt // ck
    q4s, o_intra, kvs = [], [], []
    for c in range(nc):
        r0, r1 = c * ck, (c + 1) * ck
        q4, k4, kd4 = stack(qt[r0:r1]), stack(kt[r0:r1]), stack(kd[r0:r1])
        v4 = jnp.concatenate([v_ref[0, r0:r1, hd * dv:(hd + 1) * dv] for hd in range(nh)], axis=0)
        a = jnp.where(causal, _dot_nt(q4, k4), 0.0)
        q4s.append(q4)
        o_intra.append(_dot(a.astype(BF16), v4))
        kvs.append(_dot_tn(v4, kd4))

    for c in range(nc):
        r0, r1 = c * ck, (c + 1) * ck
        o = o_intra[c] + _dot_nt(q4s[c], st.astype(BF16))
        ms = jnp.mean(o * o, axis=-1, keepdims=True)
        on = o * lax.rsqrt(ms + NORM_EPS) * gn
        for hd in range(nh):
            sg = sg_ref[0, r0:r1, hd * dv:(hd + 1) * dv].astype(F32)
            o_ref[0, r0:r1, hd * dv:(hd + 1) * dv] = (on[hd * ck:(hd + 1) * ck] * sg).astype(o_ref.dtype)
        st = st * jnp.exp(bcum[r1 - 1:r1, :]) + kvs[c]

    st_ref[...] = st

    @pl.when(t == nt - 1)
    def _emit():
        per_group = LANES // dk
        for grp in range(nh // per_group):
            s_grp = st[:, grp * LANES:(grp + 1) * LANES].T
            for hh in range(per_group):
                s_ref[0, grp * per_group + hh] = s_grp[hh * dk:(hh + 1) * dk, :]


def _gla_prompt(qg, kg, vg, lf, sg, gn, tt):
    b, seq, _ = qg.shape
    nt = seq // tt
    kspec = pl.BlockSpec((1, tt, G_KEY_WIDTH), lambda bi, ti: (bi, ti, 0))
    vspec = pl.BlockSpec((1, tt, G_WIDTH), lambda bi, ti: (bi, ti, 0))
    return pl.pallas_call(
        functools.partial(_gla_prompt_kernel, nt=nt),
        grid=(b, nt),
        in_specs=[kspec, kspec, vspec, kspec, vspec, pl.BlockSpec((1, G_VAL_DIM), lambda bi, ti: (0, 0))],
        out_specs=(vspec, pl.BlockSpec((1, G_HEADS, G_KEY_DIM, G_VAL_DIM), lambda bi, ti: (bi, 0, 0, 0))),
        out_shape=(jax.ShapeDtypeStruct((b, seq, G_WIDTH), BF16),
                   jax.ShapeDtypeStruct((b, G_HEADS, G_KEY_DIM, G_VAL_DIM), F32)),
        scratch_shapes=[pltpu.VMEM((G_VAL_DIM, G_KEY_WIDTH), F32)],
        compiler_params=pltpu.CompilerParams(
            dimension_semantics=("arbitrary", "arbitrary"), vmem_limit_bytes=VMEM_LIMIT_BYTES),
        name="gla_prompt",
    )(qg, kg, vg, lf, sg, gn)


def _sample_score_kernel(pt_ref, qt_ref, kt_ref, kc_ref, pn_ref, idx_ref, pown_ref,
                         buf_ref, sem_ref, qb_ref, sc_ref, *, layer, npages, nbuf, bs, nsel):
    b = pl.program_id(0)
    total = bs * npages
    nblk = npages // PAGES_PER_BLOCK
    dh = A_HEAD_DIM

    def page_copy(g, slot):
        bb = g // npages
        return pltpu.make_async_copy(kc_ref.at[layer, pt_ref[bb, g - bb * npages]], buf_ref.at[slot], sem_ref.at[slot])

    @pl.when(b == 0)
    def _prime():
        for s in range(nbuf):
            page_copy(s, s).start()

    qcol = _lane_column(qt_ref[...], b)
    kcol = _lane_column(kt_ref[...], b)
    for h in range(A_HEADS):
        qb_ref[h] = jnp.broadcast_to(qcol[h], (dh, PAGE_SIZE))

    def page_body(p, _):
        g = b * npages + p
        slot = lax.rem(g, nbuf)
        page_copy(g, slot).wait()
        for h in range(A_HEADS):
            sc_ref[p, h:h + 1, :] = jnp.sum(buf_ref[slot, h] * qb_ref[h], axis=0, keepdims=True)

        @pl.when(g + nbuf < total)
        def _next():
            page_copy(g + nbuf, slot).start()

        return 0

    lax.fori_loop(0, npages, page_body, 0)

    lane = lax.broadcasted_iota(jnp.int32, (A_HEADS, LANES), 1).astype(F32)
    sub = lax.broadcasted_iota(jnp.int32, (A_HEADS, LANES), 0)

    gate = jnp.full((A_HEADS, LANES), NEG_INF, F32)
    for j in range(nblk):
        sblk = sc_ref[PAGES_PER_BLOCK * j]
        for pp in range(1, PAGES_PER_BLOCK):
            sblk = sblk + sc_ref[PAGES_PER_BLOCK * j + pp]
        gj = jnp.sum(sblk, axis=1, keepdims=True) * (1.0 / MOBA_BLOCK)
        gate = jnp.where(lane == float(j), gj, gate)

    g = gate
    sel = jnp.zeros((A_HEADS, LANES), F32)
    idx_out = jnp.full((A_HEADS, LANES), -1.0, F32)
    for r in range(nsel):
        m = jnp.max(g, axis=1, keepdims=True)
        idx = jnp.min(jnp.where(g == m, lane, float(LANES)), axis=1, keepdims=True)
        pick = jnp.logical_and(lane == idx, m > NEG_INF)
        sel = jnp.where(pick, 1.0, sel)
        g = jnp.where(pick, NEG_INF, g)
        idx_out = jnp.where(lane == float(r), jnp.where(m > NEG_INF, idx, -1.0), idx_out)
    idx_ref[0] = idx_out.astype(jnp.int32)

    s_own = jnp.zeros((A_HEADS, LANES), F32)
    for h in range(A_HEADS):
        so = jnp.sum(qcol[h] * kcol[h], axis=0, keepdims=True)
        s_own = jnp.where(sub == h, jnp.broadcast_to(so, (A_HEADS, LANES)), s_own)

    masks = [jnp.broadcast_to(sel[:, j:j + 1], (A_HEADS, LANES)) > 0.0 for j in range(nblk)]
    mx = s_own
    for pg in range(npages):
        mx = jnp.maximum(mx, jnp.where(masks[pg // PAGES_PER_BLOCK], sc_ref[pg], NEG_INF))
    m = jnp.max(mx, axis=1, keepdims=True)
    lsum = jnp.zeros((A_HEADS, LANES), F32)
    for pg in range(npages):
        p = jnp.where(masks[pg // PAGES_PER_BLOCK], jnp.exp(sc_ref[pg] - m), 0.0)
        pn_ref[0, pg] = p
        lsum = lsum + p
    p_own = jnp.exp(s_own - m)
    inv = 1.0 / (jnp.sum(lsum, axis=1, keepdims=True) + p_own)
    for pg in range(npages):
        pn_ref[0, pg] = pn_ref[0, pg] * inv
    pown_ref[0] = p_own * inv


def _sample_score(page_table, qt, kt, cache_t, layer, nbuf=SAMPLE_PAGES_IN_FLIGHT):
    bs, npages = page_table.shape
    nblk = npages // PAGES_PER_BLOCK
    nsel = min(MOBA_TOPK, nblk + 1)
    feat = pl.BlockSpec((A_HEADS, A_HEAD_DIM, bs), lambda b, pt: (0, 0, 0))
    grid_spec = pltpu.PrefetchScalarGridSpec(
        num_scalar_prefetch=1,
        grid=(bs,),
        in_specs=[feat, feat, pl.BlockSpec(memory_space=pl.ANY)],
        out_specs=(
            pl.BlockSpec((1, npages, A_HEADS, PAGE_SIZE), lambda b, pt: (b, 0, 0, 0)),
            pl.BlockSpec((1, A_HEADS, LANES), lambda b, pt: (b, 0, 0)),
            pl.BlockSpec((1, A_HEADS, LANES), lambda b, pt: (b, 0, 0)),
        ),
        scratch_shapes=[
            pltpu.VMEM((min(nbuf, npages), A_HEADS, A_HEAD_DIM, PAGE_SIZE), F32),
            pltpu.SemaphoreType.DMA((min(nbuf, npages),)),
            pltpu.VMEM((A_HEADS, A_HEAD_DIM, PAGE_SIZE), F32),
            pltpu.VMEM((npages, A_HEADS, PAGE_SIZE), F32),
        ],
    )
    return pl.pallas_call(
        functools.partial(_sample_score_kernel, layer=layer, npages=npages, nbuf=min(nbuf, npages), bs=bs, nsel=nsel),
        grid_spec=grid_spec,
        out_shape=(
            jax.ShapeDtypeStruct((bs, npages, A_HEADS, PAGE_SIZE), F32),
            jax.ShapeDtypeStruct((bs, A_HEADS, LANES), jnp.int32),
            jax.ShapeDtypeStruct((bs, A_HEADS, LANES), F32),
        ),
        compiler_params=pltpu.CompilerParams(dimension_semantics=("arbitrary",), vmem_limit_bytes=VMEM_LIMIT_BYTES),
        name="sample_score",
    )(page_table, qt, kt, cache_t)


def _sample_pv_kernel(pt_ref, idx_ref, pn_ref, pown_ref, vt_ref, sgt_ref, vc_ref, o_ref,
                      vbuf_ref, sem_ref, *, layer, nsel, bs):
    b = pl.program_id(0)
    slot = lax.rem(b, 2)

    def issue(bb, sl):
        for h in range(A_HEADS):
            for r in range(nsel):
                j = jnp.maximum(idx_ref[bb, h * nsel + r], 0)
                for pp in range(PAGES_PER_BLOCK):
                    pg = pt_ref[bb, PAGES_PER_BLOCK * j + pp]
                    pltpu.make_async_copy(vc_ref.at[layer, pg, h], vbuf_ref.at[sl, h, r, pp], sem_ref.at[sl]).start()

    @pl.when(b == 0)
    def _first():
        issue(0, 0)
        o_ref[...] = jnp.zeros_like(o_ref)

    @pl.when(b + 1 < bs)
    def _prefetch():
        issue(b + 1, 1 - slot)

    for _ in range(A_HEADS * nsel * PAGES_PER_BLOCK):
        pltpu.make_async_copy(vc_ref.at[layer, 0, 0], vbuf_ref.at[slot, 0, 0, 0], sem_ref.at[slot]).wait()

    vcol = _lane_column(vt_ref[...], b)
    sgcol = _lane_column(sgt_ref[...], b)
    mine = lax.broadcasted_iota(jnp.int32, (1, bs), 1) == b
    for h in range(A_HEADS):
        acc = jnp.zeros((A_HEAD_DIM, PAGE_SIZE), F32)
        for r in range(nsel):
            jraw = idx_ref[b, h * nsel + r]
            j = jnp.maximum(jraw, 0)
            w = jnp.where(jraw >= 0, 1.0, 0.0)
            for pp in range(PAGES_PER_BLOCK):
                prow = pn_ref[0, PAGES_PER_BLOCK * j + pp, h:h + 1, :] * w
                acc = acc + vbuf_ref[slot, h, r, pp] * prow
        ocol = jnp.sum(acc, axis=1, keepdims=True) + pown_ref[0, h:h + 1, 0:1] * vcol[h]
        o_ref[h] = jnp.where(mine, jnp.broadcast_to(ocol * sgcol[h], (A_HEAD_DIM, bs)), o_ref[h])


def _sample_pv(page_table, idx, pn, pown, vt, sgt, cache_t, layer):
    bs, npages = page_table.shape
    nsel = idx.shape[1] // A_HEADS
    feat = pl.BlockSpec((A_HEADS, A_HEAD_DIM, bs), lambda b, pt, ix: (0, 0, 0))
    grid_spec = pltpu.PrefetchScalarGridSpec(
        num_scalar_prefetch=2,
        grid=(bs,),
        in_specs=[
            pl.BlockSpec((1, npages, A_HEADS, PAGE_SIZE), lambda b, pt, ix: (b, 0, 0, 0)),
            pl.BlockSpec((1, A_HEADS, LANES), lambda b, pt, ix: (b, 0, 0)),
            feat, feat,
            pl.BlockSpec(memory_space=pl.ANY),
        ],
        out_specs=feat,
        scratch_shapes=[
            pltpu.VMEM((2, A_HEADS, nsel, PAGES_PER_BLOCK, A_HEAD_DIM, PAGE_SIZE), F32),
            pltpu.SemaphoreType.DMA((2,)),
        ],
    )
    return pl.pallas_call(
        functools.partial(_sample_pv_kernel, layer=layer, nsel=nsel, bs=bs),
        grid_spec=grid_spec,
        out_shape=jax.ShapeDtypeStruct((A_HEADS, A_HEAD_DIM, bs), F32),
        compiler_params=pltpu.CompilerParams(dimension_semantics=("arbitrary",), vmem_limit_bytes=VMEM_LIMIT_BYTES),
        name="sample_pv",
    )(page_table, idx, pn, pown, vt, sgt, cache_t)


def _gla_sample_kernel(qt_ref, kt_ref, gt_ref, v_ref, sg_ref, s0_ref, gn_ref, s_ref, o_ref):
    b = pl.program_id(0)
    dv = G_VAL_DIM
    bs = v_ref.shape[0]

    @pl.when(b == 0)
    def _init():
        o_ref[...] = jnp.zeros_like(o_ref)

    qcol = _lane_column(qt_ref[...], b)
    kcol = _lane_column(kt_ref[...], b)
    gcol = _lane_column(gt_ref[...], b)
    mine = lax.broadcasted_iota(jnp.int32, (bs, dv), 0) == b
    for h in range(G_HEADS):
        cols = slice(h * dv, (h + 1) * dv)
        v = jnp.sum(jnp.where(mine, v_ref[:, cols], 0.0), axis=0, keepdims=True)
        sg = jnp.sum(jnp.where(mine, sg_ref[:, cols], 0.0), axis=0, keepdims=True)
        s_new = jnp.exp(gcol[h]) * s0_ref[0, h] + kcol[h] * v
        s_ref[0, h] = s_new
        o = jnp.sum(qcol[h] * s_new, axis=0, keepdims=True) * (G_KEY_DIM ** -0.5)
        ms = jnp.mean(o * o, axis=-1, keepdims=True)
        on = o * lax.rsqrt(ms + NORM_EPS) * gn_ref[...] * sg
        o_ref[:, cols] = jnp.where(mine, jnp.broadcast_to(on, (bs, dv)), o_ref[:, cols])


def _gla_sample(qt, kt, gt, v, sg, s0, gn):
    bs = s0.shape[0]
    feat = pl.BlockSpec((G_HEADS, G_KEY_DIM, bs), lambda b: (0, 0, 0))
    tok = pl.BlockSpec((bs, G_WIDTH), lambda b: (0, 0))
    sspec = pl.BlockSpec((1, G_HEADS, G_KEY_DIM, G_VAL_DIM), lambda b: (b, 0, 0, 0))
    return pl.pallas_call(
        _gla_sample_kernel,
        grid=(bs,),
        in_specs=[feat, feat, feat, tok, tok, sspec, pl.BlockSpec((1, G_VAL_DIM), lambda b: (0, 0))],
        out_specs=(sspec, tok),
        out_shape=(jax.ShapeDtypeStruct(s0.shape, F32), jax.ShapeDtypeStruct((bs, G_WIDTH), F32)),
        compiler_params=pltpu.CompilerParams(dimension_semantics=("arbitrary",)),
        name="gla_sample",
    )(qt, kt, gt, v, sg, s0, gn)


def _rope_tables(pos):
    inv = jnp.power(jnp.float32(ROPE_THETA), -jnp.arange(ROT_HALF, dtype=F32) * (2.0 / ROT_DIM))
    ang = pos.astype(F32)[:, None] * inv[None, :]
    return jnp.cos(ang).T, jnp.sin(ang).T


def _pick_tile(n, pref):
    t = min(n, pref)
    while n % t:
        t //= 2
    return t


def kernel(x_prompt, x_sample, cache_k, cache_v, state_gla, page_table, norm_g, w_in, w_a2, b_a2, gla_norm_g, w_out, final_norm_g):
    bp, lp, d = x_prompt.shape
    bs, ls, _ = x_sample.shape
    depth = norm_g.shape[0]
    past_len = page_table.shape[1] * PAGE_SIZE
    assert ls == 1 and lp % MOBA_BLOCK == 0 and past_len % MOBA_BLOCK == 0
    assert cache_k.shape[2:] == (PAGE_SIZE, A_HEADS, A_HEAD_DIM)

    tabs_p = _rope_tables(jnp.arange(lp, dtype=jnp.int32))
    tabs_s = _rope_tables(jnp.full((bs,), past_len, dtype=jnp.int32))
    kc_t = jnp.transpose(cache_k, (0, 1, 3, 4, 2))
    vc_t = jnp.transpose(cache_v, (0, 1, 3, 4, 2))

    tm_p = _pick_tile(lp, 512)
    tt = _pick_tile(lp, 256)
    hp = x_prompt.reshape(bp * lp, d)
    hs = x_sample.reshape(bs, d)
    fg = final_norm_g.reshape(1, d)
    sp_l, ks_l, vs_l, ss_l = [], [], [], []
    kv_prompt = (jnp.zeros((depth, bp, A_WIDTH, lp), F32), jnp.zeros((depth, bp, A_WIDTH, lp), F32))
    rows = A_WIDTH
    g_lo = 4 * rows
    g_hi = g_lo + 2 * G_KEY_WIDTH + 2 * G_WIDTH
    for l in range(depth):
        wt = jnp.transpose(w_in[l])
        w_t = wt[0:4 * rows].astype(BF16)
        a1_t = jnp.pad(wt[g_hi:], ((0, LANES - G_GATE_RANK), (0, 0)))
        w_row = jnp.transpose(jnp.concatenate([wt[g_lo:g_hi], a1_t], axis=0)).astype(BF16)
        wa2 = jnp.pad(w_a2[l], ((0, LANES - G_GATE_RANK), (0, 0))).astype(BF16)
        ba2 = b_a2[l].reshape(1, G_KEY_WIDTH)
        ng = norm_g[l].reshape(1, d)
        gn = gla_norm_g[l].reshape(1, G_VAL_DIM)
        wo = w_out[l].astype(BF16)
        last = l == depth - 1

        qt, kt_all, vt_all, sgt, qg, kg, vg, sgg, lf = _proj_in(
            hp, ng, w_row, w_t, wa2, ba2, tabs_p, bp, tm_p, l, depth, kv_prompt)
        kv_prompt = (kt_all, vt_all)
        oa = _moba_prompt(qt, kt_all, vt_all, sgt, l)
        og, s_fin = _gla_prompt(qg.reshape(bp, lp, -1), kg.reshape(bp, lp, -1), vg.reshape(bp, lp, -1),
                                lf.reshape(bp, lp, -1), sgg.reshape(bp, lp, -1), gn, tt)
        hp = _proj_out(oa.reshape(bp * lp, A_WIDTH), og.reshape(bp * lp, G_WIDTH), hp, wo, fg, tm_p, last)
        sp_l.append(s_fin)

        qt, kt_s, vt_s, sgt, qg, kg, vg, sgg, lf = _proj_in(hs, ng, w_row, w_t, wa2, ba2, tabs_s, 1, bs, 0, 1)
        a_feat = lambda a: a.reshape(A_HEADS, A_HEAD_DIM, bs)
        pn, idx, pown = _sample_score(page_table, a_feat(qt), a_feat(kt_s), kc_t, l)
        nsel = min(MOBA_TOPK, page_table.shape[1] // PAGES_PER_BLOCK + 1)
        idx_s = idx[:, :, :nsel].reshape(bs, A_HEADS * nsel)
        oa_s = _sample_pv(page_table, idx_s, pn, pown, a_feat(vt_s), a_feat(sgt), vc_t, l)
        g_feat = lambda a: jnp.transpose(a.astype(F32)).reshape(G_HEADS, G_KEY_DIM, bs)
        s_new, og_s = _gla_sample(g_feat(qg), g_feat(kg), g_feat(lf), vg.astype(F32), sgg.astype(F32),
                                  state_gla[l], gn)
        hs = _proj_out(oa_s.reshape(A_WIDTH, bs), og_s, hs, wo, fg, bs, last, a_feature_major=True)
        ks_l.append(jnp.transpose(kt_s[0, 0]).reshape(bs, 1, A_HEADS, A_HEAD_DIM))
        vs_l.append(jnp.transpose(vt_s[0, 0]).reshape(bs, 1, A_HEADS, A_HEAD_DIM))
        ss_l.append(s_new)

    kt_all, vt_all = kv_prompt
    to_cache_layout = lambda a: jnp.transpose(a.reshape(depth, bp, A_HEADS, A_HEAD_DIM, lp), (0, 1, 4, 2, 3))
    return (hp.reshape(bp, lp, d), hs.reshape(bs, ls, d), to_cache_layout(kt_all), to_cache_layout(vt_all),
            jnp.stack(sp_l), jnp.stack(ks_l), jnp.stack(vs_l), jnp.stack(ss_l))
```

```python
import functools

import jax
import jax.numpy as jnp
from jax import lax
from jax.experimental import pallas as pl
from jax.experimental.pallas import tpu as pltpu

A_HEADS = 8
A_HEAD_DIM = 64
A_WIDTH = A_HEADS * A_HEAD_DIM
ROT_DIM = A_HEAD_DIM // 4
ROT_HALF = ROT_DIM // 2
ROPE_THETA = 500000.0
MOBA_BLOCK = 256
MOBA_TOPK = 3
G_HEADS = 4
G_KEY_DIM = 64
G_VAL_DIM = 128
G_KEY_WIDTH = G_HEADS * G_KEY_DIM
G_WIDTH = G_HEADS * G_VAL_DIM
G_GATE_RANK = 16
G_GATE_NORM = 16.0
G_CHUNK = 32
NORM_EPS = 1e-6
PAGE_SIZE = 128
PAGES_PER_BLOCK = MOBA_BLOCK // PAGE_SIZE

LANES = 128
VMEM_LIMIT_BYTES = 56 * 1024 * 1024

F32 = jnp.float32
BF16 = jnp.bfloat16
NEG_INF = float("-inf")
MASK_BIAS = -1e30
MOBA_HEADS_PER_STEP = 4
MOBA_BLOCKS_PER_ITERATION = 4
SAMPLE_PAGES_IN_FLIGHT = 24
V_PAD_ROWS = 16

_NT = (((1,), (1,)), ((), ()))
_TN = (((0,), (0,)), ((), ()))


def _dot(a, b):
    return jnp.dot(a, b, preferred_element_type=F32)


def _dot_nt(a, b):
    return lax.dot_general(a, b, _NT, preferred_element_type=F32)


def _dot_tn(a, b):
    return lax.dot_general(a, b, _TN, preferred_element_type=F32)


def _split3(x):
    x1 = x.astype(BF16)
    r1 = x - x1.astype(F32)
    x2 = r1.astype(BF16)
    x3 = (r1 - x2.astype(F32)).astype(BF16)
    return x1, x2, x3


def _lane_column(x, lane):
    pick = lax.broadcasted_iota(jnp.int32, (1,) * (x.ndim - 1) + (x.shape[-1],), x.ndim - 1) == lane
    return jnp.sum(jnp.where(pick, x.astype(F32), 0.0), axis=-1, keepdims=True)


def _proj_in_kernel(*refs, n_alias):
    x_ref, g_ref, w_ref, wt_ref, wa2_ref, ba2_ref, ct_ref, st_ref = refs[:8]
    qt_ref, kt_ref, vt_ref, sgt_ref, qg_ref, kg_ref, vg_ref, sgg_ref, lf_ref = refs[8 + n_alias:]
    x = x_ref[...]
    ms = jnp.mean(x * x, axis=-1, keepdims=True)
    h = (x * lax.rsqrt(ms + NORM_EPS) * g_ref[...]).astype(BF16)

    def proj(lo, hi):
        return _dot(h, w_ref[:, lo:hi])

    o = 0
    qg_ref[...] = proj(o, o + G_KEY_WIDTH).astype(qg_ref.dtype)
    o += G_KEY_WIDTH
    kg_ref[...] = proj(o, o + G_KEY_WIDTH).astype(kg_ref.dtype)
    o += G_KEY_WIDTH
    vg_ref[...] = proj(o, o + G_WIDTH).astype(vg_ref.dtype)
    o += G_WIDTH
    zg = proj(o, o + G_WIDTH)
    sgg_ref[...] = (zg * jax.nn.sigmoid(zg)).astype(sgg_ref.dtype)
    o += G_WIDTH
    a1 = proj(o, o + LANES)
    la = _dot(a1.astype(BF16), wa2_ref[...]) + ba2_ref[...]
    lf_ref[...] = -(jnp.maximum(-la, 0.0) + jnp.log1p(jnp.exp(-jnp.abs(la)))) * (1.0 / G_GATE_NORM)

    zt = _dot_nt(wt_ref[...], h)
    ct, st = ct_ref[...], st_ref[...]
    q_scale = A_HEAD_DIM ** -0.5
    for hd in range(A_HEADS):
        b0 = hd * A_HEAD_DIM
        x1 = zt[b0:b0 + ROT_HALF]
        x2 = zt[b0 + ROT_HALF:b0 + ROT_DIM]
        qh = jnp.concatenate([x1 * ct - x2 * st, x2 * ct + x1 * st, zt[b0 + ROT_DIM:b0 + A_HEAD_DIM]], axis=0)
        qt_ref[b0:b0 + A_HEAD_DIM, :] = (qh * q_scale).astype(qt_ref.dtype)
        k0 = b0 + A_WIDTH
        x1 = zt[k0:k0 + ROT_HALF]
        x2 = zt[k0 + ROT_HALF:k0 + ROT_DIM]
        kt_ref[b0:b0 + ROT_HALF, :] = x1 * ct - x2 * st
        kt_ref[b0 + ROT_HALF:b0 + ROT_DIM, :] = x2 * ct + x1 * st
        kt_ref[b0 + ROT_DIM:b0 + A_HEAD_DIM, :] = zt[k0 + ROT_DIM:k0 + A_HEAD_DIM]
    vt_ref[...] = zt[2 * A_WIDTH:3 * A_WIDTH]
    zg = zt[3 * A_WIDTH:4 * A_WIDTH]
    sgt_ref[...] = (zg * jax.nn.sigmoid(zg)).astype(sgt_ref.dtype)


def _proj_in(x, norm_g, w_row, w_t, wa2, ba2, tabs, n_batch, tm, layer, depth, kv_prev=None):
    n, d = x.shape
    seq = n // n_batch
    nl = seq // tm
    ct_tab, st_tab = tabs
    row = lambda i: (i, 0)
    const = lambda i: (0, 0)
    feat_shape = jax.ShapeDtypeStruct((n_batch, A_WIDTH, seq), BF16)
    kv_shape = jax.ShapeDtypeStruct((depth, n_batch, A_WIDTH, seq), F32)
    out_shapes = (
        feat_shape,
        kv_shape, kv_shape,
        feat_shape,
        jax.ShapeDtypeStruct((n, G_KEY_WIDTH), BF16),
        jax.ShapeDtypeStruct((n, G_KEY_WIDTH), BF16),
        jax.ShapeDtypeStruct((n, G_WIDTH), BF16),
        jax.ShapeDtypeStruct((n, G_WIDTH), BF16),
        jax.ShapeDtypeStruct((n, G_KEY_WIDTH), F32),
    )
    feat_spec = pl.BlockSpec((None, A_WIDTH, tm), lambda i: (i // nl, 0, i % nl))
    kv_spec = pl.BlockSpec((None, None, A_WIDTH, tm), lambda i: (layer, i // nl, 0, i % nl))
    in_specs = [
        pl.BlockSpec((tm, d), row),
        pl.BlockSpec((1, d), const),
        pl.BlockSpec(w_row.shape, const),
        pl.BlockSpec(w_t.shape, const),
        pl.BlockSpec((LANES, G_KEY_WIDTH), const),
        pl.BlockSpec((1, G_KEY_WIDTH), const),
        pl.BlockSpec((ROT_HALF, tm), lambda i: (0, i % nl)),
        pl.BlockSpec((ROT_HALF, tm), lambda i: (0, i % nl)),
    ]
    args = [x, norm_g, w_row, w_t, wa2, ba2, ct_tab, st_tab]
    aliases = {}
    if kv_prev is not None:
        in_specs += [pl.BlockSpec(memory_space=pl.ANY)] * 2
        aliases = {len(args): 1, len(args) + 1: 2}
        args += list(kv_prev)
    return pl.pallas_call(
        functools.partial(_proj_in_kernel, n_alias=len(aliases)),
        grid=(n // tm,),
        in_specs=in_specs,
        out_specs=(
            feat_spec, kv_spec, kv_spec, feat_spec,
            pl.BlockSpec((tm, G_KEY_WIDTH), row), pl.BlockSpec((tm, G_KEY_WIDTH), row),
            pl.BlockSpec((tm, G_WIDTH), row), pl.BlockSpec((tm, G_WIDTH), row),
            pl.BlockSpec((tm, G_KEY_WIDTH), row),
        ),
        out_shape=out_shapes,
        input_output_aliases=aliases,
        compiler_params=pltpu.CompilerParams(dimension_semantics=("arbitrary",), vmem_limit_bytes=VMEM_LIMIT_BYTES),
        name="proj_in",
    )(*args)


def _proj_out_kernel(a_ref, g_ref, x_ref, w_ref, fg_ref, o_ref, *, final, a_feature_major):
    half = g_ref.shape[1]
    a = a_ref[...].astype(BF16)
    ya = _dot_tn(a, w_ref[0:half, :]) if a_feature_major else _dot(a, w_ref[0:half, :])
    xo = x_ref[...] + (ya + _dot(g_ref[...].astype(BF16), w_ref[half:, :]))
    if final:
        ms = jnp.mean(xo * xo, axis=-1, keepdims=True)
        xo = xo * lax.rsqrt(ms + NORM_EPS) * fg_ref[...]
    o_ref[...] = xo


def _proj_out(a, g, x, w, fg, tm, final, a_feature_major=False):
    n, d = x.shape
    row = lambda i: (i, 0)
    const = lambda i: (0, 0)
    a_spec = pl.BlockSpec((a.shape[0], tm), lambda i: (0, i)) if a_feature_major else pl.BlockSpec((tm, a.shape[1]), row)
    return pl.pallas_call(
        functools.partial(_proj_out_kernel, final=final, a_feature_major=a_feature_major),
        grid=(n // tm,),
        in_specs=[
            a_spec,
            pl.BlockSpec((tm, g.shape[1]), row),
            pl.BlockSpec((tm, d), row),
            pl.BlockSpec(w.shape, const),
            pl.BlockSpec((1, d), const),
        ],
        out_specs=pl.BlockSpec((tm, d), row),
        out_shape=jax.ShapeDtypeStruct((n, d), F32),
        compiler_params=pltpu.CompilerParams(dimension_semantics=("arbitrary",), vmem_limit_bytes=VMEM_LIMIT_BYTES),
        name="proj_out",
    )(a, g, x, w, fg)


def _moba_prompt_kernel(qt_ref, kt_ref, vt_ref, sgt_ref, o_ref, ka_ref, va_ref, km_ref, qa_ref, s_ref, acc_ref, m_ref,
                        *, nb, nsel, unroll):
    qi = pl.program_id(2)
    blk = MOBA_BLOCK
    dh = A_HEAD_DIM
    heads = qt_ref.shape[0] // dh
    per_group = LANES // dh

    @pl.when(qi == 0)
    def _prepare():
        lane = lax.broadcasted_iota(jnp.int32, (blk, LANES), 1)
        ones_rows = jnp.where(lax.broadcasted_iota(jnp.int32, (V_PAD_ROWS, blk), 0) == 0, 1.0, 0.0)
        for j in range(nb):
            kj = kt_ref[:, j * blk:(j + 1) * blk].T
            km_ref[j:j + 1, :] = jnp.mean(kj, axis=0, keepdims=True)
            onehot = jnp.where(lane == dh + j, 1.0, 0.0)
            for hh in range(heads):
                grp, sub = divmod(hh, per_group)
                kh = kj[:, grp * LANES:(grp + 1) * LANES]
                if sub:
                    kh = pltpu.roll(kh, LANES - sub * dh, 1)
                ka_ref[hh, j * blk:(j + 1) * blk, :] = jnp.where(lane < dh, kh, onehot).astype(BF16)
                vh = vt_ref[hh * dh:(hh + 1) * dh, j * blk:(j + 1) * blk]
                va_ref[j, hh] = jnp.concatenate([vh, ones_rows], axis=0).astype(BF16)
        m_ref[...] = jnp.zeros_like(m_ref)

    @pl.when(jnp.logical_and(qi == 0, jnp.logical_and(pl.program_id(0) == 0, pl.program_id(1) == 0)))
    def _define_scores():
        s_ref[...] = jnp.zeros_like(s_ref)

    qb = jnp.minimum(qi, nb - 1)
    prev = jnp.maximum(qi - 1, 0)
    m_old = [m_ref[hh] for hh in range(heads)]

    for hh in range(heads):
        acc_ref[hh] = _dot(va_ref[prev, hh], jnp.exp(s_ref[hh, nb] - m_old[hh]).astype(BF16))

    rowf = lax.broadcasted_iota(jnp.int32, (nb, blk), 0).astype(F32)
    past = rowf < qb.astype(F32)
    causal = lax.broadcasted_iota(jnp.int32, (blk, blk), 0) <= lax.broadcasted_iota(jnp.int32, (blk, blk), 1)
    pad = jnp.zeros((LANES - dh - nb, blk), BF16)
    r_own = pl.multiple_of(qb * blk, blk)

    m_run, qd = [], []
    for hh in range(heads):
        qh = qt_ref[hh * dh:(hh + 1) * dh, :]
        km = km_ref[:, hh * dh:(hh + 1) * dh]
        km1 = km.astype(BF16)
        km2 = (km - km1.astype(F32)).astype(BF16)
        g = jnp.where(past, _dot(km1, qh) + _dot(km2, qh), NEG_INF)
        sel = jnp.zeros((nb, blk), F32)
        for _ in range(nsel):
            m = jnp.max(g, axis=0, keepdims=True)
            idx = jnp.min(jnp.where(g == m, rowf, float(nb)), axis=0, keepdims=True)
            pick = jnp.logical_and(rowf == idx, m > NEG_INF)
            sel = jnp.where(pick, 1.0, sel)
            g = jnp.where(pick, NEG_INF, g)
        bias = jnp.where(sel > 0.0, 0.0, MASK_BIAS).astype(BF16)
        qa_ref[hh] = jnp.concatenate([qh, bias, pad], axis=0)
        qd.append(jnp.concatenate([qh, jnp.zeros((LANES - dh, blk), BF16)], axis=0))

    for hh in range(heads):
        s = jnp.where(causal, _dot(ka_ref[hh, pl.ds(r_own, blk), :], qd[hh]), MASK_BIAS)
        s_ref[hh, nb] = s
        m_run.append(jnp.max(s, axis=0, keepdims=True))

    def blocks(j0, count, ms):
        ms = list(ms)
        for hh in range(heads):
            part = None
            for u in range(count):
                j = j0 + u
                p = jnp.where(j < prev, jnp.exp(s_ref[hh, j] - m_old[hh]), 0.0).astype(BF16)
                t = _dot(va_ref[j, hh], p)
                part = t if part is None else part + t
                r0 = pl.multiple_of(j * blk, blk)
                s = _dot(ka_ref[hh, pl.ds(r0, blk), :], qa_ref[hh])
                s_ref[hh, j] = s
                ms[hh] = jnp.maximum(ms[hh], jnp.max(s, axis=0, keepdims=True))
            acc_ref[hh] += part
        return tuple(ms)

    m_fin = tuple(m_run)
    done = 0
    width = unroll
    while width >= 1:
        trips = lax.shift_right_logical(qi - done, width.bit_length() - 1)
        m_fin = lax.fori_loop(0, trips, functools.partial(
            lambda gi, ms, j0, w: blocks(j0 + gi * w, w, ms), j0=done, w=width), m_fin)
        done = done + trips * width
        width //= 2
    for hh in range(heads):
        m_ref[hh] = m_fin[hh]

    ot = jnp.concatenate([acc_ref[hh, 0:dh, :] / acc_ref[hh, dh:dh + 1, :] for hh in range(heads)], axis=0)
    o_ref[...] = (ot * sgt_ref[...].astype(F32)).T.astype(o_ref.dtype)


def _moba_prompt(qt, kt_all, vt_all, sgt, layer):
    b, width, seq = qt.shape
    nb = seq // MOBA_BLOCK
    nsel = min(MOBA_TOPK, nb)
    heads = MOBA_HEADS_PER_STEP
    hw = heads * A_HEAD_DIM
    assert A_HEAD_DIM + nb <= LANES and width % hw == 0
    unroll = MOBA_BLOCKS_PER_ITERATION
    q_spec = pl.BlockSpec((None, hw, MOBA_BLOCK), lambda bi, hp, qi: (bi, hp, jnp.minimum(qi, nb - 1)))
    g_spec = pl.BlockSpec((None, hw, MOBA_BLOCK), lambda bi, hp, qi: (bi, hp, jnp.maximum(qi - 1, 0)))
    kv_spec = pl.BlockSpec((None, None, hw, seq), lambda bi, hp, qi: (layer, bi, hp, 0))
    return pl.pallas_call(
        functools.partial(_moba_prompt_kernel, nb=nb, nsel=nsel, unroll=unroll),
        grid=(b, width // hw, nb + 1),
        in_specs=[q_spec, kv_spec, kv_spec, g_spec],
        out_specs=pl.BlockSpec((None, MOBA_BLOCK, hw), lambda bi, hp, qi: (bi, jnp.maximum(qi - 1, 0), hp)),
        out_shape=jax.ShapeDtypeStruct((b, seq, width), BF16),
        scratch_shapes=[
            pltpu.VMEM((heads, seq, LANES), BF16),
            pltpu.VMEM((nb, heads, A_HEAD_DIM + V_PAD_ROWS, MOBA_BLOCK), BF16),
            pltpu.VMEM((nb, hw), F32),
            pltpu.VMEM((heads, LANES, MOBA_BLOCK), BF16),
            pltpu.VMEM((heads, nb + 1, MOBA_BLOCK, MOBA_BLOCK), F32),
            pltpu.VMEM((heads, A_HEAD_DIM + V_PAD_ROWS, MOBA_BLOCK), F32),
            pltpu.VMEM((heads, 1, MOBA_BLOCK), F32),
        ],
        compiler_params=pltpu.CompilerParams(
            dimension_semantics=("arbitrary", "arbitrary", "arbitrary"), vmem_limit_bytes=VMEM_LIMIT_BYTES),
        name="moba_prompt",
    )(qt, kt_all, vt_all, sgt)


def _gla_prompt_kernel(q_ref, k_ref, v_ref, lf_ref, sg_ref, gn_ref, o_ref, s_ref, st_ref, *, nt):
    t = pl.program_id(1)
    tt = q_ref.shape[1]
    ck = G_CHUNK
    dk, dv = G_KEY_DIM, G_VAL_DIM
    nh = G_HEADS

    @pl.when(t == 0)
    def _init():
        st_ref[...] = jnp.zeros_like(st_ref)

    g1, g2, g3 = _split3(lf_ref[0])
    row = lax.broadcasted_iota(jnp.int32, (tt, tt), 0)
    col = lax.broadcasted_iota(jnp.int32, (tt, tt), 1)
    ck_shift = ck.bit_length() - 1
    dk_shift = dk.bit_length() - 1
    same = lax.shift_right_logical(row, ck_shift) == lax.shift_right_logical(col, ck_shift)
    lower = jnp.where(jnp.logical_and(same, col <= row), 1.0, 0.0).astype(BF16)
    upper = jnp.where(jnp.logical_and(same, col > row), 1.0, 0.0).astype(BF16)
    bcum = _dot(lower, g1) + _dot(lower, g2) + _dot(lower, g3)
    rest = _dot(upper, g1) + _dot(upper, g2) + _dot(upper, g3)

    qf = q_ref[0].astype(F32)
    kf = k_ref[0].astype(F32)
    qt = qf * jnp.exp(bcum) * (dk ** -0.5)
    kt = kf * jnp.exp(-bcum)
    kd = kf * jnp.exp(rest)

    head_of_lane = lax.shift_right_logical(lax.broadcasted_iota(jnp.int32, (ck, nh * dk), 1), dk_shift)
    own = [head_of_lane == hd for hd in range(nh)]
    ri = lax.broadcasted_iota(jnp.int32, (nh * ck, nh * ck), 0)
    ci = lax.broadcasted_iota(jnp.int32, (nh * ck, nh * ck), 1)
    causal = jnp.bitwise_and(ri, ck - 1) >= jnp.bitwise_and(ci, ck - 1)
    gn = gn_ref[...]
    st = st_ref[...]

    def stack(x):
        return jnp.concatenate([jnp.where(own[hd], x, 0.0) for hd in range(nh)], axis=0).astype(BF16)

    nc = tt // ck
    q4s, o_intra, kvs = [], [], []
    for c in range(nc):
        r0, r1 = c * ck, (c + 1) * ck
        q4, k4, kd4 = stack(qt[r0:r1]), stack(kt[r0:r1]), stack(kd[r0:r1])
        v4 = jnp.concatenate([v_ref[0, r0:r1, hd * dv:(hd + 1) * dv] for hd in range(nh)], axis=0)
        a = jnp.where(causal, _dot_nt(q4, k4), 0.0)
        q4s.append(q4)
        o_intra.append(_dot(a.astype(BF16), v4))
        kvs.append(_dot_tn(v4, kd4))

    for c in range(nc):
        r0, r1 = c * ck, (c + 1) * ck
        o = o_intra[c] + _dot_nt(q4s[c], st.astype(BF16))
        ms = jnp.mean(o * o, axis=-1, keepdims=True)
        on = o * lax.rsqrt(ms + NORM_EPS) * gn
        for hd in range(nh):
            sg = sg_ref[0, r0:r1, hd * dv:(hd + 1) * dv].astype(F32)
            o_ref[0, r0:r1, hd * dv:(hd + 1) * dv] = (on[hd * ck:(hd + 1) * ck] * sg).astype(o_ref.dtype)
        st = st * jnp.exp(bcum[r1 - 1:r1, :]) + kvs[c]

    st_ref[...] = st

    @pl.when(t == nt - 1)
    def _emit():
        per_group = LANES // dk
        for grp in range(nh // per_group):
            s_grp = st[:, grp * LANES:(grp + 1) * LANES].T
            for hh in range(per_group):
                s_ref[0, grp * per_group + hh] = s_grp[hh * dk:(hh + 1) * dk, :]


def _gla_prompt(qg, kg, vg, lf, sg, gn, tt):
    b, seq, _ = qg.shape
    nt = seq // tt
    kspec = pl.BlockSpec((1, tt, G_KEY_WIDTH), lambda bi, ti: (bi, ti, 0))
    vspec = pl.BlockSpec((1, tt, G_WIDTH), lambda bi, ti: (bi, ti, 0))
    return pl.pallas_call(
        functools.partial(_gla_prompt_kernel, nt=nt),
        grid=(b, nt),
        in_specs=[kspec, kspec, vspec, kspec, vspec, pl.BlockSpec((1, G_VAL_DIM), lambda bi, ti: (0, 0))],
        out_specs=(vspec, pl.BlockSpec((1, G_HEADS, G_KEY_DIM, G_VAL_DIM), lambda bi, ti: (bi, 0, 0, 0))),
        out_shape=(jax.ShapeDtypeStruct((b, seq, G_WIDTH), BF16),
                   jax.ShapeDtypeStruct((b, G_HEADS, G_KEY_DIM, G_VAL_DIM), F32)),
        scratch_shapes=[pltpu.VMEM((G_VAL_DIM, G_KEY_WIDTH), F32)],
        compiler_params=pltpu.CompilerParams(
            dimension_semantics=("arbitrary", "arbitrary"), vmem_limit_bytes=VMEM_LIMIT_BYTES),
        name="gla_prompt",
    )(qg, kg, vg, lf, sg, gn)


def _sample_score_kernel(pt_ref, qt_ref, kt_ref, kc_ref, pn_ref, idx_ref, pown_ref,
                         buf_ref, sem_ref, qb_ref, sc_ref, *, layer, npages, nbuf, bs, nsel):
    b = pl.program_id(0)
    total = bs * npages
    nblk = npages // PAGES_PER_BLOCK
    dh = A_HEAD_DIM

    def page_copy(g, slot):
        bb = g // npages
        return pltpu.make_async_copy(kc_ref.at[layer, pt_ref[bb, g - bb * npages]], buf_ref.at[slot], sem_ref.at[slot])

    @pl.when(b == 0)
    def _prime():
        for s in range(nbuf):
            page_copy(s, s).start()

    qcol = _lane_column(qt_ref[...], b)
    kcol = _lane_column(kt_ref[...], b)
    for h in range(A_HEADS):
        qb_ref[h] = jnp.broadcast_to(qcol[h], (dh, PAGE_SIZE))

    def page_body(p, _):
        g = b * npages + p
        slot = lax.rem(g, nbuf)
        page_copy(g, slot).wait()
        for h in range(A_HEADS):
            sc_ref[p, h:h + 1, :] = jnp.sum(buf_ref[slot, h] * qb_ref[h], axis=0, keepdims=True)

        @pl.when(g + nbuf < total)
        def _next():
            page_copy(g + nbuf, slot).start()

        return 0

    lax.fori_loop(0, npages, page_body, 0)

    lane = lax.broadcasted_iota(jnp.int32, (A_HEADS, LANES), 1).astype(F32)
    sub = lax.broadcasted_iota(jnp.int32, (A_HEADS, LANES), 0)

    gate = jnp.full((A_HEADS, LANES), NEG_INF, F32)
    for j in range(nblk):
        sblk = sc_ref[PAGES_PER_BLOCK * j]
        for pp in range(1, PAGES_PER_BLOCK):
            sblk = sblk + sc_ref[PAGES_PER_BLOCK * j + pp]
        gj = jnp.sum(sblk, axis=1, keepdims=True) * (1.0 / MOBA_BLOCK)
        gate = jnp.where(lane == float(j), gj, gate)

    g = gate
    sel = jnp.zeros((A_HEADS, LANES), F32)
    idx_out = jnp.full((A_HEADS, LANES), -1.0, F32)
    for r in range(nsel):
        m = jnp.max(g, axis=1, keepdims=True)
        idx = jnp.min(jnp.where(g == m, lane, float(LANES)), axis=1, keepdims=True)
        pick = jnp.logical_and(lane == idx, m > NEG_INF)
        sel = jnp.where(pick, 1.0, sel)
        g = jnp.where(pick, NEG_INF, g)
        idx_out = jnp.where(lane == float(r), jnp.where(m > NEG_INF, idx, -1.0), idx_out)
    idx_ref[0] = idx_out.astype(jnp.int32)

    s_own = jnp.zeros((A_HEADS, LANES), F32)
    for h in range(A_HEADS):
        so = jnp.sum(qcol[h] * kcol[h], axis=0, keepdims=True)
        s_own = jnp.where(sub == h, jnp.broadcast_to(so, (A_HEADS, LANES)), s_own)

    masks = [jnp.broadcast_to(sel[:, j:j + 1], (A_HEADS, LANES)) > 0.0 for j in range(nblk)]
    mx = s_own
    for pg in range(npages):
        mx = jnp.maximum(mx, jnp.where(masks[pg // PAGES_PER_BLOCK], sc_ref[pg], NEG_INF))
    m = jnp.max(mx, axis=1, keepdims=True)
    lsum = jnp.zeros((A_HEADS, LANES), F32)
    for pg in range(npages):
        p = jnp.where(masks[pg // PAGES_PER_BLOCK], jnp.exp(sc_ref[pg] - m), 0.0)
        pn_ref[0, pg] = p
        lsum = lsum + p
    p_own = jnp.exp(s_own - m)
    inv = 1.0 / (jnp.sum(lsum, axis=1, keepdims=True) + p_own)
    for pg in range(npages):
        pn_ref[0, pg] = pn_ref[0, pg] * inv
    pown_ref[0] = p_own * inv


def _sample_score(page_table, qt, kt, cache_t, layer, nbuf=SAMPLE_PAGES_IN_FLIGHT):
    bs, npages = page_table.shape
    nblk = npages // PAGES_PER_BLOCK
    nsel = min(MOBA_TOPK, nblk + 1)
    feat = pl.BlockSpec((A_HEADS, A_HEAD_DIM, bs), lambda b, pt: (0, 0, 0))
    grid_spec = pltpu.PrefetchScalarGridSpec(
        num_scalar_prefetch=1,
        grid=(bs,),
        in_specs=[feat, feat, pl.BlockSpec(memory_space=pl.ANY)],
        out_specs=(
            pl.BlockSpec((1, npages, A_HEADS, PAGE_SIZE), lambda b, pt: (b, 0, 0, 0)),
            pl.BlockSpec((1, A_HEADS, LANES), lambda b, pt: (b, 0, 0)),
            pl.BlockSpec((1, A_HEADS, LANES), lambda b, pt: (b, 0, 0)),
        ),
        scratch_shapes=[
            pltpu.VMEM((min(nbuf, npages), A_HEADS, A_HEAD_DIM, PAGE_SIZE), F32),
            pltpu.SemaphoreType.DMA((min(nbuf, npages),)),
            pltpu.VMEM((A_HEADS, A_HEAD_DIM, PAGE_SIZE), F32),
            pltpu.VMEM((npages, A_HEADS, PAGE_SIZE), F32),
        ],
    )
    return pl.pallas_call(
        functools.partial(_sample_score_kernel, layer=layer, npages=npages, nbuf=min(nbuf, npages), bs=bs, nsel=nsel),
        grid_spec=grid_spec,
        out_shape=(
            jax.ShapeDtypeStruct((bs, npages, A_HEADS, PAGE_SIZE), F32),
            jax.ShapeDtypeStruct((bs, A_HEADS, LANES), jnp.int32),
            jax.ShapeDtypeStruct((bs, A_HEADS, LANES), F32),
        ),
        compiler_params=pltpu.CompilerParams(dimension_semantics=("arbitrary",), vmem_limit_bytes=VMEM_LIMIT_BYTES),
        name="sample_score",
    )(page_table, qt, kt, cache_t)


def _sample_pv_kernel(pt_ref, idx_ref, pn_ref, pown_ref, vt_ref, sgt_ref, vc_ref, o_ref,
                      vbuf_ref, sem_ref, *, layer, nsel, bs):
    b = pl.program_id(0)
    slot = lax.rem(b, 2)

    def issue(bb, sl):
        for h in range(A_HEADS):
            for r in range(nsel):
                j = jnp.maximum(idx_ref[bb, h * nsel + r], 0)
                for pp in range(PAGES_PER_BLOCK):
                    pg = pt_ref[bb, PAGES_PER_BLOCK * j + pp]
                    pltpu.make_async_copy(vc_ref.at[layer, pg, h], vbuf_ref.at[sl, h, r, pp], sem_ref.at[sl]).start()

    @pl.when(b == 0)
    def _first():
        issue(0, 0)
        o_ref[...] = jnp.zeros_like(o_ref)

    @pl.when(b + 1 < bs)
    def _prefetch():
        issue(b + 1, 1 - slot)

    for _ in range(A_HEADS * nsel * PAGES_PER_BLOCK):
        pltpu.make_async_copy(vc_ref.at[layer, 0, 0], vbuf_ref.at[slot, 0, 0, 0], sem_ref.at[slot]).wait()

    vcol = _lane_column(vt_ref[...], b)
    sgcol = _lane_column(sgt_ref[...], b)
    mine = lax.broadcasted_iota(jnp.int32, (1, bs), 1) == b
    for h in range(A_HEADS):
        acc = jnp.zeros((A_HEAD_DIM, PAGE_SIZE), F32)
        for r in range(nsel):
            jraw = idx_ref[b, h * nsel + r]
            j = jnp.maximum(jraw, 0)
            w = jnp.where(jraw >= 0, 1.0, 0.0)
            for pp in range(PAGES_PER_BLOCK):
                prow = pn_ref[0, PAGES_PER_BLOCK * j + pp, h:h + 1, :] * w
                acc = acc + vbuf_ref[slot, h, r, pp] * prow
        ocol = jnp.sum(acc, axis=1, keepdims=True) + pown_ref[0, h:h + 1, 0:1] * vcol[h]
        o_ref[h] = jnp.where(mine, jnp.broadcast_to(ocol * sgcol[h], (A_HEAD_DIM, bs)), o_ref[h])


def _sample_pv(page_table, idx, pn, pown, vt, sgt, cache_t, layer):
    bs, npages = page_table.shape
    nsel = idx.shape[1] // A_HEADS
    feat = pl.BlockSpec((A_HEADS, A_HEAD_DIM, bs), lambda b, pt, ix: (0, 0, 0))
    grid_spec = pltpu.PrefetchScalarGridSpec(
        num_scalar_prefetch=2,
        grid=(bs,),
        in_specs=[
            pl.BlockSpec((1, npages, A_HEADS, PAGE_SIZE), lambda b, pt, ix: (b, 0, 0, 0)),
            pl.BlockSpec((1, A_HEADS, LANES), lambda b, pt, ix: (b, 0, 0)),
            feat, feat,
            pl.BlockSpec(memory_space=pl.ANY),
        ],
        out_specs=feat,
        scratch_shapes=[
            pltpu.VMEM((2, A_HEADS, nsel, PAGES_PER_BLOCK, A_HEAD_DIM, PAGE_SIZE), F32),
            pltpu.SemaphoreType.DMA((2,)),
        ],
    )
    return pl.pallas_call(
        functools.partial(_sample_pv_kernel, layer=layer, nsel=nsel, bs=bs),
        grid_spec=grid_spec,
        out_shape=jax.ShapeDtypeStruct((A_HEADS, A_HEAD_DIM, bs), F32),
        compiler_params=pltpu.CompilerParams(dimension_semantics=("arbitrary",), vmem_limit_bytes=VMEM_LIMIT_BYTES),
        name="sample_pv",
    )(page_table, idx, pn, pown, vt, sgt, cache_t)


def _gla_sample_kernel(qt_ref, kt_ref, gt_ref, v_ref, sg_ref, s0_ref, gn_ref, s_ref, o_ref):
    b = pl.program_id(0)
    dv = G_VAL_DIM
    bs = v_ref.shape[0]

    @pl.when(b == 0)
    def _init():
        o_ref[...] = jnp.zeros_like(o_ref)

    qcol = _lane_column(qt_ref[...], b)
    kcol = _lane_column(kt_ref[...], b)
    gcol = _lane_column(gt_ref[...], b)
    mine = lax.broadcasted_iota(jnp.int32, (bs, dv), 0) == b
    for h in range(G_HEADS):
        cols = slice(h * dv, (h + 1) * dv)
        v = jnp.sum(jnp.where(mine, v_ref[:, cols], 0.0), axis=0, keepdims=True)
        sg = jnp.sum(jnp.where(mine, sg_ref[:, cols], 0.0), axis=0, keepdims=True)
        s_new = jnp.exp(gcol[h]) * s0_ref[0, h] + kcol[h] * v
        s_ref[0, h] = s_new
        o = jnp.sum(qcol[h] * s_new, axis=0, keepdims=True) * (G_KEY_DIM ** -0.5)
        ms = jnp.mean(o * o, axis=-1, keepdims=True)
        on = o * lax.rsqrt(ms + NORM_EPS) * gn_ref[...] * sg
        o_ref[:, cols] = jnp.where(mine, jnp.broadcast_to(on, (bs, dv)), o_ref[:, cols])


def _gla_sample(qt, kt, gt, v, sg, s0, gn):
    bs = s0.shape[0]
    feat = pl.BlockSpec((G_HEADS, G_KEY_DIM, bs), lambda b: (0, 0, 0))
    tok = pl.BlockSpec((bs, G_WIDTH), lambda b: (0, 0))
    sspec = pl.BlockSpec((1, G_HEADS, G_KEY_DIM, G_VAL_DIM), lambda b: (b, 0, 0, 0))
    return pl.pallas_call(
        _gla_sample_kernel,
        grid=(bs,),
        in_specs=[feat, feat, feat, tok, tok, sspec, pl.BlockSpec((1, G_VAL_DIM), lambda b: (0, 0))],
        out_specs=(sspec, tok),
        out_shape=(jax.ShapeDtypeStruct(s0.shape, F32), jax.ShapeDtypeStruct((bs, G_WIDTH), F32)),
        compiler_params=pltpu.CompilerParams(dimension_semantics=("arbitrary",)),
        name="gla_sample",
    )(qt, kt, gt, v, sg, s0, gn)


def _rope_tables(pos):
    inv = jnp.power(jnp.float32(ROPE_THETA), -jnp.arange(ROT_HALF, dtype=F32) * (2.0 / ROT_DIM))
    ang = pos.astype(F32)[:, None] * inv[None, :]
    return jnp.cos(ang).T, jnp.sin(ang).T


def _pick_tile(n, pref):
    t = min(n, pref)
    while n % t:
        t //= 2
    return t


def kernel(x_prompt, x_sample, cache_k, cache_v, state_gla, page_table, norm_g, w_in, w_a2, b_a2, gla_norm_g, w_out, final_norm_g):
    bp, lp, d = x_prompt.shape
    bs, ls, _ = x_sample.shape
    depth = norm_g.shape[0]
    past_len = page_table.shape[1] * PAGE_SIZE
    assert ls == 1 and lp % MOBA_BLOCK == 0 and past_len % MOBA_BLOCK == 0
    assert cache_k.shape[2:] == (PAGE_SIZE, A_HEADS, A_HEAD_DIM)

    tabs_p = _rope_tables(jnp.arange(lp, dtype=jnp.int32))
    tabs_s = _rope_tables(jnp.full((bs,), past_len, dtype=jnp.int32))
    kc_t = jnp.transpose(cache_k, (0, 1, 3, 4, 2))
    vc_t = jnp.transpose(cache_v, (0, 1, 3, 4, 2))

    tm_p = _pick_tile(lp, 512)
    tt = _pick_tile(lp, 256)
    hp = x_prompt.reshape(bp * lp, d)
    hs = x_sample.reshape(bs, d)
    fg = final_norm_g.reshape(1, d)
    sp_l, ks_l, vs_l, ss_l = [], [], [], []
    kv_prompt = (jnp.zeros((depth, bp, A_WIDTH, lp), F32), jnp.zeros((depth, bp, A_WIDTH, lp), F32))
    rows = A_WIDTH
    g_lo = 4 * rows
    g_hi = g_lo + 2 * G_KEY_WIDTH + 2 * G_WIDTH
    for l in range(depth):
        wt = jnp.transpose(w_in[l])
        w_t = wt[0:4 * rows].astype(BF16)
        a1_t = jnp.pad(wt[g_hi:], ((0, LANES - G_GATE_RANK), (0, 0)))
        w_row = jnp.transpose(jnp.concatenate([wt[g_lo:g_hi], a1_t], axis=0)).astype(BF16)
        wa2 = jnp.pad(w_a2[l], ((0, LANES - G_GATE_RANK), (0, 0))).astype(BF16)
        ba2 = b_a2[l].reshape(1, G_KEY_WIDTH)
        ng = norm_g[l].reshape(1, d)
        gn = gla_norm_g[l].reshape(1, G_VAL_DIM)
        wo = w_out[l].astype(BF16)
        last = l == depth - 1

        qt, kt_all, vt_all, sgt, qg, kg, vg, sgg, lf = _proj_in(
            hp, ng, w_row, w_t, wa2, ba2, tabs_p, bp, tm_p, l, depth, kv_prompt)
        kv_prompt = (kt_all, vt_all)
        oa = _moba_prompt(qt, kt_all, vt_all, sgt, l)
        og, s_fin = _gla_prompt(qg.reshape(bp, lp, -1), kg.reshape(bp, lp, -1), vg.reshape(bp, lp, -1),
                                lf.reshape(bp, lp, -1), sgg.reshape(bp, lp, -1), gn, tt)
        hp = _proj_out(oa.reshape(bp * lp, A_WIDTH), og.reshape(bp * lp, G_WIDTH), hp, wo, fg, tm_p, last)
        sp_l.append(s_fin)

        qt, kt_s, vt_s, sgt, qg, kg, vg, sgg, lf = _proj_in(hs, ng, w_row, w_t, wa2, ba2, tabs_s, 1, bs, 0, 1)
        a_feat = lambda a: a.reshape(A_HEADS, A_HEAD_DIM, bs)
        pn, idx, pown = _sample_score(page_table, a_feat(qt), a_feat(kt_s), kc_t, l)
        nsel = min(MOBA_TOPK, page_table.shape[1] // PAGES_PER_BLOCK + 1)
        idx_s = idx[:, :, :nsel].reshape(bs, A_HEADS * nsel)
        oa_s = _sample_pv(page_table, idx_s, pn, pown, a_feat(vt_s), a_feat(sgt), vc_t, l)
        g_feat = lambda a: jnp.transpose(a.astype(F32)).reshape(G_HEADS, G_KEY_DIM, bs)
        s_new, og_s = _gla_sample(g_feat(qg), g_feat(kg), g_feat(lf), vg.astype(F32), sgg.astype(F32),
                                  state_gla[l], gn)
        hs = _proj_out(oa_s.reshape(A_WIDTH, bs), og_s, hs, wo, fg, bs, last, a_feature_major=True)
        ks_l.append(jnp.transpose(kt_s[0, 0]).reshape(bs, 1, A_HEADS, A_HEAD_DIM))
        vs_l.append(jnp.transpose(vt_s[0, 0]).reshape(bs, 1, A_HEADS, A_HEAD_DIM))
        ss_l.append(s_new)

    kt_all, vt_all = kv_prompt
    to_cache_layout = lambda a: jnp.transpose(a.reshape(depth, bp, A_HEADS, A_HEAD_DIM, lp), (0, 1, 4, 2, 3))
    return (hp.reshape(bp, lp, d), hs.reshape(bs, ls, d), to_cache_layout(kt_all), to_cache_layout(vt_all),
            jnp.stack(sp_l), jnp.stack(ks_l), jnp.stack(vs_l), jnp.stack(ss_l))
```

```python
import functools

import jax
import jax.numpy as jnp
from jax import lax
from jax.experimental import pallas as pl
from jax.experimental.pallas import tpu as pltpu

A_HEADS = 8
A_HEAD_DIM = 64
A_WIDTH = A_HEADS * A_HEAD_DIM
ROT_DIM = A_HEAD_DIM // 4
ROT_HALF = ROT_DIM // 2
ROPE_THETA = 500000.0
MOBA_BLOCK = 256
MOBA_TOPK = 3
G_HEADS = 4
G_KEY_DIM = 64
G_VAL_DIM = 128
G_KEY_WIDTH = G_HEADS * G_KEY_DIM
G_WIDTH = G_HEADS * G_VAL_DIM
G_GATE_RANK = 16
G_GATE_NORM = 16.0
G_CHUNK = 32
NORM_EPS = 1e-6
PAGE_SIZE = 128
PAGES_PER_BLOCK = MOBA_BLOCK // PAGE_SIZE

LANES = 128
VMEM_LIMIT_BYTES = 56 * 1024 * 1024

F32 = jnp.float32
BF16 = jnp.bfloat16
NEG_INF = float("-inf")
MASK_BIAS = -1e30
MOBA_HEADS_PER_STEP = 4
MOBA_BLOCKS_PER_ITERATION = 4
SAMPLE_PAGES_IN_FLIGHT = 24
V_PAD_ROWS = 16

_NT = (((1,), (1,)), ((), ()))
_TN = (((0,), (0,)), ((), ()))


def _dot(a, b):
    return jnp.dot(a, b, preferred_element_type=F32)


def _dot_nt(a, b):
    return lax.dot_general(a, b, _NT, preferred_element_type=F32)


def _dot_tn(a, b):
    return lax.dot_general(a, b, _TN, preferred_element_type=F32)


def _split3(x):
    x1 = x.astype(BF16)
    r1 = x - x1.astype(F32)
    x2 = r1.astype(BF16)
    x3 = (r1 - x2.astype(F32)).astype(BF16)
    return x1, x2, x3


def _lane_column(x, lane):
    pick = lax.broadcasted_iota(jnp.int32, (1,) * (x.ndim - 1) + (x.shape[-1],), x.ndim - 1) == lane
    return jnp.sum(jnp.where(pick, x.astype(F32), 0.0), axis=-1, keepdims=True)


def _proj_in_kernel(*refs, n_alias):
    x_ref, g_ref, w_ref, wt_ref, wa2_ref, ba2_ref, ct_ref, st_ref = refs[:8]
    qt_ref, kt_ref, vt_ref, sgt_ref, qg_ref, kg_ref, vg_ref, sgg_ref, lf_ref = refs[8 + n_alias:]
    x = x_ref[...]
    ms = jnp.mean(x * x, axis=-1, keepdims=True)
    h = (x * lax.rsqrt(ms + NORM_EPS) * g_ref[...]).astype(BF16)

    def proj(lo, hi):
        return _dot(h, w_ref[:, lo:hi])

    o = 0
    qg_ref[...] = proj(o, o + G_KEY_WIDTH).astype(qg_ref.dtype)
    o += G_KEY_WIDTH
    kg_ref[...] = proj(o, o + G_KEY_WIDTH).astype(kg_ref.dtype)
    o += G_KEY_WIDTH
    vg_ref[...] = proj(o, o + G_WIDTH).astype(vg_ref.dtype)
    o += G_WIDTH
    zg = proj(o, o + G_WIDTH)
    sgg_ref[...] = (zg * jax.nn.sigmoid(zg)).astype(sgg_ref.dtype)
    o += G_WIDTH
    a1 = proj(o, o + LANES)
    la = _dot(a1.astype(BF16), wa2_ref[...]) + ba2_ref[...]
    lf_ref[...] = -(jnp.maximum(-la, 0.0) + jnp.log1p(jnp.exp(-jnp.abs(la)))) * (1.0 / G_GATE_NORM)

    zt = _dot_nt(wt_ref[...], h)
    ct, st = ct_ref[...], st_ref[...]
    q_scale = A_HEAD_DIM ** -0.5
    for hd in range(A_HEADS):
        b0 = hd * A_HEAD_DIM
        x1 = zt[b0:b0 + ROT_HALF]
        x2 = zt[b0 + ROT_HALF:b0 + ROT_DIM]
        qh = jnp.concatenate([x1 * ct - x2 * st, x2 * ct + x1 * st, zt[b0 + ROT_DIM:b0 + A_HEAD_DIM]], axis=0)
        qt_ref[b0:b0 + A_HEAD_DIM, :] = (qh * q_scale).astype(qt_ref.dtype)
        k0 = b0 + A_WIDTH
        x1 = zt[k0:k0 + ROT_HALF]
        x2 = zt[k0 + ROT_HALF:k0 + ROT_DIM]
        kt_ref[b0:b0 + ROT_HALF, :] = x1 * ct - x2 * st
        kt_ref[b0 + ROT_HALF:b0 + ROT_DIM, :] = x2 * ct + x1 * st
        kt_ref[b0 + ROT_DIM:b0 + A_HEAD_DIM, :] = zt[k0 + ROT_DIM:k0 + A_HEAD_DIM]
    vt_ref[...] = zt[2 * A_WIDTH:3 * A_WIDTH]
    zg = zt[3 * A_WIDTH:4 * A_WIDTH]
    sgt_ref[...] = (zg * jax.nn.sigmoid(zg)).astype(sgt_ref.dtype)


def _proj_in(x, norm_g, w_row, w_t, wa2, ba2, tabs, n_batch, tm, layer, depth, kv_prev=None):
    n, d = x.shape
    seq = n // n_batch
    nl = seq // tm
    ct_tab, st_tab = tabs
    row = lambda i: (i, 0)
    const = lambda i: (0, 0)
    feat_shape = jax.ShapeDtypeStruct((n_batch, A_WIDTH, seq), BF16)
    kv_shape = jax.ShapeDtypeStruct((depth, n_batch, A_WIDTH, seq), F32)
    out_shapes = (
        feat_shape,
        kv_shape, kv_shape,
        feat_shape,
        jax.ShapeDtypeStruct((n, G_KEY_WIDTH), BF16),
        jax.ShapeDtypeStruct((n, G_KEY_WIDTH), BF16),
        jax.ShapeDtypeStruct((n, G_WIDTH), BF16),
        jax.ShapeDtypeStruct((n, G_WIDTH), BF16),
        jax.ShapeDtypeStruct((n, G_KEY_WIDTH), F32),
    )
    feat_spec = pl.BlockSpec((None, A_WIDTH, tm), lambda i: (i // nl, 0, i % nl))
    kv_spec = pl.BlockSpec((None, None, A_WIDTH, tm), lambda i: (layer, i // nl, 0, i % nl))
    in_specs = [
        pl.BlockSpec((tm, d), row),
        pl.BlockSpec((1, d), const),
        pl.BlockSpec(w_row.shape, const),
        pl.BlockSpec(w_t.shape, const),
        pl.BlockSpec((LANES, G_KEY_WIDTH), const),
        pl.BlockSpec((1, G_KEY_WIDTH), const),
        pl.BlockSpec((ROT_HALF, tm), lambda i: (0, i % nl)),
        pl.BlockSpec((ROT_HALF, tm), lambda i: (0, i % nl)),
    ]
    args = [x, norm_g, w_row, w_t, wa2, ba2, ct_tab, st_tab]
    aliases = {}
    if kv_prev is not None:
        in_specs += [pl.BlockSpec(memory_space=pl.ANY)] * 2
        aliases = {len(args): 1, len(args) + 1: 2}
        args += list(kv_prev)
    return pl.pallas_call(
        functools.partial(_proj_in_kernel, n_alias=len(aliases)),
        grid=(n // tm,),
        in_specs=in_specs,
        out_specs=(
            feat_spec, kv_spec, kv_spec, feat_spec,
            pl.BlockSpec((tm, G_KEY_WIDTH), row), pl.BlockSpec((tm, G_KEY_WIDTH), row),
            pl.BlockSpec((tm, G_WIDTH), row), pl.BlockSpec((tm, G_WIDTH), row),
            pl.BlockSpec((tm, G_KEY_WIDTH), row),
        ),
        out_shape=out_shapes,
        input_output_aliases=aliases,
        compiler_params=pltpu.CompilerParams(dimension_semantics=("arbitrary",), vmem_limit_bytes=VMEM_LIMIT_BYTES),
        name="proj_in",
    )(*args)


def _proj_out_kernel(a_ref, g_ref, x_ref, w_ref, fg_ref, o_ref, *, final, a_feature_major):
    half = g_ref.shape[1]
    a = a_ref[...].astype(BF16)
    ya = _dot_tn(a, w_ref[0:half, :]) if a_feature_major else _dot(a, w_ref[0:half, :])
    xo = x_ref[...] + (ya + _dot(g_ref[...].astype(BF16), w_ref[half:, :]))
    if final:
        ms = jnp.mean(xo * xo, axis=-1, keepdims=True)
        xo = xo * lax.rsqrt(ms + NORM_EPS) * fg_ref[...]
    o_ref[...] = xo


def _proj_out(a, g, x, w, fg, tm, final, a_feature_major=False):
    n, d = x.shape
    row = lambda i: (i, 0)
    const = lambda i: (0, 0)
    a_spec = pl.BlockSpec((a.shape[0], tm), lambda i: (0, i)) if a_feature_major else pl.BlockSpec((tm, a.shape[1]), row)
    return pl.pallas_call(
        functools.partial(_proj_out_kernel, final=final, a_feature_major=a_feature_major),
        grid=(n // tm,),
        in_specs=[
            a_spec,
            pl.BlockSpec((tm, g.shape[1]), row),
            pl.BlockSpec((tm, d), row),
            pl.BlockSpec(w.shape, const),
            pl.BlockSpec((1, d), const),
        ],
        out_specs=pl.BlockSpec((tm, d), row),
        out_shape=jax.ShapeDtypeStruct((n, d), F32),
        compiler_params=pltpu.CompilerParams(dimension_semantics=("arbitrary",), vmem_limit_bytes=VMEM_LIMIT_BYTES),
        name="proj_out",
    )(a, g, x, w, fg)


def _moba_prompt_kernel(qt_ref, kt_ref, vt_ref, sgt_ref, o_ref, ka_ref, va_ref, km_ref, qa_ref, s_ref, acc_ref, m_ref,
                        *, nb, nsel, unroll):
    qi = pl.program_id(2)
    blk = MOBA_BLOCK
    dh = A_HEAD_DIM
    heads = qt_ref.shape[0] // dh
    per_group = LANES // dh

    @pl.when(qi == 0)
    def _prepare():
        lane = lax.broadcasted_iota(jnp.int32, (blk, LANES), 1)
        ones_rows = jnp.where(lax.broadcasted_iota(jnp.int32, (V_PAD_ROWS, blk), 0) == 0, 1.0, 0.0)
        for j in range(nb):
            kj = kt_ref[:, j * blk:(j + 1) * blk].T
            km_ref[j:j + 1, :] = jnp.mean(kj, axis=0, keepdims=True)
            onehot = jnp.where(lane == dh + j, 1.0, 0.0)
            for hh in range(heads):
                grp, sub = divmod(hh, per_group)
                kh = kj[:, grp * LANES:(grp + 1) * LANES]
                if sub:
                    kh = pltpu.roll(kh, LANES - sub * dh, 1)
                ka_ref[hh, j * blk:(j + 1) * blk, :] = jnp.where(lane < dh, kh, onehot).astype(BF16)
                vh = vt_ref[hh * dh:(hh + 1) * dh, j * blk:(j + 1) * blk]
                va_ref[j, hh] = jnp.concatenate([vh, ones_rows], axis=0).astype(BF16)
        m_ref[...] = jnp.zeros_like(m_ref)

    @pl.when(jnp.logical_and(qi == 0, jnp.logical_and(pl.program_id(0) == 0, pl.program_id(1) == 0)))
    def _define_scores():
        s_ref[...] = jnp.zeros_like(s_ref)

    prev = jnp.maximum(qi - 1, 0)
    m_old = [m_ref[hh] for hh in range(heads)]

    for hh in range(heads):
        acc_ref[hh] = _dot(va_ref[prev, hh], jnp.exp(s_ref[hh, nb] - m_old[hh]).astype(BF16))

    def over_past_blocks(run, carry):
        done = 0
        width = unroll
        while width >= 1:
            trips = lax.shift_right_logical(qi - done, width.bit_length() - 1)
            carry = lax.fori_loop(0, trips, functools.partial(
                lambda gi, c, j0, w: run(j0 + gi * w, w, c), j0=done, w=width), carry)
            done = done + trips * width
            width //= 2
        return carry

    def pass2(hh, j, part):
        p = jnp.where(j < prev, jnp.exp(s_ref[hh, j] - m_old[hh]), 0.0).astype(BF16)
        t = _dot(va_ref[j, hh], p)
        return t if part is None else part + t

    @pl.when(qi < nb)
    def _step():
        rowf = lax.broadcasted_iota(jnp.int32, (nb, blk), 0).astype(F32)
        past = rowf < qi.astype(F32)
        causal = lax.broadcasted_iota(jnp.int32, (blk, blk), 0) <= lax.broadcasted_iota(jnp.int32, (blk, blk), 1)
        pad = jnp.zeros((LANES - dh - nb, blk), BF16)
        r_own = pl.multiple_of(qi * blk, blk)

        m_run, qd = [], []
        for hh in range(heads):
            qh = qt_ref[hh * dh:(hh + 1) * dh, :]
            km = km_ref[:, hh * dh:(hh + 1) * dh]
            km1 = km.astype(BF16)
            km2 = (km - km1.astype(F32)).astype(BF16)
            g = jnp.where(past, _dot(km1, qh) + _dot(km2, qh), NEG_INF)
            sel = jnp.zeros((nb, blk), F32)
            for _ in range(nsel):
                m = jnp.max(g, axis=0, keepdims=True)
                idx = jnp.min(jnp.where(g == m, rowf, float(nb)), axis=0, keepdims=True)
                pick = jnp.logical_and(rowf == idx, m > NEG_INF)
                sel = jnp.where(pick, 1.0, sel)
                g = jnp.where(pick, NEG_INF, g)
            bias = jnp.where(sel > 0.0, 0.0, MASK_BIAS).astype(BF16)
            qa_ref[hh] = jnp.concatenate([qh, bias, pad], axis=0)
            qd.append(jnp.concatenate([qh, jnp.zeros((LANES - dh, blk), BF16)], axis=0))

        for hh in range(heads):
            s = jnp.where(causal, _dot(ka_ref[hh, pl.ds(r_own, blk), :], qd[hh]), MASK_BIAS)
            s_ref[hh, nb] = s
            m_run.append(jnp.max(s, axis=0, keepdims=True))

        def blocks(j0, count, ms):
            ms = list(ms)
            for hh in range(heads):
                part = None
                for u in range(count):
                    j = j0 + u
                    part = pass2(hh, j, part)
                    r0 = pl.multiple_of(j * blk, blk)
                    s = _dot(ka_ref[hh, pl.ds(r0, blk), :], qa_ref[hh])
                    s_ref[hh, j] = s
                    ms[hh] = jnp.maximum(ms[hh], jnp.max(s, axis=0, keepdims=True))
                acc_ref[hh] += part
            return tuple(ms)

        m_fin = over_past_blocks(blocks, tuple(m_run))
        for hh in range(heads):
            m_ref[hh] = m_fin[hh]

    @pl.when(qi == nb)
    def _drain():
        def blocks(j0, count, carry):
            for hh in range(heads):
                part = None
                for u in range(count):
                    part = pass2(hh, j0 + u, part)
                acc_ref[hh] += part
            return carry

        over_past_blocks(blocks, 0)

    ot = jnp.concatenate([acc_ref[hh, 0:dh, :] / acc_ref[hh, dh:dh + 1, :] for hh in range(heads)], axis=0)
    o_ref[...] = (ot * sgt_ref[...].astype(F32)).T.astype(o_ref.dtype)


def _moba_prompt(qt, kt_all, vt_all, sgt, layer):
    b, width, seq = qt.shape
    nb = seq // MOBA_BLOCK
    nsel = min(MOBA_TOPK, nb)
    heads = MOBA_HEADS_PER_STEP
    hw = heads * A_HEAD_DIM
    assert A_HEAD_DIM + nb <= LANES and width % hw == 0
    unroll = MOBA_BLOCKS_PER_ITERATION
    q_spec = pl.BlockSpec((None, hw, MOBA_BLOCK), lambda bi, hp, qi: (bi, hp, jnp.minimum(qi, nb - 1)))
    g_spec = pl.BlockSpec((None, hw, MOBA_BLOCK), lambda bi, hp, qi: (bi, hp, jnp.maximum(qi - 1, 0)))
    kv_spec = pl.BlockSpec((None, None, hw, seq), lambda bi, hp, qi: (layer, bi, hp, 0))
    return pl.pallas_call(
        functools.partial(_moba_prompt_kernel, nb=nb, nsel=nsel, unroll=unroll),
        grid=(b, width // hw, nb + 1),
        in_specs=[q_spec, kv_spec, kv_spec, g_spec],
        out_specs=pl.BlockSpec((None, MOBA_BLOCK, hw), lambda bi, hp, qi: (bi, jnp.maximum(qi - 1, 0), hp)),
        out_shape=jax.ShapeDtypeStruct((b, seq, width), BF16),
        scratch_shapes=[
            pltpu.VMEM((heads, seq, LANES), BF16),
            pltpu.VMEM((nb, heads, A_HEAD_DIM + V_PAD_ROWS, MOBA_BLOCK), BF16),
            pltpu.VMEM((nb, hw), F32),
            pltpu.VMEM((heads, LANES, MOBA_BLOCK), BF16),
            pltpu.VMEM((heads, nb + 1, MOBA_BLOCK, MOBA_BLOCK), F32),
            pltpu.VMEM((heads, A_HEAD_DIM + V_PAD_ROWS, MOBA_BLOCK), F32),
            pltpu.VMEM((heads, 1, MOBA_BLOCK), F32),
        ],
        compiler_params=pltpu.CompilerParams(
            dimension_semantics=("arbitrary", "arbitrary", "arbitrary"), vmem_limit_bytes=VMEM_LIMIT_BYTES),
        name="moba_prompt",
    )(qt, kt_all, vt_all, sgt)


def _gla_prompt_kernel(q_ref, k_ref, v_ref, lf_ref, sg_ref, gn_ref, o_ref, s_ref, st_ref, *, nt):
    t = pl.program_id(1)
    tt = q_ref.shape[1]
    ck = G_CHUNK
    dk, dv = G_KEY_DIM, G_VAL_DIM
    nh = G_HEADS

    @pl.when(t == 0)
    def _init():
        st_ref[...] = jnp.zeros_like(st_ref)

    g1, g2, g3 = _split3(lf_ref[0])
    row = lax.broadcasted_iota(jnp.int32, (tt, tt), 0)
    col = lax.broadcasted_iota(jnp.int32, (tt, tt), 1)
    ck_shift = ck.bit_length() - 1
    dk_shift = dk.bit_length() - 1
    same = lax.shift_right_logical(row, ck_shift) == lax.shift_right_logical(col, ck_shift)
    lower = jnp.where(jnp.logical_and(same, col <= row), 1.0, 0.0).astype(BF16)
    upper = jnp.where(jnp.logical_and(same, col > row), 1.0, 0.0).astype(BF16)
    bcum = _dot(lower, g1) + _dot(lower, g2) + _dot(lower, g3)
    rest = _dot(upper, g1) + _dot(upper, g2) + _dot(upper, g3)

    qf = q_ref[0].astype(F32)
    kf = k_ref[0].astype(F32)
    qt = qf * jnp.exp(bcum) * (dk ** -0.5)
    kt = kf * jnp.exp(-bcum)
    kd = kf * jnp.exp(rest)

    head_of_lane = lax.shift_right_logical(lax.broadcasted_iota(jnp.int32, (ck, nh * dk), 1), dk_shift)
    own = [head_of_lane == hd for hd in range(nh)]
    ri = lax.broadcasted_iota(jnp.int32, (nh * ck, nh * ck), 0)
    ci = lax.broadcasted_iota(jnp.int32, (nh * ck, nh * ck), 1)
    causal = jnp.bitwise_and(ri, ck - 1) >= jnp.bitwise_and(ci, ck - 1)
    gn = gn_ref[...]
    st = st_ref[...]

    def stack(x):
        return jnp.concatenate([jnp.where(own[hd], x, 0.0) for hd in range(nh)], axis=0).astype(BF16)

    nc = tt // ck
    q4s, o_intra, kvs = [], [], []
    for c in range(nc):
        r0, r1 = c * ck, (c + 1) * ck
        q4, k4, kd4 = stack(qt[r0:r1]), stack(kt[r0:r1]), stack(kd[r0:r1])
        v4 = jnp.concatenate([v_ref[0, r0:r1, hd * dv:(hd + 1) * dv] for hd in range(nh)], axis=0)
        a = jnp.where(causal, _dot_nt(q4, k4), 0.0)
        q4s.append(q4)
        o_intra.append(_dot(a.astype(BF16), v4))
        kvs.append(_dot_tn(v4, kd4))

    for c in range(nc):
        r0, r1 = c * ck, (c + 1) * ck
        o = o_intra[c] + _dot_nt(q4s[c], st.astype(BF16))
        ms = jnp.mean(o * o, axis=-1, keepdims=True)
        on = o * lax.rsqrt(ms + NORM_EPS) * gn
        for hd in range(nh):
            sg = sg_ref[0, r0:r1, hd * dv:(hd + 1) * dv].astype(F32)
            o_ref[0, r0:r1, hd * dv:(hd + 1) * dv] = (on[hd * ck:(hd + 1) * ck] * sg).astype(o_ref.dtype)
        st = st * jnp.exp(bcum[r1 - 1:r1, :]) + kvs[c]

    st_ref[...] = st

    @pl.when(t == nt - 1)
    def _emit():
        per_group = LANES // dk
        for grp in range(nh // per_group):
            s_grp = st[:, grp * LANES:(grp + 1) * LANES].T
            for hh in range(per_group):
                s_ref[0, grp * per_group + hh] = s_grp[hh * dk:(hh + 1) * dk, :]


def _gla_prompt(qg, kg, vg, lf, sg, gn, tt):
    b, seq, _ = qg.shape
    nt = seq // tt
    kspec = pl.BlockSpec((1, tt, G_KEY_WIDTH), lambda bi, ti: (bi, ti, 0))
    vspec = pl.BlockSpec((1, tt, G_WIDTH), lambda bi, ti: (bi, ti, 0))
    return pl.pallas_call(
        functools.partial(_gla_prompt_kernel, nt=nt),
        grid=(b, nt),
        in_specs=[kspec, kspec, vspec, kspec, vspec, pl.BlockSpec((1, G_VAL_DIM), lambda bi, ti: (0, 0))],
        out_specs=(vspec, pl.BlockSpec((1, G_HEADS, G_KEY_DIM, G_VAL_DIM), lambda bi, ti: (bi, 0, 0, 0))),
        out_shape=(jax.ShapeDtypeStruct((b, seq, G_WIDTH), BF16),
                   jax.ShapeDtypeStruct((b, G_HEADS, G_KEY_DIM, G_VAL_DIM), F32)),
        scratch_shapes=[pltpu.VMEM((G_VAL_DIM, G_KEY_WIDTH), F32)],
        compiler_params=pltpu.CompilerParams(
            dimension_semantics=("arbitrary", "arbitrary"), vmem_limit_bytes=VMEM_LIMIT_BYTES),
        name="gla_prompt",
    )(qg, kg, vg, lf, sg, gn)


def _sample_score_kernel(pt_ref, qt_ref, kt_ref, kc_ref, pn_ref, idx_ref, pown_ref,
                         buf_ref, sem_ref, qb_ref, sc_ref, *, layer, npages, nbuf, bs, nsel):
    b = pl.program_id(0)
    total = bs * npages
    nblk = npages // PAGES_PER_BLOCK
    dh = A_HEAD_DIM

    def page_copy(g, slot):
        bb = g // npages
        return pltpu.make_async_copy(kc_ref.at[layer, pt_ref[bb, g - bb * npages]], buf_ref.at[slot], sem_ref.at[slot])

    @pl.when(b == 0)
    def _prime():
        for s in range(nbuf):
            page_copy(s, s).start()

    qcol = _lane_column(qt_ref[...], b)
    kcol = _lane_column(kt_ref[...], b)
    for h in range(A_HEADS):
        qb_ref[h] = jnp.broadcast_to(qcol[h], (dh, PAGE_SIZE))

    def page_body(p, _):
        g = b * npages + p
        slot = lax.rem(g, nbuf)
        page_copy(g, slot).wait()
        for h in range(A_HEADS):
            sc_ref[p, h:h + 1, :] = jnp.sum(buf_ref[slot, h] * qb_ref[h], axis=0, keepdims=True)

        @pl.when(g + nbuf < total)
        def _next():
            page_copy(g + nbuf, slot).start()

        return 0

    lax.fori_loop(0, npages, page_body, 0)

    lane = lax.broadcasted_iota(jnp.int32, (A_HEADS, LANES), 1).astype(F32)
    sub = lax.broadcasted_iota(jnp.int32, (A_HEADS, LANES), 0)

    gate = jnp.full((A_HEADS, LANES), NEG_INF, F32)
    for j in range(nblk):
        sblk = sc_ref[PAGES_PER_BLOCK * j]
        for pp in range(1, PAGES_PER_BLOCK):
            sblk = sblk + sc_ref[PAGES_PER_BLOCK * j + pp]
        gj = jnp.sum(sblk, axis=1, keepdims=True) * (1.0 / MOBA_BLOCK)
        gate = jnp.where(lane == float(j), gj, gate)

    g = gate
    sel = jnp.zeros((A_HEADS, LANES), F32)
    idx_out = jnp.full((A_HEADS, LANES), -1.0, F32)
    for r in range(nsel):
        m = jnp.max(g, axis=1, keepdims=True)
        idx = jnp.min(jnp.where(g == m, lane, float(LANES)), axis=1, keepdims=True)
        pick = jnp.logical_and(lane == idx, m > NEG_INF)
        sel = jnp.where(pick, 1.0, sel)
        g = jnp.where(pick, NEG_INF, g)
        idx_out = jnp.where(lane == float(r), jnp.where(m > NEG_INF, idx, -1.0), idx_out)
    idx_ref[0] = idx_out.astype(jnp.int32)

    s_own = jnp.zeros((A_HEADS, LANES), F32)
    for h in range(A_HEADS):
        so = jnp.sum(qcol[h] * kcol[h], axis=0, keepdims=True)
        s_own = jnp.where(sub == h, jnp.broadcast_to(so, (A_HEADS, LANES)), s_own)

    masks = [jnp.broadcast_to(sel[:, j:j + 1], (A_HEADS, LANES)) > 0.0 for j in range(nblk)]
    mx = s_own
    for pg in range(npages):
        mx = jnp.maximum(mx, jnp.where(masks[pg // PAGES_PER_BLOCK], sc_ref[pg], NEG_INF))
    m = jnp.max(mx, axis=1, keepdims=True)
    lsum = jnp.zeros((A_HEADS, LANES), F32)
    for pg in range(npages):
        p = jnp.where(masks[pg // PAGES_PER_BLOCK], jnp.exp(sc_ref[pg] - m), 0.0)
        pn_ref[0, pg] = p
        lsum = lsum + p
    p_own = jnp.exp(s_own - m)
    inv = 1.0 / (jnp.sum(lsum, axis=1, keepdims=True) + p_own)
    for pg in range(npages):
        pn_ref[0, pg] = pn_ref[0, pg] * inv
    pown_ref[0] = p_own * inv


def _sample_score(page_table, qt, kt, cache_t, layer, nbuf=SAMPLE_PAGES_IN_FLIGHT):
    bs, npages = page_table.shape
    nblk = npages // PAGES_PER_BLOCK
    nsel = min(MOBA_TOPK, nblk + 1)
    feat = pl.BlockSpec((A_HEADS, A_HEAD_DIM, bs), lambda b, pt: (0, 0, 0))
    grid_spec = pltpu.PrefetchScalarGridSpec(
        num_scalar_prefetch=1,
        grid=(bs,),
        in_specs=[feat, feat, pl.BlockSpec(memory_space=pl.ANY)],
        out_specs=(
            pl.BlockSpec((1, npages, A_HEADS, PAGE_SIZE), lambda b, pt: (b, 0, 0, 0)),
            pl.BlockSpec((1, A_HEADS, LANES), lambda b, pt: (b, 0, 0)),
            pl.BlockSpec((1, A_HEADS, LANES), lambda b, pt: (b, 0, 0)),
        ),
        scratch_shapes=[
            pltpu.VMEM((min(nbuf, npages), A_HEADS, A_HEAD_DIM, PAGE_SIZE), F32),
            pltpu.SemaphoreType.DMA((min(nbuf, npages),)),
            pltpu.VMEM((A_HEADS, A_HEAD_DIM, PAGE_SIZE), F32),
            pltpu.VMEM((npages, A_HEADS, PAGE_SIZE), F32),
        ],
    )
    return pl.pallas_call(
        functools.partial(_sample_score_kernel, layer=layer, npages=npages, nbuf=min(nbuf, npages), bs=bs, nsel=nsel),
        grid_spec=grid_spec,
        out_shape=(
            jax.ShapeDtypeStruct((bs, npages, A_HEADS, PAGE_SIZE), F32),
            jax.ShapeDtypeStruct((bs, A_HEADS, LANES), jnp.int32),
            jax.ShapeDtypeStruct((bs, A_HEADS, LANES), F32),
        ),
        compiler_params=pltpu.CompilerParams(dimension_semantics=("arbitrary",), vmem_limit_bytes=VMEM_LIMIT_BYTES),
        name="sample_score",
    )(page_table, qt, kt, cache_t)


def _sample_pv_kernel(pt_ref, idx_ref, pn_ref, pown_ref, vt_ref, sgt_ref, vc_ref, o_ref,
                      vbuf_ref, sem_ref, *, layer, nsel, bs):
    b = pl.program_id(0)
    slot = lax.rem(b, 2)

    def issue(bb, sl):
        for h in range(A_HEADS):
            for r in range(nsel):
                j = jnp.maximum(idx_ref[bb, h * nsel + r], 0)
                for pp in range(PAGES_PER_BLOCK):
                    pg = pt_ref[bb, PAGES_PER_BLOCK * j + pp]
                    pltpu.make_async_copy(vc_ref.at[layer, pg, h], vbuf_ref.at[sl, h, r, pp], sem_ref.at[sl]).start()

    @pl.when(b == 0)
    def _first():
        issue(0, 0)
        o_ref[...] = jnp.zeros_like(o_ref)

    @pl.when(b + 1 < bs)
    def _prefetch():
        issue(b + 1, 1 - slot)

    for _ in range(A_HEADS * nsel * PAGES_PER_BLOCK):
        pltpu.make_async_copy(vc_ref.at[layer, 0, 0], vbuf_ref.at[slot, 0, 0, 0], sem_ref.at[slot]).wait()

    vcol = _lane_column(vt_ref[...], b)
    sgcol = _lane_column(sgt_ref[...], b)
    mine = lax.broadcasted_iota(jnp.int32, (1, bs), 1) == b
    for h in range(A_HEADS):
        acc = jnp.zeros((A_HEAD_DIM, PAGE_SIZE), F32)
        for r in range(nsel):
            jraw = idx_ref[b, h * nsel + r]
            j = jnp.maximum(jraw, 0)
            w = jnp.where(jraw >= 0, 1.0, 0.0)
            for pp in range(PAGES_PER_BLOCK):
                prow = pn_ref[0, PAGES_PER_BLOCK * j + pp, h:h + 1, :] * w
                acc = acc + vbuf_ref[slot, h, r, pp] * prow
        ocol = jnp.sum(acc, axis=1, keepdims=True) + pown_ref[0, h:h + 1, 0:1] * vcol[h]
        o_ref[h] = jnp.where(mine, jnp.broadcast_to(ocol * sgcol[h], (A_HEAD_DIM, bs)), o_ref[h])


def _sample_pv(page_table, idx, pn, pown, vt, sgt, cache_t, layer):
    bs, npages = page_table.shape
    nsel = idx.shape[1] // A_HEADS
    feat = pl.BlockSpec((A_HEADS, A_HEAD_DIM, bs), lambda b, pt, ix: (0, 0, 0))
    grid_spec = pltpu.PrefetchScalarGridSpec(
        num_scalar_prefetch=2,
        grid=(bs,),
        in_specs=[
            pl.BlockSpec((1, npages, A_HEADS, PAGE_SIZE), lambda b, pt, ix: (b, 0, 0, 0)),
            pl.BlockSpec((1, A_HEADS, LANES), lambda b, pt, ix: (b, 0, 0)),
            feat, feat,
            pl.BlockSpec(memory_space=pl.ANY),
        ],
        out_specs=feat,
        scratch_shapes=[
            pltpu.VMEM((2, A_HEADS, nsel, PAGES_PER_BLOCK, A_HEAD_DIM, PAGE_SIZE), F32),
            pltpu.SemaphoreType.DMA((2,)),
        ],
    )
    return pl.pallas_call(
        functools.partial(_sample_pv_kernel, layer=layer, nsel=nsel, bs=bs),
        grid_spec=grid_spec,
        out_shape=jax.ShapeDtypeStruct((A_HEADS, A_HEAD_DIM, bs), F32),
        compiler_params=pltpu.CompilerParams(dimension_semantics=("arbitrary",), vmem_limit_bytes=VMEM_LIMIT_BYTES),
        name="sample_pv",
    )(page_table, idx, pn, pown, vt, sgt, cache_t)


def _gla_sample_kernel(qt_ref, kt_ref, gt_ref, v_ref, sg_ref, s0_ref, gn_ref, s_ref, o_ref):
    b = pl.program_id(0)
    dv = G_VAL_DIM
    bs = v_ref.shape[0]

    @pl.when(b == 0)
    def _init():
        o_ref[...] = jnp.zeros_like(o_ref)

    qcol = _lane_column(qt_ref[...], b)
    kcol = _lane_column(kt_ref[...], b)
    gcol = _lane_column(gt_ref[...], b)
    mine = lax.broadcasted_iota(jnp.int32, (bs, dv), 0) == b
    for h in range(G_HEADS):
        cols = slice(h * dv, (h + 1) * dv)
        v = jnp.sum(jnp.where(mine, v_ref[:, cols], 0.0), axis=0, keepdims=True)
        sg = jnp.sum(jnp.where(mine, sg_ref[:, cols], 0.0), axis=0, keepdims=True)
        s_new = jnp.exp(gcol[h]) * s0_ref[0, h] + kcol[h] * v
        s_ref[0, h] = s_new
        o = jnp.sum(qcol[h] * s_new, axis=0, keepdims=True) * (G_KEY_DIM ** -0.5)
        ms = jnp.mean(o * o, axis=-1, keepdims=True)
        on = o * lax.rsqrt(ms + NORM_EPS) * gn_ref[...] * sg
        o_ref[:, cols] = jnp.where(mine, jnp.broadcast_to(on, (bs, dv)), o_ref[:, cols])


def _gla_sample(qt, kt, gt, v, sg, s0, gn):
    bs = s0.shape[0]
    feat = pl.BlockSpec((G_HEADS, G_KEY_DIM, bs), lambda b: (0, 0, 0))
    tok = pl.BlockSpec((bs, G_WIDTH), lambda b: (0, 0))
    sspec = pl.BlockSpec((1, G_HEADS, G_KEY_DIM, G_VAL_DIM), lambda b: (b, 0, 0, 0))
    return pl.pallas_call(
        _gla_sample_kernel,
        grid=(bs,),
        in_specs=[feat, feat, feat, tok, tok, sspec, pl.BlockSpec((1, G_VAL_DIM), lambda b: (0, 0))],
        out_specs=(sspec, tok),
        out_shape=(jax.ShapeDtypeStruct(s0.shape, F32), jax.ShapeDtypeStruct((bs, G_WIDTH), F32)),
        compiler_params=pltpu.CompilerParams(dimension_semantics=("arbitrary",)),
        name="gla_sample",
    )(qt, kt, gt, v, sg, s0, gn)


def _rope_tables(pos):
    inv = jnp.power(jnp.float32(ROPE_THETA), -jnp.arange(ROT_HALF, dtype=F32) * (2.0 / ROT_DIM))
    ang = pos.astype(F32)[:, None] * inv[None, :]
    return jnp.cos(ang).T, jnp.sin(ang).T


def _pick_tile(n, pref):
    t = min(n, pref)
    while n % t:
        t //= 2
    return t


def kernel(x_prompt, x_sample, cache_k, cache_v, state_gla, page_table, norm_g, w_in, w_a2, b_a2, gla_norm_g, w_out, final_norm_g):
    bp, lp, d = x_prompt.shape
    bs, ls, _ = x_sample.shape
    depth = norm_g.shape[0]
    past_len = page_table.shape[1] * PAGE_SIZE
    assert ls == 1 and lp % MOBA_BLOCK == 0 and past_len % MOBA_BLOCK == 0
    assert cache_k.shape[2:] == (PAGE_SIZE, A_HEADS, A_HEAD_DIM)

    tabs_p = _rope_tables(jnp.arange(lp, dtype=jnp.int32))
    tabs_s = _rope_tables(jnp.full((bs,), past_len, dtype=jnp.int32))
    kc_t = jnp.transpose(cache_k, (0, 1, 3, 4, 2))
    vc_t = jnp.transpose(cache_v, (0, 1, 3, 4, 2))

    tm_p = _pick_tile(lp, 512)
    tt = _pick_tile(lp, 256)
    hp = x_prompt.reshape(bp * lp, d)
    hs = x_sample.reshape(bs, d)
    fg = final_norm_g.reshape(1, d)
    sp_l, ks_l, vs_l, ss_l = [], [], [], []
    kv_prompt = (jnp.zeros((depth, bp, A_WIDTH, lp), F32), jnp.zeros((depth, bp, A_WIDTH, lp), F32))
    rows = A_WIDTH
    g_lo = 4 * rows
    g_hi = g_lo + 2 * G_KEY_WIDTH + 2 * G_WIDTH
    for l in range(depth):
        wt = jnp.transpose(w_in[l])
        w_t = wt[0:4 * rows].astype(BF16)
        a1_t = jnp.pad(wt[g_hi:], ((0, LANES - G_GATE_RANK), (0, 0)))
        w_row = jnp.transpose(jnp.concatenate([wt[g_lo:g_hi], a1_t], axis=0)).astype(BF16)
        wa2 = jnp.pad(w_a2[l], ((0, LANES - G_GATE_RANK), (0, 0))).astype(BF16)
        ba2 = b_a2[l].reshape(1, G_KEY_WIDTH)
        ng = norm_g[l].reshape(1, d)
        gn = gla_norm_g[l].reshape(1, G_VAL_DIM)
        wo = w_out[l].astype(BF16)
        last = l == depth - 1

        qt, kt_all, vt_all, sgt, qg, kg, vg, sgg, lf = _proj_in(
            hp, ng, w_row, w_t, wa2, ba2, tabs_p, bp, tm_p, l, depth, kv_prompt)
        kv_prompt = (kt_all, vt_all)
        oa = _moba_prompt(qt, kt_all, vt_all, sgt, l)
        og, s_fin = _gla_prompt(qg.reshape(bp, lp, -1), kg.reshape(bp, lp, -1), vg.reshape(bp, lp, -1),
                                lf.reshape(bp, lp, -1), sgg.reshape(bp, lp, -1), gn, tt)
        hp = _proj_out(oa.reshape(bp * lp, A_WIDTH), og.reshape(bp * lp, G_WIDTH), hp, wo, fg, tm_p, last)
        sp_l.append(s_fin)

        qt, kt_s, vt_s, sgt, qg, kg, vg, sgg, lf = _proj_in(hs, ng, w_row, w_t, wa2, ba2, tabs_s, 1, bs, 0, 1)
        a_feat = lambda a: a.reshape(A_HEADS, A_HEAD_DIM, bs)
        pn, idx, pown = _sample_score(page_table, a_feat(qt), a_feat(kt_s), kc_t, l)
        nsel = min(MOBA_TOPK, page_table.shape[1] // PAGES_PER_BLOCK + 1)
        idx_s = idx[:, :, :nsel].reshape(bs, A_HEADS * nsel)
        oa_s = _sample_pv(page_table, idx_s, pn, pown, a_feat(vt_s), a_feat(sgt), vc_t, l)
        g_feat = lambda a: jnp.transpose(a.astype(F32)).reshape(G_HEADS, G_KEY_DIM, bs)
        s_new, og_s = _gla_sample(g_feat(qg), g_feat(kg), g_feat(lf), vg.astype(F32), sgg.astype(F32),
                                  state_gla[l], gn)
        hs = _proj_out(oa_s.reshape(A_WIDTH, bs), og_s, hs, wo, fg, bs, last, a_feature_major=True)
        ks_l.append(jnp.transpose(kt_s[0, 0]).reshape(bs, 1, A_HEADS, A_HEAD_DIM))
        vs_l.append(jnp.transpose(vt_s[0, 0]).reshape(bs, 1, A_HEADS, A_HEAD_DIM))
        ss_l.append(s_new)

    kt_all, vt_all = kv_prompt
    to_cache_layout = lambda a: jnp.transpose(a.reshape(depth, bp, A_HEADS, A_HEAD_DIM, lp), (0, 1, 4, 2, 3))
    return (hp.reshape(bp, lp, d), hs.reshape(bs, ls, d), to_cache_layout(kt_all), to_cache_layout(vt_all),
            jnp.stack(sp_l), jnp.stack(ks_l), jnp.stack(vs_l), jnp.stack(ss_l))
```

```python
import functools

import jax
import jax.numpy as jnp
from jax import lax
from jax.experimental import pallas as pl
from jax.experimental.pallas import tpu as pltpu

A_HEADS = 8
A_HEAD_DIM = 64
A_WIDTH = A_HEADS * A_HEAD_DIM
ROT_DIM = A_HEAD_DIM // 4
ROT_HALF = ROT_DIM // 2
ROPE_THETA = 500000.0
MOBA_BLOCK = 256
MOBA_TOPK = 3
G_HEADS = 4
G_KEY_DIM = 64
G_VAL_DIM = 128
G_KEY_WIDTH = G_HEADS * G_KEY_DIM
G_WIDTH = G_HEADS * G_VAL_DIM
G_GATE_RANK = 16
G_GATE_NORM = 16.0
G_CHUNK = 32
NORM_EPS = 1e-6
PAGE_SIZE = 128
PAGES_PER_BLOCK = MOBA_BLOCK // PAGE_SIZE

LANES = 128
VMEM_LIMIT_BYTES = 56 * 1024 * 1024

F32 = jnp.float32
BF16 = jnp.bfloat16
NEG_INF = float("-inf")
MASK_BIAS = -1e30
MOBA_HEADS_PER_STEP = 4
MOBA_BLOCKS_PER_ITERATION = 4
SAMPLE_PAGES_IN_FLIGHT = 24
V_PAD_ROWS = 16

_NT = (((1,), (1,)), ((), ()))
_TN = (((0,), (0,)), ((), ()))


def _dot(a, b):
    return jnp.dot(a, b, preferred_element_type=F32)


def _dot_nt(a, b):
    return lax.dot_general(a, b, _NT, preferred_element_type=F32)


def _dot_tn(a, b):
    return lax.dot_general(a, b, _TN, preferred_element_type=F32)


def _split3(x):
    x1 = x.astype(BF16)
    r1 = x - x1.astype(F32)
    x2 = r1.astype(BF16)
    x3 = (r1 - x2.astype(F32)).astype(BF16)
    return x1, x2, x3


def _lane_column(x, lane):
    pick = lax.broadcasted_iota(jnp.int32, (1,) * (x.ndim - 1) + (x.shape[-1],), x.ndim - 1) == lane
    return jnp.sum(jnp.where(pick, x.astype(F32), 0.0), axis=-1, keepdims=True)


def _proj_in_kernel(*refs, n_alias):
    x_ref, g_ref, w_ref, wt_ref, wa2_ref, ba2_ref, ct_ref, st_ref = refs[:8]
    qt_ref, kt_ref, vt_ref, sgt_ref, qg_ref, kg_ref, vg_ref, sgg_ref, lf_ref = refs[8 + n_alias:]
    x = x_ref[...]
    ms = jnp.mean(x * x, axis=-1, keepdims=True)
    h = (x * lax.rsqrt(ms + NORM_EPS) * g_ref[...]).astype(BF16)

    def proj(lo, hi):
        return _dot(h, w_ref[:, lo:hi])

    o = 0
    qg_ref[...] = proj(o, o + G_KEY_WIDTH).astype(qg_ref.dtype)
    o += G_KEY_WIDTH
    kg_ref[...] = proj(o, o + G_KEY_WIDTH).astype(kg_ref.dtype)
    o += G_KEY_WIDTH
    vg_ref[...] = proj(o, o + G_WIDTH).astype(vg_ref.dtype)
    o += G_WIDTH
    zg = proj(o, o + G_WIDTH)
    sgg_ref[...] = (zg * jax.nn.sigmoid(zg)).astype(sgg_ref.dtype)
    o += G_WIDTH
    a1 = proj(o, o + LANES)
    la = _dot(a1.astype(BF16), wa2_ref[...]) + ba2_ref[...]
    lf_ref[...] = -(jnp.maximum(-la, 0.0) + jnp.log1p(jnp.exp(-jnp.abs(la)))) * (1.0 / G_GATE_NORM)

    zt = _dot_nt(wt_ref[...], h)
    ct, st = ct_ref[...], st_ref[...]
    q_scale = A_HEAD_DIM ** -0.5
    for hd in range(A_HEADS):
        b0 = hd * A_HEAD_DIM
        x1 = zt[b0:b0 + ROT_HALF]
        x2 = zt[b0 + ROT_HALF:b0 + ROT_DIM]
        qh = jnp.concatenate([x1 * ct - x2 * st, x2 * ct + x1 * st, zt[b0 + ROT_DIM:b0 + A_HEAD_DIM]], axis=0)
        qt_ref[b0:b0 + A_HEAD_DIM, :] = (qh * q_scale).astype(qt_ref.dtype)
        k0 = b0 + A_WIDTH
        x1 = zt[k0:k0 + ROT_HALF]
        x2 = zt[k0 + ROT_HALF:k0 + ROT_DIM]
        kt_ref[b0:b0 + ROT_HALF, :] = x1 * ct - x2 * st
        kt_ref[b0 + ROT_HALF:b0 + ROT_DIM, :] = x2 * ct + x1 * st
        kt_ref[b0 + ROT_DIM:b0 + A_HEAD_DIM, :] = zt[k0 + ROT_DIM:k0 + A_HEAD_DIM]
    vt_ref[...] = zt[2 * A_WIDTH:3 * A_WIDTH]
    zg = zt[3 * A_WIDTH:4 * A_WIDTH]
    sgt_ref[...] = (zg * jax.nn.sigmoid(zg)).astype(sgt_ref.dtype)


def _proj_in(x, norm_g, w_row, w_t, wa2, ba2, tabs, n_batch, tm, layer, depth, kv_prev=None):
    n, d = x.shape
    seq = n // n_batch
    nl = seq // tm
    ct_tab, st_tab = tabs
    row = lambda i: (i, 0)
    const = lambda i: (0, 0)
    feat_shape = jax.ShapeDtypeStruct((n_batch, A_WIDTH, seq), BF16)
    kv_shape = jax.ShapeDtypeStruct((depth, n_batch, A_WIDTH, seq), F32)
    out_shapes = (
        feat_shape,
        kv_shape, kv_shape,
        feat_shape,
        jax.ShapeDtypeStruct((n, G_KEY_WIDTH), BF16),
        jax.ShapeDtypeStruct((n, G_KEY_WIDTH), BF16),
        jax.ShapeDtypeStruct((n, G_WIDTH), BF16),
        jax.ShapeDtypeStruct((n, G_WIDTH), BF16),
        jax.ShapeDtypeStruct((n, G_KEY_WIDTH), F32),
    )
    feat_spec = pl.BlockSpec((None, A_WIDTH, tm), lambda i: (i // nl, 0, i % nl))
    kv_spec = pl.BlockSpec((None, None, A_WIDTH, tm), lambda i: (layer, i // nl, 0, i % nl))
    in_specs = [
        pl.BlockSpec((tm, d), row),
        pl.BlockSpec((1, d), const),
        pl.BlockSpec(w_row.shape, const),
        pl.BlockSpec(w_t.shape, const),
        pl.BlockSpec((LANES, G_KEY_WIDTH), const),
        pl.BlockSpec((1, G_KEY_WIDTH), const),
        pl.BlockSpec((ROT_HALF, tm), lambda i: (0, i % nl)),
        pl.BlockSpec((ROT_HALF, tm), lambda i: (0, i % nl)),
    ]
    args = [x, norm_g, w_row, w_t, wa2, ba2, ct_tab, st_tab]
    aliases = {}
    if kv_prev is not None:
        in_specs += [pl.BlockSpec(memory_space=pl.ANY)] * 2
        aliases = {len(args): 1, len(args) + 1: 2}
        args += list(kv_prev)
    return pl.pallas_call(
        functools.partial(_proj_in_kernel, n_alias=len(aliases)),
        grid=(n // tm,),
        in_specs=in_specs,
        out_specs=(
            feat_spec, kv_spec, kv_spec, feat_spec,
            pl.BlockSpec((tm, G_KEY_WIDTH), row), pl.BlockSpec((tm, G_KEY_WIDTH), row),
            pl.BlockSpec((tm, G_WIDTH), row), pl.BlockSpec((tm, G_WIDTH), row),
            pl.BlockSpec((tm, G_KEY_WIDTH), row),
        ),
        out_shape=out_shapes,
        input_output_aliases=aliases,
        compiler_params=pltpu.CompilerParams(dimension_semantics=("arbitrary",), vmem_limit_bytes=VMEM_LIMIT_BYTES),
        name="proj_in",
    )(*args)


def _proj_out_kernel(a_ref, g_ref, x_ref, w_ref, fg_ref, o_ref, *, final, a_feature_major):
    half = g_ref.shape[1]
    a = a_ref[...].astype(BF16)
    ya = _dot_tn(a, w_ref[0:half, :]) if a_feature_major else _dot(a, w_ref[0:half, :])
    xo = x_ref[...] + (ya + _dot(g_ref[...].astype(BF16), w_ref[half:, :]))
    if final:
        ms = jnp.mean(xo * xo, axis=-1, keepdims=True)
        xo = xo * lax.rsqrt(ms + NORM_EPS) * fg_ref[...]
    o_ref[...] = xo


def _proj_out(a, g, x, w, fg, tm, final, a_feature_major=False):
    n, d = x.shape
    row = lambda i: (i, 0)
    const = lambda i: (0, 0)
    a_spec = pl.BlockSpec((a.shape[0], tm), lambda i: (0, i)) if a_feature_major else pl.BlockSpec((tm, a.shape[1]), row)
    return pl.pallas_call(
        functools.partial(_proj_out_kernel, final=final, a_feature_major=a_feature_major),
        grid=(n // tm,),
        in_specs=[
            a_spec,
            pl.BlockSpec((tm, g.shape[1]), row),
            pl.BlockSpec((tm, d), row),
            pl.BlockSpec(w.shape, const),
            pl.BlockSpec((1, d), const),
        ],
        out_specs=pl.BlockSpec((tm, d), row),
        out_shape=jax.ShapeDtypeStruct((n, d), F32),
        compiler_params=pltpu.CompilerParams(dimension_semantics=("arbitrary",), vmem_limit_bytes=VMEM_LIMIT_BYTES),
        name="proj_out",
    )(a, g, x, w, fg)


def _moba_prompt_kernel(*refs, nb, nsel, unroll, side):
    if side is None:
        qt_ref, kt_ref, vt_ref, sgt_ref, o_ref, ka_ref, va_ref, km_ref, qa_ref, s_ref, acc_ref, m_ref = refs
    else:
        (pt_ref, qt_ref, kt_ref, vt_ref, sgt_ref, qs_ref, ks_ref, kc_ref, o_ref, pn_ref, idx_ref, pown_ref,
         ka_ref, va_ref, km_ref, qa_ref, s_ref, acc_ref, m_ref, buf_ref, sem_ref, qb_ref, sc_ref) = refs
    qi = pl.program_id(2)
    blk = MOBA_BLOCK
    dh = A_HEAD_DIM
    heads = qt_ref.shape[0] // dh
    per_group = LANES // dh

    if side is not None:
        layer, npages, bs, pps, nsel_s = side
        sps = npages // pps
        step = (pl.program_id(0) * pl.num_programs(1) + pl.program_id(1)) * pl.num_programs(2) + qi
        seq = step // sps
        part = step - seq * sps
        total = bs * npages
        half = lax.rem(step, 2) * pps

        def page_copy(g, slot):
            bb = g // npages
            return pltpu.make_async_copy(
                kc_ref.at[layer, pt_ref[bb, g - bb * npages]], buf_ref.at[slot], sem_ref.at[slot])

        @pl.when(step == 0)
        def _prime():
            for i in range(2 * pps):
                page_copy(i, i).start()

        @pl.when(seq < bs)
        def _score_pages():
            @pl.when(part == 0)
            def _new_sequence():
                qcol = _lane_column(qs_ref[...], seq)
                for h in range(A_HEADS):
                    qb_ref[h] = jnp.broadcast_to(qcol[h], (dh, PAGE_SIZE))

            g0 = step * pps
            for i in range(pps):
                page_copy(g0 + i, half + i).wait()
            for h in range(A_HEADS):
                qb = qb_ref[h]
                for i in range(pps):
                    sc_ref[part * pps + i, h:h + 1, :] = jnp.sum(buf_ref[half + i, h] * qb, axis=0, keepdims=True)
            for i in range(pps):
                @pl.when(g0 + i + 2 * pps < total)
                def _refill():
                    page_copy(g0 + i + 2 * pps, half + i).start()

            @pl.when(part == sps - 1)
            def _finish_sequence():
                _sample_choose_and_weigh(sc_ref, _lane_column(qs_ref[...], seq), _lane_column(ks_ref[...], seq),
                                         pn_ref, idx_ref, pown_ref, npages, nsel_s)

    @pl.when(qi == 0)
    def _prepare():
        lane = lax.broadcasted_iota(jnp.int32, (blk, LANES), 1)
        ones_rows = jnp.where(lax.broadcasted_iota(jnp.int32, (V_PAD_ROWS, blk), 0) == 0, 1.0, 0.0)
        for j in range(nb):
            kj = kt_ref[:, j * blk:(j + 1) * blk].T
            km_ref[j:j + 1, :] = jnp.mean(kj, axis=0, keepdims=True)
            onehot = jnp.where(lane == dh + j, 1.0, 0.0)
            for hh in range(heads):
                grp, sub = divmod(hh, per_group)
                kh = kj[:, grp * LANES:(grp + 1) * LANES]
                if sub:
                    kh = pltpu.roll(kh, LANES - sub * dh, 1)
                ka_ref[hh, j * blk:(j + 1) * blk, :] = jnp.where(lane < dh, kh, onehot).astype(BF16)
                vh = vt_ref[hh * dh:(hh + 1) * dh, j * blk:(j + 1) * blk]
                va_ref[j, hh] = jnp.concatenate([vh, ones_rows], axis=0).astype(BF16)
        m_ref[...] = jnp.zeros_like(m_ref)

    @pl.when(jnp.logical_and(qi == 0, jnp.logical_and(pl.program_id(0) == 0, pl.program_id(1) == 0)))
    def _define_scores():
        s_ref[...] = jnp.zeros_like(s_ref)

    qb = jnp.minimum(qi, nb - 1)
    prev = jnp.maximum(qi - 1, 0)
    m_old = [m_ref[hh] for hh in range(heads)]

    for hh in range(heads):
        acc_ref[hh] = _dot(va_ref[prev, hh], jnp.exp(s_ref[hh, nb] - m_old[hh]).astype(BF16))

    rowf = lax.broadcasted_iota(jnp.int32, (nb, blk), 0).astype(F32)
    past = rowf < qb.astype(F32)
    causal = lax.broadcasted_iota(jnp.int32, (blk, blk), 0) <= lax.broadcasted_iota(jnp.int32, (blk, blk), 1)
    pad = jnp.zeros((LANES - dh - nb, blk), BF16)
    r_own = pl.multiple_of(qb * blk, blk)

    m_run, qd = [], []
    for hh in range(heads):
        qh = qt_ref[hh * dh:(hh + 1) * dh, :]
        km = km_ref[:, hh * dh:(hh + 1) * dh]
        km1 = km.astype(BF16)
        km2 = (km - km1.astype(F32)).astype(BF16)
        g = jnp.where(past, _dot(km1, qh) + _dot(km2, qh), NEG_INF)
        sel = jnp.zeros((nb, blk), F32)
        for _ in range(nsel):
            m = jnp.max(g, axis=0, keepdims=True)
            idx = jnp.min(jnp.where(g == m, rowf, float(nb)), axis=0, keepdims=True)
            pick = jnp.logical_and(rowf == idx, m > NEG_INF)
            sel = jnp.where(pick, 1.0, sel)
            g = jnp.where(pick, NEG_INF, g)
        bias = jnp.where(sel > 0.0, 0.0, MASK_BIAS).astype(BF16)
        qa_ref[hh] = jnp.concatenate([qh, bias, pad], axis=0)
        qd.append(jnp.concatenate([qh, jnp.zeros((LANES - dh, blk), BF16)], axis=0))

    for hh in range(heads):
        s = jnp.where(causal, _dot(ka_ref[hh, pl.ds(r_own, blk), :], qd[hh]), MASK_BIAS)
        s_ref[hh, nb] = s
        m_run.append(jnp.max(s, axis=0, keepdims=True))

    def blocks(j0, count, ms):
        ms = list(ms)
        for hh in range(heads):
            part = None
            for u in range(count):
                j = j0 + u
                p = jnp.where(j < prev, jnp.exp(s_ref[hh, j] - m_old[hh]), 0.0).astype(BF16)
                t = _dot(va_ref[j, hh], p)
                part = t if part is None else part + t
                r0 = pl.multiple_of(j * blk, blk)
                s = _dot(ka_ref[hh, pl.ds(r0, blk), :], qa_ref[hh])
                s_ref[hh, j] = s
                ms[hh] = jnp.maximum(ms[hh], jnp.max(s, axis=0, keepdims=True))
            acc_ref[hh] += part
        return tuple(ms)

    m_fin = tuple(m_run)
    done = 0
    width = unroll
    while width >= 1:
        trips = lax.shift_right_logical(qi - done, width.bit_length() - 1)
        m_fin = lax.fori_loop(0, trips, functools.partial(
            lambda gi, ms, j0, w: blocks(j0 + gi * w, w, ms), j0=done, w=width), m_fin)
        done = done + trips * width
        width //= 2
    for hh in range(heads):
        m_ref[hh] = m_fin[hh]

    ot = jnp.concatenate([acc_ref[hh, 0:dh, :] / acc_ref[hh, dh:dh + 1, :] for hh in range(heads)], axis=0)
    o_ref[...] = (ot * sgt_ref[...].astype(F32)).T.astype(o_ref.dtype)


def _score_pages_per_step(npages, bs, steps):
    for pps in range(1, npages + 1):
        if npages % pps == 0 and bs * (npages // pps) <= steps:
            return pps
    return None


def _moba_prompt(qt, kt_all, vt_all, sgt, layer, decode=None):
    b, width, seq = qt.shape
    nb = seq // MOBA_BLOCK
    nsel = min(MOBA_TOPK, nb)
    heads = MOBA_HEADS_PER_STEP
    hw = heads * A_HEAD_DIM
    assert A_HEAD_DIM + nb <= LANES and width % hw == 0
    unroll = MOBA_BLOCKS_PER_ITERATION
    n_hp = width // hw
    grid = (b, n_hp, nb + 1)
    scratch = [
        pltpu.VMEM((heads, seq, LANES), BF16),
        pltpu.VMEM((nb, heads, A_HEAD_DIM + V_PAD_ROWS, MOBA_BLOCK), BF16),
        pltpu.VMEM((nb, hw), F32),
        pltpu.VMEM((heads, LANES, MOBA_BLOCK), BF16),
        pltpu.VMEM((heads, nb + 1, MOBA_BLOCK, MOBA_BLOCK), F32),
        pltpu.VMEM((heads, A_HEAD_DIM + V_PAD_ROWS, MOBA_BLOCK), F32),
        pltpu.VMEM((heads, 1, MOBA_BLOCK), F32),
    ]
    params = pltpu.CompilerParams(
        dimension_semantics=("arbitrary", "arbitrary", "arbitrary"), vmem_limit_bytes=VMEM_LIMIT_BYTES)
    q_map = lambda bi, hp, qi, *_: (bi, hp, jnp.minimum(qi, nb - 1))
    g_map = lambda bi, hp, qi, *_: (bi, hp, jnp.maximum(qi - 1, 0))
    kv_map = lambda bi, hp, qi, *_: (layer, bi, hp, 0)
    o_map = lambda bi, hp, qi, *_: (bi, jnp.maximum(qi - 1, 0), hp)
    in_specs = [pl.BlockSpec((None, hw, MOBA_BLOCK), q_map), pl.BlockSpec((None, None, hw, seq), kv_map),
                pl.BlockSpec((None, None, hw, seq), kv_map), pl.BlockSpec((None, hw, MOBA_BLOCK), g_map)]
    o_spec = pl.BlockSpec((None, MOBA_BLOCK, hw), o_map)
    o_shape = jax.ShapeDtypeStruct((b, seq, width), BF16)
    if decode is None:
        return pl.pallas_call(
            functools.partial(_moba_prompt_kernel, nb=nb, nsel=nsel, unroll=unroll, side=None),
            grid=grid, in_specs=in_specs, out_specs=o_spec, out_shape=o_shape, scratch_shapes=scratch,
            compiler_params=params, name="moba_prompt",
        )(qt, kt_all, vt_all, sgt)

    page_table, q_s, k_s, cache_t, pps = decode
    bs, npages = page_table.shape
    sps = npages // pps
    nsel_s = min(MOBA_TOPK, npages // PAGES_PER_BLOCK + 1)
    seq_of = lambda bi, hp, qi: jnp.minimum(((bi * n_hp + hp) * (nb + 1) + qi) // sps, bs - 1)
    feat = pl.BlockSpec((A_HEADS, A_HEAD_DIM, bs), lambda bi, hp, qi, pt: (0, 0, 0))
    grid_spec = pltpu.PrefetchScalarGridSpec(
        num_scalar_prefetch=1,
        grid=grid,
        in_specs=in_specs + [feat, feat, pl.BlockSpec(memory_space=pl.ANY)],
        out_specs=(
            o_spec,
            pl.BlockSpec((1, npages, A_HEADS, PAGE_SIZE), lambda bi, hp, qi, pt: (seq_of(bi, hp, qi), 0, 0, 0)),
            pl.BlockSpec((1, A_HEADS, LANES), lambda bi, hp, qi, pt: (seq_of(bi, hp, qi), 0, 0)),
            pl.BlockSpec((1, A_HEADS, LANES), lambda bi, hp, qi, pt: (seq_of(bi, hp, qi), 0, 0)),
        ),
        scratch_shapes=scratch + [
            pltpu.VMEM((2 * pps, A_HEADS, A_HEAD_DIM, PAGE_SIZE), F32),
            pltpu.SemaphoreType.DMA((2 * pps,)),
            pltpu.VMEM((A_HEADS, A_HEAD_DIM, PAGE_SIZE), F32),
            pltpu.VMEM((npages, A_HEADS, PAGE_SIZE), F32),
        ],
    )
    return pl.pallas_call(
        functools.partial(_moba_prompt_kernel, nb=nb, nsel=nsel, unroll=unroll,
                          side=(layer, npages, bs, pps, nsel_s)),
        grid_spec=grid_spec,
        out_shape=(
            o_shape,
            jax.ShapeDtypeStruct((bs, npages, A_HEADS, PAGE_SIZE), F32),
            jax.ShapeDtypeStruct((bs, A_HEADS, LANES), jnp.int32),
            jax.ShapeDtypeStruct((bs, A_HEADS, LANES), F32),
        ),
        compiler_params=params,
        name="moba_prompt",
    )(page_table, qt, kt_all, vt_all, sgt, q_s, k_s, cache_t)


def _gla_prompt_kernel(q_ref, k_ref, v_ref, lf_ref, sg_ref, gn_ref, o_ref, s_ref, st_ref, *, nt):
    t = pl.program_id(1)
    tt = q_ref.shape[1]
    ck = G_CHUNK
    dk, dv = G_KEY_DIM, G_VAL_DIM
    nh = G_HEADS

    @pl.when(t == 0)
    def _init():
        st_ref[...] = jnp.zeros_like(st_ref)

    g1, g2, g3 = _split3(lf_ref[0])
    row = lax.broadcasted_iota(jnp.int32, (tt, tt), 0)
    col = lax.broadcasted_iota(jnp.int32, (tt, tt), 1)
    ck_shift = ck.bit_length() - 1
    dk_shift = dk.bit_length() - 1
    same = lax.shift_right_logical(row, ck_shift) == lax.shift_right_logical(col, ck_shift)
    lower = jnp.where(jnp.logical_and(same, col <= row), 1.0, 0.0).astype(BF16)
    upper = jnp.where(jnp.logical_and(same, col > row), 1.0, 0.0).astype(BF16)
    bcum = _dot(lower, g1) + _dot(lower, g2) + _dot(lower, g3)
    rest = _dot(upper, g1) + _dot(upper, g2) + _dot(upper, g3)

    qf = q_ref[0].astype(F32)
    kf = k_ref[0].astype(F32)
    qt = qf * jnp.exp(bcum) * (dk ** -0.5)
    kt = kf * jnp.exp(-bcum)
    kd = kf * jnp.exp(rest)

    head_of_lane = lax.shift_right_logical(lax.broadcasted_iota(jnp.int32, (ck, nh * dk), 1), dk_shift)
    own = [head_of_lane == hd for hd in range(nh)]
    ri = lax.broadcasted_iota(jnp.int32, (nh * ck, nh * ck), 0)
    ci = lax.broadcasted_iota(jnp.int32, (nh * ck, nh * ck), 1)
    causal = jnp.bitwise_and(ri, ck - 1) >= jnp.bitwise_and(ci, ck - 1)
    gn = gn_ref[...]
    st = st_ref[...]

    def stack(x):
        return jnp.concatenate([jnp.where(own[hd], x, 0.0) for hd in range(nh)], axis=0).astype(BF16)

    nc = tt // ck
    q4s, o_intra, kvs = [], [], []
    for c in range(nc):
        r0, r1 = c * ck, (c + 1) * ck
        q4, k4, kd4 = stack(qt[r0:r1]), stack(kt[r0:r1]), stack(kd[r0:r1])
        v4 = jnp.concatenate([v_ref[0, r0:r1, hd * dv:(hd + 1) * dv] for hd in range(nh)], axis=0)
        a = jnp.where(causal, _dot_nt(q4, k4), 0.0)
        q4s.append(q4)
        o_intra.append(_dot(a.astype(BF16), v4))
        kvs.append(_dot_tn(v4, kd4))

    for c in range(nc):
        r0, r1 = c * ck, (c + 1) * ck
        o = o_intra[c] + _dot_nt(q4s[c], st.astype(BF16))
        ms = jnp.mean(o * o, axis=-1, keepdims=True)
        on = o * lax.rsqrt(ms + NORM_EPS) * gn
        for hd in range(nh):
            sg = sg_ref[0, r0:r1, hd * dv:(hd + 1) * dv].astype(F32)
            o_ref[0, r0:r1, hd * dv:(hd + 1) * dv] = (on[hd * ck:(hd + 1) * ck] * sg).astype(o_ref.dtype)
        st = st * jnp.exp(bcum[r1 - 1:r1, :]) + kvs[c]

    st_ref[...] = st

    @pl.when(t == nt - 1)
    def _emit():
        per_group = LANES // dk
        for grp in range(nh // per_group):
            s_grp = st[:, grp * LANES:(grp + 1) * LANES].T
            for hh in range(per_group):
                s_ref[0, grp * per_group + hh] = s_grp[hh * dk:(hh + 1) * dk, :]


def _gla_prompt(qg, kg, vg, lf, sg, gn, tt):
    b, seq, _ = qg.shape
    nt = seq // tt
    kspec = pl.BlockSpec((1, tt, G_KEY_WIDTH), lambda bi, ti: (bi, ti, 0))
    vspec = pl.BlockSpec((1, tt, G_WIDTH), lambda bi, ti: (bi, ti, 0))
    return pl.pallas_call(
        functools.partial(_gla_prompt_kernel, nt=nt),
        grid=(b, nt),
        in_specs=[kspec, kspec, vspec, kspec, vspec, pl.BlockSpec((1, G_VAL_DIM), lambda bi, ti: (0, 0))],
        out_specs=(vspec, pl.BlockSpec((1, G_HEADS, G_KEY_DIM, G_VAL_DIM), lambda bi, ti: (bi, 0, 0, 0))),
        out_shape=(jax.ShapeDtypeStruct((b, seq, G_WIDTH), BF16),
                   jax.ShapeDtypeStruct((b, G_HEADS, G_KEY_DIM, G_VAL_DIM), F32)),
        scratch_shapes=[pltpu.VMEM((G_VAL_DIM, G_KEY_WIDTH), F32)],
        compiler_params=pltpu.CompilerParams(
            dimension_semantics=("arbitrary", "arbitrary"), vmem_limit_bytes=VMEM_LIMIT_BYTES),
        name="gla_prompt",
    )(qg, kg, vg, lf, sg, gn)


def _sample_score_kernel(pt_ref, qt_ref, kt_ref, kc_ref, pn_ref, idx_ref, pown_ref,
                         buf_ref, sem_ref, qb_ref, sc_ref, *, layer, npages, nbuf, bs, nsel):
    b = pl.program_id(0)
    total = bs * npages
    nblk = npages // PAGES_PER_BLOCK
    dh = A_HEAD_DIM

    def page_copy(g, slot):
        bb = g // npages
        return pltpu.make_async_copy(kc_ref.at[layer, pt_ref[bb, g - bb * npages]], buf_ref.at[slot], sem_ref.at[slot])

    @pl.when(b == 0)
    def _prime():
        for s in range(nbuf):
            page_copy(s, s).start()

    qcol = _lane_column(qt_ref[...], b)
    kcol = _lane_column(kt_ref[...], b)
    for h in range(A_HEADS):
        qb_ref[h] = jnp.broadcast_to(qcol[h], (dh, PAGE_SIZE))

    def page_body(p, _):
        g = b * npages + p
        slot = lax.rem(g, nbuf)
        page_copy(g, slot).wait()
        for h in range(A_HEADS):
            sc_ref[p, h:h + 1, :] = jnp.sum(buf_ref[slot, h] * qb_ref[h], axis=0, keepdims=True)

        @pl.when(g + nbuf < total)
        def _next():
            page_copy(g + nbuf, slot).start()

        return 0

    lax.fori_loop(0, npages, page_body, 0)
    _sample_choose_and_weigh(sc_ref, qcol, kcol, pn_ref, idx_ref, pown_ref, npages, nsel)


def _sample_choose_and_weigh(sc_ref, qcol, kcol, pn_ref, idx_ref, pown_ref, npages, nsel):
    nblk = npages // PAGES_PER_BLOCK
    lane = lax.broadcasted_iota(jnp.int32, (A_HEADS, LANES), 1).astype(F32)
    sub = lax.broadcasted_iota(jnp.int32, (A_HEADS, LANES), 0)

    gate = jnp.full((A_HEADS, LANES), NEG_INF, F32)
    for j in range(nblk):
        sblk = sc_ref[PAGES_PER_BLOCK * j]
        for pp in range(1, PAGES_PER_BLOCK):
            sblk = sblk + sc_ref[PAGES_PER_BLOCK * j + pp]
        gj = jnp.sum(sblk, axis=1, keepdims=True) * (1.0 / MOBA_BLOCK)
        gate = jnp.where(lane == float(j), gj, gate)

    g = gate
    sel = jnp.zeros((A_HEADS, LANES), F32)
    idx_out = jnp.full((A_HEADS, LANES), -1.0, F32)
    for r in range(nsel):
        m = jnp.max(g, axis=1, keepdims=True)
        idx = jnp.min(jnp.where(g == m, lane, float(LANES)), axis=1, keepdims=True)
        pick = jnp.logical_and(lane == idx, m > NEG_INF)
        sel = jnp.where(pick, 1.0, sel)
        g = jnp.where(pick, NEG_INF, g)
        idx_out = jnp.where(lane == float(r), jnp.where(m > NEG_INF, idx, -1.0), idx_out)
    idx_ref[0] = idx_out.astype(jnp.int32)

    s_own = jnp.zeros((A_HEADS, LANES), F32)
    for h in range(A_HEADS):
        so = jnp.sum(qcol[h] * kcol[h], axis=0, keepdims=True)
        s_own = jnp.where(sub == h, jnp.broadcast_to(so, (A_HEADS, LANES)), s_own)

    masks = [jnp.broadcast_to(sel[:, j:j + 1], (A_HEADS, LANES)) > 0.0 for j in range(nblk)]
    mx = s_own
    for pg in range(npages):
        mx = jnp.maximum(mx, jnp.where(masks[pg // PAGES_PER_BLOCK], sc_ref[pg], NEG_INF))
    m = jnp.max(mx, axis=1, keepdims=True)
    lsum = jnp.zeros((A_HEADS, LANES), F32)
    for pg in range(npages):
        p = jnp.where(masks[pg // PAGES_PER_BLOCK], jnp.exp(sc_ref[pg] - m), 0.0)
        pn_ref[0, pg] = p
        lsum = lsum + p
    p_own = jnp.exp(s_own - m)
    inv = 1.0 / (jnp.sum(lsum, axis=1, keepdims=True) + p_own)
    for pg in range(npages):
        pn_ref[0, pg] = pn_ref[0, pg] * inv
    pown_ref[0] = p_own * inv


def _sample_score(page_table, qt, kt, cache_t, layer, nbuf=SAMPLE_PAGES_IN_FLIGHT):
    bs, npages = page_table.shape
    nblk = npages // PAGES_PER_BLOCK
    nsel = min(MOBA_TOPK, nblk + 1)
    feat = pl.BlockSpec((A_HEADS, A_HEAD_DIM, bs), lambda b, pt: (0, 0, 0))
    grid_spec = pltpu.PrefetchScalarGridSpec(
        num_scalar_prefetch=1,
        grid=(bs,),
        in_specs=[feat, feat, pl.BlockSpec(memory_space=pl.ANY)],
        out_specs=(
            pl.BlockSpec((1, npages, A_HEADS, PAGE_SIZE), lambda b, pt: (b, 0, 0, 0)),
            pl.BlockSpec((1, A_HEADS, LANES), lambda b, pt: (b, 0, 0)),
            pl.BlockSpec((1, A_HEADS, LANES), lambda b, pt: (b, 0, 0)),
        ),
        scratch_shapes=[
            pltpu.VMEM((min(nbuf, npages), A_HEADS, A_HEAD_DIM, PAGE_SIZE), F32),
            pltpu.SemaphoreType.DMA((min(nbuf, npages),)),
            pltpu.VMEM((A_HEADS, A_HEAD_DIM, PAGE_SIZE), F32),
            pltpu.VMEM((npages, A_HEADS, PAGE_SIZE), F32),
        ],
    )
    return pl.pallas_call(
        functools.partial(_sample_score_kernel, layer=layer, npages=npages, nbuf=min(nbuf, npages), bs=bs, nsel=nsel),
        grid_spec=grid_spec,
        out_shape=(
            jax.ShapeDtypeStruct((bs, npages, A_HEADS, PAGE_SIZE), F32),
            jax.ShapeDtypeStruct((bs, A_HEADS, LANES), jnp.int32),
            jax.ShapeDtypeStruct((bs, A_HEADS, LANES), F32),
        ),
        compiler_params=pltpu.CompilerParams(dimension_semantics=("arbitrary",), vmem_limit_bytes=VMEM_LIMIT_BYTES),
        name="sample_score",
    )(page_table, qt, kt, cache_t)


def _sample_pv_kernel(pt_ref, idx_ref, pn_ref, pown_ref, vt_ref, sgt_ref, vc_ref, o_ref,
                      vbuf_ref, sem_ref, *, layer, nsel, bs):
    b = pl.program_id(0)
    slot = lax.rem(b, 2)

    def issue(bb, sl):
        for h in range(A_HEADS):
            for r in range(nsel):
                j = jnp.maximum(idx_ref[bb, h * nsel + r], 0)
                for pp in range(PAGES_PER_BLOCK):
                    pg = pt_ref[bb, PAGES_PER_BLOCK * j + pp]
                    pltpu.make_async_copy(vc_ref.at[layer, pg, h], vbuf_ref.at[sl, h, r, pp], sem_ref.at[sl]).start()

    @pl.when(b == 0)
    def _first():
        issue(0, 0)
        o_ref[...] = jnp.zeros_like(o_ref)

    @pl.when(b + 1 < bs)
    def _prefetch():
        issue(b + 1, 1 - slot)

    for _ in range(A_HEADS * nsel * PAGES_PER_BLOCK):
        pltpu.make_async_copy(vc_ref.at[layer, 0, 0], vbuf_ref.at[slot, 0, 0, 0], sem_ref.at[slot]).wait()

    vcol = _lane_column(vt_ref[...], b)
    sgcol = _lane_column(sgt_ref[...], b)
    mine = lax.broadcasted_iota(jnp.int32, (1, bs), 1) == b
    for h in range(A_HEADS):
        acc = jnp.zeros((A_HEAD_DIM, PAGE_SIZE), F32)
        for r in range(nsel):
            jraw = idx_ref[b, h * nsel + r]
            j = jnp.maximum(jraw, 0)
            w = jnp.where(jraw >= 0, 1.0, 0.0)
            for pp in range(PAGES_PER_BLOCK):
                prow = pn_ref[0, PAGES_PER_BLOCK * j + pp, h:h + 1, :] * w
                acc = acc + vbuf_ref[slot, h, r, pp] * prow
        ocol = jnp.sum(acc, axis=1, keepdims=True) + pown_ref[0, h:h + 1, 0:1] * vcol[h]
        o_ref[h] = jnp.where(mine, jnp.broadcast_to(ocol * sgcol[h], (A_HEAD_DIM, bs)), o_ref[h])


def _sample_pv(page_table, idx, pn, pown, vt, sgt, cache_t, layer):
    bs, npages = page_table.shape
    nsel = idx.shape[1] // A_HEADS
    feat = pl.BlockSpec((A_HEADS, A_HEAD_DIM, bs), lambda b, pt, ix: (0, 0, 0))
    grid_spec = pltpu.PrefetchScalarGridSpec(
        num_scalar_prefetch=2,
        grid=(bs,),
        in_specs=[
            pl.BlockSpec((1, npages, A_HEADS, PAGE_SIZE), lambda b, pt, ix: (b, 0, 0, 0)),
            pl.BlockSpec((1, A_HEADS, LANES), lambda b, pt, ix: (b, 0, 0)),
            feat, feat,
            pl.BlockSpec(memory_space=pl.ANY),
        ],
        out_specs=feat,
        scratch_shapes=[
            pltpu.VMEM((2, A_HEADS, nsel, PAGES_PER_BLOCK, A_HEAD_DIM, PAGE_SIZE), F32),
            pltpu.SemaphoreType.DMA((2,)),
        ],
    )
    return pl.pallas_call(
        functools.partial(_sample_pv_kernel, layer=layer, nsel=nsel, bs=bs),
        grid_spec=grid_spec,
        out_shape=jax.ShapeDtypeStruct((A_HEADS, A_HEAD_DIM, bs), F32),
        compiler_params=pltpu.CompilerParams(dimension_semantics=("arbitrary",), vmem_limit_bytes=VMEM_LIMIT_BYTES),
        name="sample_pv",
    )(page_table, idx, pn, pown, vt, sgt, cache_t)


def _gla_sample_kernel(qt_ref, kt_ref, gt_ref, v_ref, sg_ref, s0_ref, gn_ref, s_ref, o_ref):
    b = pl.program_id(0)
    dv = G_VAL_DIM
    bs = v_ref.shape[0]

    @pl.when(b == 0)
    def _init():
        o_ref[...] = jnp.zeros_like(o_ref)

    qcol = _lane_column(qt_ref[...], b)
    kcol = _lane_column(kt_ref[...], b)
    gcol = _lane_column(gt_ref[...], b)
    mine = lax.broadcasted_iota(jnp.int32, (bs, dv), 0) == b
    for h in range(G_HEADS):
        cols = slice(h * dv, (h + 1) * dv)
        v = jnp.sum(jnp.where(mine, v_ref[:, cols], 0.0), axis=0, keepdims=True)
        sg = jnp.sum(jnp.where(mine, sg_ref[:, cols], 0.0), axis=0, keepdims=True)
        s_new = jnp.exp(gcol[h]) * s0_ref[0, h] + kcol[h] * v
        s_ref[0, h] = s_new
        o = jnp.sum(qcol[h] * s_new, axis=0, keepdims=True) * (G_KEY_DIM ** -0.5)
        ms = jnp.mean(o * o, axis=-1, keepdims=True)
        on = o * lax.rsqrt(ms + NORM_EPS) * gn_ref[...] * sg
        o_ref[:, cols] = jnp.where(mine, jnp.broadcast_to(on, (bs, dv)), o_ref[:, cols])


def _gla_sample(qt, kt, gt, v, sg, s0, gn):
    bs = s0.shape[0]
    feat = pl.BlockSpec((G_HEADS, G_KEY_DIM, bs), lambda b: (0, 0, 0))
    tok = pl.BlockSpec((bs, G_WIDTH), lambda b: (0, 0))
    sspec = pl.BlockSpec((1, G_HEADS, G_KEY_DIM, G_VAL_DIM), lambda b: (b, 0, 0, 0))
    return pl.pallas_call(
        _gla_sample_kernel,
        grid=(bs,),
        in_specs=[feat, feat, feat, tok, tok, sspec, pl.BlockSpec((1, G_VAL_DIM), lambda b: (0, 0))],
        out_specs=(sspec, tok),
        out_shape=(jax.ShapeDtypeStruct(s0.shape, F32), jax.ShapeDtypeStruct((bs, G_WIDTH), F32)),
        compiler_params=pltpu.CompilerParams(dimension_semantics=("arbitrary",)),
        name="gla_sample",
    )(qt, kt, gt, v, sg, s0, gn)


def _rope_tables(pos):
    inv = jnp.power(jnp.float32(ROPE_THETA), -jnp.arange(ROT_HALF, dtype=F32) * (2.0 / ROT_DIM))
    ang = pos.astype(F32)[:, None] * inv[None, :]
    return jnp.cos(ang).T, jnp.sin(ang).T


def _pick_tile(n, pref):
    t = min(n, pref)
    while n % t:
        t //= 2
    return t


def kernel(x_prompt, x_sample, cache_k, cache_v, state_gla, page_table, norm_g, w_in, w_a2, b_a2, gla_norm_g, w_out, final_norm_g):
    bp, lp, d = x_prompt.shape
    bs, ls, _ = x_sample.shape
    depth = norm_g.shape[0]
    past_len = page_table.shape[1] * PAGE_SIZE
    assert ls == 1 and lp % MOBA_BLOCK == 0 and past_len % MOBA_BLOCK == 0
    assert cache_k.shape[2:] == (PAGE_SIZE, A_HEADS, A_HEAD_DIM)

    tabs_p = _rope_tables(jnp.arange(lp, dtype=jnp.int32))
    tabs_s = _rope_tables(jnp.full((bs,), past_len, dtype=jnp.int32))
    kc_t = jnp.transpose(cache_k, (0, 1, 3, 4, 2))
    vc_t = jnp.transpose(cache_v, (0, 1, 3, 4, 2))

    tm_p = _pick_tile(lp, 512)
    tt = _pick_tile(lp, 256)
    hp = x_prompt.reshape(bp * lp, d)
    hs = x_sample.reshape(bs, d)
    fg = final_norm_g.reshape(1, d)
    sp_l, ks_l, vs_l, ss_l = [], [], [], []
    kv_prompt = (jnp.zeros((depth, bp, A_WIDTH, lp), F32), jnp.zeros((depth, bp, A_WIDTH, lp), F32))
    rows = A_WIDTH
    g_lo = 4 * rows
    g_hi = g_lo + 2 * G_KEY_WIDTH + 2 * G_WIDTH
    for l in range(depth):
        wt = jnp.transpose(w_in[l])
        w_t = wt[0:4 * rows].astype(BF16)
        a1_t = jnp.pad(wt[g_hi:], ((0, LANES - G_GATE_RANK), (0, 0)))
        w_row = jnp.transpose(jnp.concatenate([wt[g_lo:g_hi], a1_t], axis=0)).astype(BF16)
        wa2 = jnp.pad(w_a2[l], ((0, LANES - G_GATE_RANK), (0, 0))).astype(BF16)
        ba2 = b_a2[l].reshape(1, G_KEY_WIDTH)
        ng = norm_g[l].reshape(1, d)
        gn = gla_norm_g[l].reshape(1, G_VAL_DIM)
        wo = w_out[l].astype(BF16)
        last = l == depth - 1

        qt_s, kt_s, vt_s, sgt_s, qg_s, kg_s, vg_s, sgg_s, lf_s = _proj_in(
            hs, ng, w_row, w_t, wa2, ba2, tabs_s, 1, bs, 0, 1)
        a_feat = lambda a: a.reshape(A_HEADS, A_HEAD_DIM, bs)

        qt, kt_all, vt_all, sgt, qg, kg, vg, sgg, lf = _proj_in(
            hp, ng, w_row, w_t, wa2, ba2, tabs_p, bp, tm_p, l, depth, kv_prompt)
        kv_prompt = (kt_all, vt_all)
        moba_steps = bp * (A_WIDTH // (MOBA_HEADS_PER_STEP * A_HEAD_DIM)) * (lp // MOBA_BLOCK + 1)
        pps = _score_pages_per_step(page_table.shape[1], bs, moba_steps)
        if pps is None:
            oa = _moba_prompt(qt, kt_all, vt_all, sgt, l)
            pn, idx, pown = _sample_score(page_table, a_feat(qt_s), a_feat(kt_s), kc_t, l)
        else:
            oa, pn, idx, pown = _moba_prompt(
                qt, kt_all, vt_all, sgt, l, (page_table, a_feat(qt_s), a_feat(kt_s), kc_t, pps))
        og, s_fin = _gla_prompt(qg.reshape(bp, lp, -1), kg.reshape(bp, lp, -1), vg.reshape(bp, lp, -1),
                                lf.reshape(bp, lp, -1), sgg.reshape(bp, lp, -1), gn, tt)
        hp = _proj_out(oa.reshape(bp * lp, A_WIDTH), og.reshape(bp * lp, G_WIDTH), hp, wo, fg, tm_p, last)
        sp_l.append(s_fin)

        qg, kg, vg, sgg, lf, sgt = qg_s, kg_s, vg_s, sgg_s, lf_s, sgt_s
        nsel = min(MOBA_TOPK, page_table.shape[1] // PAGES_PER_BLOCK + 1)
        idx_s = idx[:, :, :nsel].reshape(bs, A_HEADS * nsel)
        oa_s = _sample_pv(page_table, idx_s, pn, pown, a_feat(vt_s), a_feat(sgt), vc_t, l)
        g_feat = lambda a: jnp.transpose(a.astype(F32)).reshape(G_HEADS, G_KEY_DIM, bs)
        s_new, og_s = _gla_sample(g_feat(qg), g_feat(kg), g_feat(lf), vg.astype(F32), sgg.astype(F32),
                                  state_gla[l], gn)
        hs = _proj_out(oa_s.reshape(A_WIDTH, bs), og_s, hs, wo, fg, bs, last, a_feature_major=True)
        ks_l.append(jnp.transpose(kt_s[0, 0]).reshape(bs, 1, A_HEADS, A_HEAD_DIM))
        vs_l.append(jnp.transpose(vt_s[0, 0]).reshape(bs, 1, A_HEADS, A_HEAD_DIM))
        ss_l.append(s_new)

    kt_all, vt_all = kv_prompt
    to_cache_layout = lambda a: jnp.transpose(a.reshape(depth, bp, A_HEADS, A_HEAD_DIM, lp), (0, 1, 4, 2, 3))
    return (hp.reshape(bp, lp, d), hs.reshape(bs, ls, d), to_cache_layout(kt_all), to_cache_layout(vt_all),
            jnp.stack(sp_l), jnp.stack(ks_l), jnp.stack(vs_l), jnp.stack(ss_l))
```

```python
import functools

import jax
import jax.numpy as jnp
from jax import lax
from jax.experimental import pallas as pl
from jax.experimental.pallas import tpu as pltpu

A_HEADS = 8
A_HEAD_DIM = 64
A_WIDTH = A_HEADS * A_HEAD_DIM
ROT_DIM = A_HEAD_DIM // 4
ROT_HALF = ROT_DIM // 2
ROPE_THETA = 500000.0
MOBA_BLOCK = 256
MOBA_TOPK = 3
G_HEADS = 4
G_KEY_DIM = 64
G_VAL_DIM = 128
G_KEY_WIDTH = G_HEADS * G_KEY_DIM
G_WIDTH = G_HEADS * G_VAL_DIM
G_GATE_RANK = 16
G_GATE_NORM = 16.0
G_CHUNK = 32
NORM_EPS = 1e-6
PAGE_SIZE = 128
PAGES_PER_BLOCK = MOBA_BLOCK // PAGE_SIZE

LANES = 128
VMEM_LIMIT_BYTES = 56 * 1024 * 1024

F32 = jnp.float32
BF16 = jnp.bfloat16
NEG_INF = float("-inf")
MASK_BIAS = -1e30
MOBA_HEADS_PER_STEP = 4
MOBA_BLOCKS_PER_ITERATION = 4
SAMPLE_PAGES_IN_FLIGHT = 24
V_PAD_ROWS = 16

_NT = (((1,), (1,)), ((), ()))
_TN = (((0,), (0,)), ((), ()))


def _dot(a, b):
    return jnp.dot(a, b, preferred_element_type=F32)


def _dot_nt(a, b):
    return lax.dot_general(a, b, _NT, preferred_element_type=F32)


def _dot_tn(a, b):
    return lax.dot_general(a, b, _TN, preferred_element_type=F32)


def _split3(x):
    x1 = x.astype(BF16)
    r1 = x - x1.astype(F32)
    x2 = r1.astype(BF16)
    x3 = (r1 - x2.astype(F32)).astype(BF16)
    return x1, x2, x3


def _lane_column(x, lane):
    pick = lax.broadcasted_iota(jnp.int32, (1,) * (x.ndim - 1) + (x.shape[-1],), x.ndim - 1) == lane
    return jnp.sum(jnp.where(pick, x.astype(F32), 0.0), axis=-1, keepdims=True)


def _proj_in_kernel(*refs, n_alias):
    x_ref, g_ref, w_ref, wt_ref, wa2_ref, ba2_ref, ct_ref, st_ref = refs[:8]
    qt_ref, kt_ref, vt_ref, sgt_ref, qg_ref, kg_ref, vg_ref, sgg_ref, lf_ref = refs[8 + n_alias:]
    x = x_ref[...]
    ms = jnp.mean(x * x, axis=-1, keepdims=True)
    h = (x * lax.rsqrt(ms + NORM_EPS) * g_ref[...]).astype(BF16)

    def proj(lo, hi):
        return _dot(h, w_ref[:, lo:hi])

    o = 0
    qg_ref[...] = proj(o, o + G_KEY_WIDTH).astype(qg_ref.dtype)
    o += G_KEY_WIDTH
    kg_ref[...] = proj(o, o + G_KEY_WIDTH).astype(kg_ref.dtype)
    o += G_KEY_WIDTH
    vg_ref[...] = proj(o, o + G_WIDTH).astype(vg_ref.dtype)
    o += G_WIDTH
    zg = proj(o, o + G_WIDTH)
    sgg_ref[...] = (zg * jax.nn.sigmoid(zg)).astype(sgg_ref.dtype)
    o += G_WIDTH
    a1 = proj(o, o + LANES)
    la = _dot(a1.astype(BF16), wa2_ref[...]) + ba2_ref[...]
    lf_ref[...] = -(jnp.maximum(-la, 0.0) + jnp.log1p(jnp.exp(-jnp.abs(la)))) * (1.0 / G_GATE_NORM)

    zt = _dot_nt(wt_ref[...], h)
    ct, st = ct_ref[...], st_ref[...]
    q_scale = A_HEAD_DIM ** -0.5
    for hd in range(A_HEADS):
        b0 = hd * A_HEAD_DIM
        x1 = zt[b0:b0 + ROT_HALF]
        x2 = zt[b0 + ROT_HALF:b0 + ROT_DIM]
        qh = jnp.concatenate([x1 * ct - x2 * st, x2 * ct + x1 * st, zt[b0 + ROT_DIM:b0 + A_HEAD_DIM]], axis=0)
        qt_ref[b0:b0 + A_HEAD_DIM, :] = (qh * q_scale).astype(qt_ref.dtype)
        k0 = b0 + A_WIDTH
        x1 = zt[k0:k0 + ROT_HALF]
        x2 = zt[k0 + ROT_HALF:k0 + ROT_DIM]
        kt_ref[b0:b0 + ROT_HALF, :] = x1 * ct - x2 * st
        kt_ref[b0 + ROT_HALF:b0 + ROT_DIM, :] = x2 * ct + x1 * st
        kt_ref[b0 + ROT_DIM:b0 + A_HEAD_DIM, :] = zt[k0 + ROT_DIM:k0 + A_HEAD_DIM]
    vt_ref[...] = zt[2 * A_WIDTH:3 * A_WIDTH]
    zg = zt[3 * A_WIDTH:4 * A_WIDTH]
    sgt_ref[...] = (zg * jax.nn.sigmoid(zg)).astype(sgt_ref.dtype)


def _proj_in(x, norm_g, w_row, w_t, wa2, ba2, tabs, n_batch, tm, layer, depth, kv_prev=None):
    n, d = x.shape
    seq = n // n_batch
    nl = seq // tm
    ct_tab, st_tab = tabs
    row = lambda i: (i, 0)
    const = lambda i: (0, 0)
    feat_shape = jax.ShapeDtypeStruct((n_batch, A_WIDTH, seq), BF16)
    kv_shape = jax.ShapeDtypeStruct((depth, n_batch, A_WIDTH, seq), F32)
    out_shapes = (
        feat_shape,
        kv_shape, kv_shape,
        feat_shape,
        jax.ShapeDtypeStruct((n, G_KEY_WIDTH), BF16),
        jax.ShapeDtypeStruct((n, G_KEY_WIDTH), BF16),
        jax.ShapeDtypeStruct((n, G_WIDTH), BF16),
        jax.ShapeDtypeStruct((n, G_WIDTH), BF16),
        jax.ShapeDtypeStruct((n, G_KEY_WIDTH), F32),
    )
    feat_spec = pl.BlockSpec((None, A_WIDTH, tm), lambda i: (i // nl, 0, i % nl))
    kv_spec = pl.BlockSpec((None, None, A_WIDTH, tm), lambda i: (layer, i // nl, 0, i % nl))
    in_specs = [
        pl.BlockSpec((tm, d), row),
        pl.BlockSpec((1, d), const),
        pl.BlockSpec(w_row.shape, const),
        pl.BlockSpec(w_t.shape, const),
        pl.BlockSpec((LANES, G_KEY_WIDTH), const),
        pl.BlockSpec((1, G_KEY_WIDTH), const),
        pl.BlockSpec((ROT_HALF, tm), lambda i: (0, i % nl)),
        pl.BlockSpec((ROT_HALF, tm), lambda i: (0, i % nl)),
    ]
    args = [x, norm_g, w_row, w_t, wa2, ba2, ct_tab, st_tab]
    aliases = {}
    if kv_prev is not None:
        in_specs += [pl.BlockSpec(memory_space=pl.ANY)] * 2
        aliases = {len(args): 1, len(args) + 1: 2}
        args += list(kv_prev)
    return pl.pallas_call(
        functools.partial(_proj_in_kernel, n_alias=len(aliases)),
        grid=(n // tm,),
        in_specs=in_specs,
        out_specs=(
            feat_spec, kv_spec, kv_spec, feat_spec,
            pl.BlockSpec((tm, G_KEY_WIDTH), row), pl.BlockSpec((tm, G_KEY_WIDTH), row),
            pl.BlockSpec((tm, G_WIDTH), row), pl.BlockSpec((tm, G_WIDTH), row),
            pl.BlockSpec((tm, G_KEY_WIDTH), row),
        ),
        out_shape=out_shapes,
        input_output_aliases=aliases,
        compiler_params=pltpu.CompilerParams(dimension_semantics=("arbitrary",), vmem_limit_bytes=VMEM_LIMIT_BYTES),
        name="proj_in",
    )(*args)


def _proj_out_kernel(a_ref, g_ref, x_ref, w_ref, fg_ref, o_ref, *, final, a_feature_major):
    half = g_ref.shape[1]
    a = a_ref[...].astype(BF16)
    ya = _dot_tn(a, w_ref[0:half, :]) if a_feature_major else _dot(a, w_ref[0:half, :])
    xo = x_ref[...] + (ya + _dot(g_ref[...].astype(BF16), w_ref[half:, :]))
    if final:
        ms = jnp.mean(xo * xo, axis=-1, keepdims=True)
        xo = xo * lax.rsqrt(ms + NORM_EPS) * fg_ref[...]
    o_ref[...] = xo


def _proj_out(a, g, x, w, fg, tm, final, a_feature_major=False):
    n, d = x.shape
    row = lambda i: (i, 0)
    const = lambda i: (0, 0)
    a_spec = pl.BlockSpec((a.shape[0], tm), lambda i: (0, i)) if a_feature_major else pl.BlockSpec((tm, a.shape[1]), row)
    return pl.pallas_call(
        functools.partial(_proj_out_kernel, final=final, a_feature_major=a_feature_major),
        grid=(n // tm,),
        in_specs=[
            a_spec,
            pl.BlockSpec((tm, g.shape[1]), row),
            pl.BlockSpec((tm, d), row),
            pl.BlockSpec(w.shape, const),
            pl.BlockSpec((1, d), const),
        ],
        out_specs=pl.BlockSpec((tm, d), row),
        out_shape=jax.ShapeDtypeStruct((n, d), F32),
        compiler_params=pltpu.CompilerParams(dimension_semantics=("arbitrary",), vmem_limit_bytes=VMEM_LIMIT_BYTES),
        name="proj_out",
    )(a, g, x, w, fg)


def _moba_prompt_kernel(*refs, nb, nsel, unroll, side):
    if side is None:
        qt_ref, kt_ref, vt_ref, sgt_ref, o_ref, ka_ref, va_ref, km_ref, qa_ref, s_ref, acc_ref, m_ref = refs
    else:
        (pt_ref, qt_ref, kt_ref, vt_ref, sgt_ref, qs_ref, ks_ref, kc_ref, o_ref, pn_ref, idx_ref, pown_ref,
         ka_ref, va_ref, km_ref, qa_ref, s_ref, acc_ref, m_ref, buf_ref, sem_ref, qb_ref, sc_ref) = refs
    qi = pl.program_id(2)
    blk = MOBA_BLOCK
    dh = A_HEAD_DIM
    heads = qt_ref.shape[0] // dh
    per_group = LANES // dh

    if side is not None:
        layer, npages, bs, pps, nsel_s = side
        sps = npages // pps
        step = (pl.program_id(0) * pl.num_programs(1) + pl.program_id(1)) * pl.num_programs(2) + qi
        seq = step // sps
        part = step - seq * sps
        total = bs * npages
        half = lax.rem(step, 2) * pps

        def page_copy(g, slot):
            bb = g // npages
            return pltpu.make_async_copy(
                kc_ref.at[layer, pt_ref[bb, g - bb * npages]], buf_ref.at[slot], sem_ref.at[slot])

        @pl.when(step == 0)
        def _prime():
            for i in range(2 * pps):
                page_copy(i, i).start()

        @pl.when(seq < bs)
        def _score_pages():
            @pl.when(part == 0)
            def _new_sequence():
                qcol = _lane_column(qs_ref[...], seq)
                for h in range(A_HEADS):
                    qb_ref[h] = jnp.broadcast_to(qcol[h], (dh, PAGE_SIZE))

            g0 = step * pps
            for i in range(pps):
                page_copy(g0 + i, half + i).wait()
            for i in range(pps):
                for h in range(A_HEADS):
                    sc_ref[part * pps + i, h:h + 1, :] = jnp.sum(
                        buf_ref[half + i, h] * qb_ref[h], axis=0, keepdims=True)

            @pl.when(g0 + 2 * pps < total)
            def _refill():
                for i in range(pps):
                    page_copy(g0 + i + 2 * pps, half + i).start()

            @pl.when(part == sps - 1)
            def _finish_sequence():
                _sample_choose_and_weigh(sc_ref, _lane_column(qs_ref[...], seq), _lane_column(ks_ref[...], seq),
                                         pn_ref, idx_ref, pown_ref, npages, nsel_s)

    @pl.when(qi == 0)
    def _prepare():
        lane = lax.broadcasted_iota(jnp.int32, (blk, LANES), 1)
        ones_rows = jnp.where(lax.broadcasted_iota(jnp.int32, (V_PAD_ROWS, blk), 0) == 0, 1.0, 0.0)
        for j in range(nb):
            kj = kt_ref[:, j * blk:(j + 1) * blk].T
            km_ref[j:j + 1, :] = jnp.mean(kj, axis=0, keepdims=True)
            onehot = jnp.where(lane == dh + j, 1.0, 0.0)
            for hh in range(heads):
                grp, sub = divmod(hh, per_group)
                kh = kj[:, grp * LANES:(grp + 1) * LANES]
                if sub:
                    kh = pltpu.roll(kh, LANES - sub * dh, 1)
                ka_ref[hh, j * blk:(j + 1) * blk, :] = jnp.where(lane < dh, kh, onehot).astype(BF16)
                vh = vt_ref[hh * dh:(hh + 1) * dh, j * blk:(j + 1) * blk]
                va_ref[j, hh] = jnp.concatenate([vh, ones_rows], axis=0).astype(BF16)
        m_ref[...] = jnp.zeros_like(m_ref)

    @pl.when(jnp.logical_and(qi == 0, jnp.logical_and(pl.program_id(0) == 0, pl.program_id(1) == 0)))
    def _define_scores():
        s_ref[...] = jnp.zeros_like(s_ref)

    qb = jnp.minimum(qi, nb - 1)
    prev = jnp.maximum(qi - 1, 0)
    m_old = [m_ref[hh] for hh in range(heads)]

    for hh in range(heads):
        acc_ref[hh] = _dot(va_ref[prev, hh], jnp.exp(s_ref[hh, nb] - m_old[hh]).astype(BF16))

    rowf = lax.broadcasted_iota(jnp.int32, (nb, blk), 0).astype(F32)
    past = rowf < qb.astype(F32)
    causal = lax.broadcasted_iota(jnp.int32, (blk, blk), 0) <= lax.broadcasted_iota(jnp.int32, (blk, blk), 1)
    pad = jnp.zeros((LANES - dh - nb, blk), BF16)
    r_own = pl.multiple_of(qb * blk, blk)

    m_run, qd = [], []
    for hh in range(heads):
        qh = qt_ref[hh * dh:(hh + 1) * dh, :]
        km = km_ref[:, hh * dh:(hh + 1) * dh]
        km1 = km.astype(BF16)
        km2 = (km - km1.astype(F32)).astype(BF16)
        g = jnp.where(past, _dot(km1, qh) + _dot(km2, qh), NEG_INF)
        sel = jnp.zeros((nb, blk), F32)
        for _ in range(nsel):
            m = jnp.max(g, axis=0, keepdims=True)
            idx = jnp.min(jnp.where(g == m, rowf, float(nb)), axis=0, keepdims=True)
            pick = jnp.logical_and(rowf == idx, m > NEG_INF)
            sel = jnp.where(pick, 1.0, sel)
            g = jnp.where(pick, NEG_INF, g)
        bias = jnp.where(sel > 0.0, 0.0, MASK_BIAS).astype(BF16)
        qa_ref[hh] = jnp.concatenate([qh, bias, pad], axis=0)
        qd.append(jnp.concatenate([qh, jnp.zeros((LANES - dh, blk), BF16)], axis=0))

    for hh in range(heads):
        s = jnp.where(causal, _dot(ka_ref[hh, pl.ds(r_own, blk), :], qd[hh]), MASK_BIAS)
        s_ref[hh, nb] = s
        m_run.append(jnp.max(s, axis=0, keepdims=True))

    def blocks(j0, count, ms):
        ms = list(ms)
        for hh in range(heads):
            part = None
            for u in range(count):
                j = j0 + u
                p = jnp.where(j < prev, jnp.exp(s_ref[hh, j] - m_old[hh]), 0.0).astype(BF16)
                t = _dot(va_ref[j, hh], p)
                part = t if part is None else part + t
                r0 = pl.multiple_of(j * blk, blk)
                s = _dot(ka_ref[hh, pl.ds(r0, blk), :], qa_ref[hh])
                s_ref[hh, j] = s
                ms[hh] = jnp.maximum(ms[hh], jnp.max(s, axis=0, keepdims=True))
            acc_ref[hh] += part
        return tuple(ms)

    m_fin = tuple(m_run)
    done = 0
    width = unroll
    while width >= 1:
        trips = lax.shift_right_logical(qi - done, width.bit_length() - 1)
        m_fin = lax.fori_loop(0, trips, functools.partial(
            lambda gi, ms, j0, w: blocks(j0 + gi * w, w, ms), j0=done, w=width), m_fin)
        done = done + trips * width
        width //= 2
    for hh in range(heads):
        m_ref[hh] = m_fin[hh]

    ot = jnp.concatenate([acc_ref[hh, 0:dh, :] / acc_ref[hh, dh:dh + 1, :] for hh in range(heads)], axis=0)
    o_ref[...] = (ot * sgt_ref[...].astype(F32)).T.astype(o_ref.dtype)


def _score_pages_per_step(npages, bs, steps):
    for pps in range(1, npages + 1):
        if npages % pps == 0 and bs * (npages // pps) <= steps:
            return pps
    return None


def _moba_prompt(qt, kt_all, vt_all, sgt, layer, decode=None):
    b, width, seq = qt.shape
    nb = seq // MOBA_BLOCK
    nsel = min(MOBA_TOPK, nb)
    heads = MOBA_HEADS_PER_STEP
    hw = heads * A_HEAD_DIM
    assert A_HEAD_DIM + nb <= LANES and width % hw == 0
    unroll = MOBA_BLOCKS_PER_ITERATION
    n_hp = width // hw
    grid = (b, n_hp, nb + 1)
    scratch = [
        pltpu.VMEM((heads, seq, LANES), BF16),
        pltpu.VMEM((nb, heads, A_HEAD_DIM + V_PAD_ROWS, MOBA_BLOCK), BF16),
        pltpu.VMEM((nb, hw), F32),
        pltpu.VMEM((heads, LANES, MOBA_BLOCK), BF16),
        pltpu.VMEM((heads, nb + 1, MOBA_BLOCK, MOBA_BLOCK), F32),
        pltpu.VMEM((heads, A_HEAD_DIM + V_PAD_ROWS, MOBA_BLOCK), F32),
        pltpu.VMEM((heads, 1, MOBA_BLOCK), F32),
    ]
    params = pltpu.CompilerParams(
        dimension_semantics=("arbitrary", "arbitrary", "arbitrary"), vmem_limit_bytes=VMEM_LIMIT_BYTES)
    q_map = lambda bi, hp, qi, *_: (bi, hp, jnp.minimum(qi, nb - 1))
    g_map = lambda bi, hp, qi, *_: (bi, hp, jnp.maximum(qi - 1, 0))
    kv_map = lambda bi, hp, qi, *_: (layer, bi, hp, 0)
    o_map = lambda bi, hp, qi, *_: (bi, jnp.maximum(qi - 1, 0), hp)
    in_specs = [pl.BlockSpec((None, hw, MOBA_BLOCK), q_map), pl.BlockSpec((None, None, hw, seq), kv_map),
                pl.BlockSpec((None, None, hw, seq), kv_map), pl.BlockSpec((None, hw, MOBA_BLOCK), g_map)]
    o_spec = pl.BlockSpec((None, MOBA_BLOCK, hw), o_map)
    o_shape = jax.ShapeDtypeStruct((b, seq, width), BF16)
    if decode is None:
        return pl.pallas_call(
            functools.partial(_moba_prompt_kernel, nb=nb, nsel=nsel, unroll=unroll, side=None),
            grid=grid, in_specs=in_specs, out_specs=o_spec, out_shape=o_shape, scratch_shapes=scratch,
            compiler_params=params, name="moba_prompt",
        )(qt, kt_all, vt_all, sgt)

    page_table, q_s, k_s, cache_t, pps = decode
    bs, npages = page_table.shape
    sps = npages // pps
    nsel_s = min(MOBA_TOPK, npages // PAGES_PER_BLOCK + 1)
    seq_of = lambda bi, hp, qi: jnp.minimum(((bi * n_hp + hp) * (nb + 1) + qi) // sps, bs - 1)
    feat = pl.BlockSpec((A_HEADS, A_HEAD_DIM, bs), lambda bi, hp, qi, pt: (0, 0, 0))
    grid_spec = pltpu.PrefetchScalarGridSpec(
        num_scalar_prefetch=1,
        grid=grid,
        in_specs=in_specs + [feat, feat, pl.BlockSpec(memory_space=pl.ANY)],
        out_specs=(
            o_spec,
            pl.BlockSpec((1, npages, A_HEADS, PAGE_SIZE), lambda bi, hp, qi, pt: (seq_of(bi, hp, qi), 0, 0, 0)),
            pl.BlockSpec((1, A_HEADS, LANES), lambda bi, hp, qi, pt: (seq_of(bi, hp, qi), 0, 0)),
            pl.BlockSpec((1, A_HEADS, LANES), lambda bi, hp, qi, pt: (seq_of(bi, hp, qi), 0, 0)),
        ),
        scratch_shapes=scratch + [
            pltpu.VMEM((2 * pps, A_HEADS, A_HEAD_DIM, PAGE_SIZE), F32),
            pltpu.SemaphoreType.DMA((2 * pps,)),
            pltpu.VMEM((A_HEADS, A_HEAD_DIM, PAGE_SIZE), F32),
            pltpu.VMEM((npages, A_HEADS, PAGE_SIZE), F32),
        ],
    )
    return pl.pallas_call(
        functools.partial(_moba_prompt_kernel, nb=nb, nsel=nsel, unroll=unroll,
                          side=(layer, npages, bs, pps, nsel_s)),
        grid_spec=grid_spec,
        out_shape=(
            o_shape,
            jax.ShapeDtypeStruct((bs, npages, A_HEADS, PAGE_SIZE), F32),
            jax.ShapeDtypeStruct((bs, A_HEADS, LANES), jnp.int32),
            jax.ShapeDtypeStruct((bs, A_HEADS, LANES), F32),
        ),
        compiler_params=params,
        name="moba_prompt",
    )(page_table, qt, kt_all, vt_all, sgt, q_s, k_s, cache_t)


def _gla_prompt_kernel(*refs, nt, side):
    if side is None:
        q_ref, k_ref, v_ref, lf_ref, sg_ref, gn_ref, o_ref, s_ref, st_ref = refs
    else:
        (pt_ref, ix_ref, q_ref, k_ref, v_ref, lf_ref, sg_ref, gn_ref, pn_ref, pown_ref, vts_ref, sgs_ref, vc_ref,
         o_ref, s_ref, oas_ref, st_ref, vbuf_ref, sem_ref) = refs
    t = pl.program_id(1)
    tt = q_ref.shape[1]
    ck = G_CHUNK
    dk, dv = G_KEY_DIM, G_VAL_DIM
    nh = G_HEADS

    if side is not None:
        layer, bs, nsel_s = side
        step = pl.program_id(0) * pl.num_programs(1) + t
        seq = lax.shift_right_logical(step, 1)
        second = step - 2 * seq
        n_copies = A_HEADS * nsel_s * PAGES_PER_BLOCK

        @pl.when(step == 0)
        def _init_decode_out():
            oas_ref[...] = jnp.zeros_like(oas_ref)

        @pl.when(jnp.logical_and(seq < bs, second == 0))
        def _gather():
            for h in range(A_HEADS):
                for r in range(nsel_s):
                    j = jnp.maximum(ix_ref[seq, h * nsel_s + r], 0)
                    for pp in range(PAGES_PER_BLOCK):
                        pg = pt_ref[seq, PAGES_PER_BLOCK * j + pp]
                        pltpu.make_async_copy(vc_ref.at[layer, pg, h], vbuf_ref.at[h, r, pp], sem_ref.at[0]).start()

        @pl.when(jnp.logical_and(seq < bs, second == 1))
        def _apply():
            for _ in range(n_copies):
                pltpu.make_async_copy(vc_ref.at[layer, 0, 0], vbuf_ref.at[0, 0, 0], sem_ref.at[0]).wait()
            _sample_apply_weights(seq, ix_ref, pn_ref, pown_ref, vts_ref, sgs_ref, vbuf_ref, oas_ref, nsel_s, bs)

    @pl.when(t == 0)
    def _init():
        st_ref[...] = jnp.zeros_like(st_ref)

    g1, g2, g3 = _split3(lf_ref[0])
    row = lax.broadcasted_iota(jnp.int32, (tt, tt), 0)
    col = lax.broadcasted_iota(jnp.int32, (tt, tt), 1)
    ck_shift = ck.bit_length() - 1
    dk_shift = dk.bit_length() - 1
    same = lax.shift_right_logical(row, ck_shift) == lax.shift_right_logical(col, ck_shift)
    lower = jnp.where(jnp.logical_and(same, col <= row), 1.0, 0.0).astype(BF16)
    upper = jnp.where(jnp.logical_and(same, col > row), 1.0, 0.0).astype(BF16)
    bcum = _dot(lower, g1) + _dot(lower, g2) + _dot(lower, g3)
    rest = _dot(upper, g1) + _dot(upper, g2) + _dot(upper, g3)

    qf = q_ref[0].astype(F32)
    kf = k_ref[0].astype(F32)
    qt = qf * jnp.exp(bcum) * (dk ** -0.5)
    kt = kf * jnp.exp(-bcum)
    kd = kf * jnp.exp(rest)

    head_of_lane = lax.shift_right_logical(lax.broadcasted_iota(jnp.int32, (ck, nh * dk), 1), dk_shift)
    own = [head_of_lane == hd for hd in range(nh)]
    ri = lax.broadcasted_iota(jnp.int32, (nh * ck, nh * ck), 0)
    ci = lax.broadcasted_iota(jnp.int32, (nh * ck, nh * ck), 1)
    causal = jnp.bitwise_and(ri, ck - 1) >= jnp.bitwise_and(ci, ck - 1)
    gn = gn_ref[...]
    st = st_ref[...]

    def stack(x):
        return jnp.concatenate([jnp.where(own[hd], x, 0.0) for hd in range(nh)], axis=0).astype(BF16)

    nc = tt // ck
    q4s, o_intra, kvs = [], [], []
    for c in range(nc):
        r0, r1 = c * ck, (c + 1) * ck
        q4, k4, kd4 = stack(qt[r0:r1]), stack(kt[r0:r1]), stack(kd[r0:r1])
        v4 = jnp.concatenate([v_ref[0, r0:r1, hd * dv:(hd + 1) * dv] for hd in range(nh)], axis=0)
        a = jnp.where(causal, _dot_nt(q4, k4), 0.0)
        q4s.append(q4)
        o_intra.append(_dot(a.astype(BF16), v4))
        kvs.append(_dot_tn(v4, kd4))

    for c in range(nc):
        r0, r1 = c * ck, (c + 1) * ck
        o = o_intra[c] + _dot_nt(q4s[c], st.astype(BF16))
        ms = jnp.mean(o * o, axis=-1, keepdims=True)
        on = o * lax.rsqrt(ms + NORM_EPS) * gn
        for hd in range(nh):
            sg = sg_ref[0, r0:r1, hd * dv:(hd + 1) * dv].astype(F32)
            o_ref[0, r0:r1, hd * dv:(hd + 1) * dv] = (on[hd * ck:(hd + 1) * ck] * sg).astype(o_ref.dtype)
        st = st * jnp.exp(bcum[r1 - 1:r1, :]) + kvs[c]

    st_ref[...] = st

    @pl.when(t == nt - 1)
    def _emit():
        per_group = LANES // dk
        for grp in range(nh // per_group):
            s_grp = st[:, grp * LANES:(grp + 1) * LANES].T
            for hh in range(per_group):
                s_ref[0, grp * per_group + hh] = s_grp[hh * dk:(hh + 1) * dk, :]


def _gla_prompt(qg, kg, vg, lf, sg, gn, tt, decode=None):
    b, seq, _ = qg.shape
    nt = seq // tt
    kmap = lambda bi, ti, *_: (bi, ti, 0)
    kspec = pl.BlockSpec((1, tt, G_KEY_WIDTH), kmap)
    vspec = pl.BlockSpec((1, tt, G_WIDTH), kmap)
    in_specs = [kspec, kspec, vspec, kspec, vspec, pl.BlockSpec((1, G_VAL_DIM), lambda bi, ti, *_: (0, 0))]
    out_specs = (vspec, pl.BlockSpec((1, G_HEADS, G_KEY_DIM, G_VAL_DIM), lambda bi, ti, *_: (bi, 0, 0, 0)))
    out_shape = (jax.ShapeDtypeStruct((b, seq, G_WIDTH), BF16),
                 jax.ShapeDtypeStruct((b, G_HEADS, G_KEY_DIM, G_VAL_DIM), F32))
    scratch = [pltpu.VMEM((G_VAL_DIM, G_KEY_WIDTH), F32)]
    params = pltpu.CompilerParams(dimension_semantics=("arbitrary", "arbitrary"), vmem_limit_bytes=VMEM_LIMIT_BYTES)
    if decode is None:
        return pl.pallas_call(
            functools.partial(_gla_prompt_kernel, nt=nt, side=None),
            grid=(b, nt), in_specs=in_specs, out_specs=out_specs, out_shape=out_shape, scratch_shapes=scratch,
            compiler_params=params, name="gla_prompt",
        )(qg, kg, vg, lf, sg, gn)

    page_table, idx, pn, pown, vt_s, sgt_s, cache_t, layer = decode
    bs, npages = page_table.shape
    nsel_s = idx.shape[1] // A_HEADS
    seq_of = lambda bi, ti: jnp.minimum((bi * nt + ti) // 2, bs - 1)
    feat = pl.BlockSpec((A_HEADS, A_HEAD_DIM, bs), lambda bi, ti, pt, ix: (0, 0, 0))
    grid_spec = pltpu.PrefetchScalarGridSpec(
        num_scalar_prefetch=2,
        grid=(b, nt),
        in_specs=in_specs + [
            pl.BlockSpec((1, npages, A_HEADS, PAGE_SIZE), lambda bi, ti, pt, ix: (seq_of(bi, ti), 0, 0, 0)),
            pl.BlockSpec((1, A_HEADS, LANES), lambda bi, ti, pt, ix: (seq_of(bi, ti), 0, 0)),
            feat, feat,
            pl.BlockSpec(memory_space=pl.ANY),
        ],
        out_specs=out_specs + (feat,),
        scratch_shapes=scratch + [
            pltpu.VMEM((A_HEADS, nsel_s, PAGES_PER_BLOCK, A_HEAD_DIM, PAGE_SIZE), F32),
            pltpu.SemaphoreType.DMA((1,)),
        ],
    )
    return pl.pallas_call(
        functools.partial(_gla_prompt_kernel, nt=nt, side=(layer, bs, nsel_s)),
        grid_spec=grid_spec,
        out_shape=out_shape + (jax.ShapeDtypeStruct((A_HEADS, A_HEAD_DIM, bs), F32),),
        compiler_params=params,
        name="gla_prompt",
    )(page_table, idx, qg, kg, vg, lf, sg, gn, pn, pown, vt_s, sgt_s, cache_t)


def _sample_score_kernel(pt_ref, qt_ref, kt_ref, kc_ref, pn_ref, idx_ref, pown_ref,
                         buf_ref, sem_ref, qb_ref, sc_ref, *, layer, npages, nbuf, bs, nsel):
    b = pl.program_id(0)
    total = bs * npages
    nblk = npages // PAGES_PER_BLOCK
    dh = A_HEAD_DIM

    def page_copy(g, slot):
        bb = g // npages
        return pltpu.make_async_copy(kc_ref.at[layer, pt_ref[bb, g - bb * npages]], buf_ref.at[slot], sem_ref.at[slot])

    @pl.when(b == 0)
    def _prime():
        for s in range(nbuf):
            page_copy(s, s).start()

    qcol = _lane_column(qt_ref[...], b)
    kcol = _lane_column(kt_ref[...], b)
    for h in range(A_HEADS):
        qb_ref[h] = jnp.broadcast_to(qcol[h], (dh, PAGE_SIZE))

    def page_body(p, _):
        g = b * npages + p
        slot = lax.rem(g, nbuf)
        page_copy(g, slot).wait()
        for h in range(A_HEADS):
            sc_ref[p, h:h + 1, :] = jnp.sum(buf_ref[slot, h] * qb_ref[h], axis=0, keepdims=True)

        @pl.when(g + nbuf < total)
        def _next():
            page_copy(g + nbuf, slot).start()

        return 0

    lax.fori_loop(0, npages, page_body, 0)
    _sample_choose_and_weigh(sc_ref, qcol, kcol, pn_ref, idx_ref, pown_ref, npages, nsel)


def _sample_choose_and_weigh(sc_ref, qcol, kcol, pn_ref, idx_ref, pown_ref, npages, nsel):
    nblk = npages // PAGES_PER_BLOCK
    lane = lax.broadcasted_iota(jnp.int32, (A_HEADS, LANES), 1).astype(F32)
    sub = lax.broadcasted_iota(jnp.int32, (A_HEADS, LANES), 0)

    gate = jnp.full((A_HEADS, LANES), NEG_INF, F32)
    for j in range(nblk):
        sblk = sc_ref[PAGES_PER_BLOCK * j]
        for pp in range(1, PAGES_PER_BLOCK):
            sblk = sblk + sc_ref[PAGES_PER_BLOCK * j + pp]
        gj = jnp.sum(sblk, axis=1, keepdims=True) * (1.0 / MOBA_BLOCK)
        gate = jnp.where(lane == float(j), gj, gate)

    g = gate
    sel = jnp.zeros((A_HEADS, LANES), F32)
    idx_out = jnp.full((A_HEADS, LANES), -1.0, F32)
    for r in range(nsel):
        m = jnp.max(g, axis=1, keepdims=True)
        idx = jnp.min(jnp.where(g == m, lane, float(LANES)), axis=1, keepdims=True)
        pick = jnp.logical_and(lane == idx, m > NEG_INF)
        sel = jnp.where(pick, 1.0, sel)
        g = jnp.where(pick, NEG_INF, g)
        idx_out = jnp.where(lane == float(r), jnp.where(m > NEG_INF, idx, -1.0), idx_out)
    idx_ref[0] = idx_out.astype(jnp.int32)

    s_own = jnp.zeros((A_HEADS, LANES), F32)
    for h in range(A_HEADS):
        so = jnp.sum(qcol[h] * kcol[h], axis=0, keepdims=True)
        s_own = jnp.where(sub == h, jnp.broadcast_to(so, (A_HEADS, LANES)), s_own)

    masks = [jnp.broadcast_to(sel[:, j:j + 1], (A_HEADS, LANES)) > 0.0 for j in range(nblk)]
    mx = s_own
    for pg in range(npages):
        mx = jnp.maximum(mx, jnp.where(masks[pg // PAGES_PER_BLOCK], sc_ref[pg], NEG_INF))
    m = jnp.max(mx, axis=1, keepdims=True)
    lsum = jnp.zeros((A_HEADS, LANES), F32)
    for pg in range(npages):
        p = jnp.where(masks[pg // PAGES_PER_BLOCK], jnp.exp(sc_ref[pg] - m), 0.0)
        pn_ref[0, pg] = p
        lsum = lsum + p
    p_own = jnp.exp(s_own - m)
    inv = 1.0 / (jnp.sum(lsum, axis=1, keepdims=True) + p_own)
    for pg in range(npages):
        pn_ref[0, pg] = pn_ref[0, pg] * inv
    pown_ref[0] = p_own * inv


def _sample_score(page_table, qt, kt, cache_t, layer, nbuf=SAMPLE_PAGES_IN_FLIGHT):
    bs, npages = page_table.shape
    nblk = npages // PAGES_PER_BLOCK
    nsel = min(MOBA_TOPK, nblk + 1)
    feat = pl.BlockSpec((A_HEADS, A_HEAD_DIM, bs), lambda b, pt: (0, 0, 0))
    grid_spec = pltpu.PrefetchScalarGridSpec(
        num_scalar_prefetch=1,
        grid=(bs,),
        in_specs=[feat, feat, pl.BlockSpec(memory_space=pl.ANY)],
        out_specs=(
            pl.BlockSpec((1, npages, A_HEADS, PAGE_SIZE), lambda b, pt: (b, 0, 0, 0)),
            pl.BlockSpec((1, A_HEADS, LANES), lambda b, pt: (b, 0, 0)),
            pl.BlockSpec((1, A_HEADS, LANES), lambda b, pt: (b, 0, 0)),
        ),
        scratch_shapes=[
            pltpu.VMEM((min(nbuf, npages), A_HEADS, A_HEAD_DIM, PAGE_SIZE), F32),
            pltpu.SemaphoreType.DMA((min(nbuf, npages),)),
            pltpu.VMEM((A_HEADS, A_HEAD_DIM, PAGE_SIZE), F32),
            pltpu.VMEM((npages, A_HEADS, PAGE_SIZE), F32),
        ],
    )
    return pl.pallas_call(
        functools.partial(_sample_score_kernel, layer=layer, npages=npages, nbuf=min(nbuf, npages), bs=bs, nsel=nsel),
        grid_spec=grid_spec,
        out_shape=(
            jax.ShapeDtypeStruct((bs, npages, A_HEADS, PAGE_SIZE), F32),
            jax.ShapeDtypeStruct((bs, A_HEADS, LANES), jnp.int32),
            jax.ShapeDtypeStruct((bs, A_HEADS, LANES), F32),
        ),
        compiler_params=pltpu.CompilerParams(dimension_semantics=("arbitrary",), vmem_limit_bytes=VMEM_LIMIT_BYTES),
        name="sample_score",
    )(page_table, qt, kt, cache_t)


def _sample_pv_kernel(pt_ref, idx_ref, pn_ref, pown_ref, vt_ref, sgt_ref, vc_ref, o_ref,
                      vbuf_ref, sem_ref, *, layer, nsel, bs):
    b = pl.program_id(0)
    slot = lax.rem(b, 2)

    def issue(bb, sl):
        for h in range(A_HEADS):
            for r in range(nsel):
                j = jnp.maximum(idx_ref[bb, h * nsel + r], 0)
                for pp in range(PAGES_PER_BLOCK):
                    pg = pt_ref[bb, PAGES_PER_BLOCK * j + pp]
                    pltpu.make_async_copy(vc_ref.at[layer, pg, h], vbuf_ref.at[sl, h, r, pp], sem_ref.at[sl]).start()

    @pl.when(b == 0)
    def _first():
        issue(0, 0)
        o_ref[...] = jnp.zeros_like(o_ref)

    @pl.when(b + 1 < bs)
    def _prefetch():
        issue(b + 1, 1 - slot)

    for _ in range(A_HEADS * nsel * PAGES_PER_BLOCK):
        pltpu.make_async_copy(vc_ref.at[layer, 0, 0], vbuf_ref.at[slot, 0, 0, 0], sem_ref.at[slot]).wait()

    _sample_apply_weights(b, idx_ref, pn_ref, pown_ref, vt_ref, sgt_ref, vbuf_ref.at[slot], o_ref, nsel, bs)


def _sample_apply_weights(b, idx_ref, pn_ref, pown_ref, vt_ref, sgt_ref, vpages_ref, o_ref, nsel, bs):
    vcol = _lane_column(vt_ref[...], b)
    sgcol = _lane_column(sgt_ref[...], b)
    mine = lax.broadcasted_iota(jnp.int32, (1, bs), 1) == b
    for h in range(A_HEADS):
        acc = jnp.zeros((A_HEAD_DIM, PAGE_SIZE), F32)
        for r in range(nsel):
            jraw = idx_ref[b, h * nsel + r]
            j = jnp.maximum(jraw, 0)
            w = jnp.where(jraw >= 0, 1.0, 0.0)
            for pp in range(PAGES_PER_BLOCK):
                prow = pn_ref[0, PAGES_PER_BLOCK * j + pp, h:h + 1, :] * w
                acc = acc + vpages_ref[h, r, pp] * prow
        ocol = jnp.sum(acc, axis=1, keepdims=True) + pown_ref[0, h:h + 1, 0:1] * vcol[h]
        o_ref[h] = jnp.where(mine, jnp.broadcast_to(ocol * sgcol[h], (A_HEAD_DIM, bs)), o_ref[h])


def _sample_pv(page_table, idx, pn, pown, vt, sgt, cache_t, layer):
    bs, npages = page_table.shape
    nsel = idx.shape[1] // A_HEADS
    feat = pl.BlockSpec((A_HEADS, A_HEAD_DIM, bs), lambda b, pt, ix: (0, 0, 0))
    grid_spec = pltpu.PrefetchScalarGridSpec(
        num_scalar_prefetch=2,
        grid=(bs,),
        in_specs=[
            pl.BlockSpec((1, npages, A_HEADS, PAGE_SIZE), lambda b, pt, ix: (b, 0, 0, 0)),
            pl.BlockSpec((1, A_HEADS, LANES), lambda b, pt, ix: (b, 0, 0)),
            feat, feat,
            pl.BlockSpec(memory_space=pl.ANY),
        ],
        out_specs=feat,
        scratch_shapes=[
            pltpu.VMEM((2, A_HEADS, nsel, PAGES_PER_BLOCK, A_HEAD_DIM, PAGE_SIZE), F32),
            pltpu.SemaphoreType.DMA((2,)),
        ],
    )
    return pl.pallas_call(
        functools.partial(_sample_pv_kernel, layer=layer, nsel=nsel, bs=bs),
        grid_spec=grid_spec,
        out_shape=jax.ShapeDtypeStruct((A_HEADS, A_HEAD_DIM, bs), F32),
        compiler_params=pltpu.CompilerParams(dimension_semantics=("arbitrary",), vmem_limit_bytes=VMEM_LIMIT_BYTES),
        name="sample_pv",
    )(page_table, idx, pn, pown, vt, sgt, cache_t)


def _gla_sample_kernel(qt_ref, kt_ref, gt_ref, v_ref, sg_ref, s0_ref, gn_ref, s_ref, o_ref):
    b = pl.program_id(0)
    dv = G_VAL_DIM
    bs = v_ref.shape[0]

    @pl.when(b == 0)
    def _init():
        o_ref[...] = jnp.zeros_like(o_ref)

    qcol = _lane_column(qt_ref[...], b)
    kcol = _lane_column(kt_ref[...], b)
    gcol = _lane_column(gt_ref[...], b)
    mine = lax.broadcasted_iota(jnp.int32, (bs, dv), 0) == b
    for h in range(G_HEADS):
        cols = slice(h * dv, (h + 1) * dv)
        v = jnp.sum(jnp.where(mine, v_ref[:, cols], 0.0), axis=0, keepdims=True)
        sg = jnp.sum(jnp.where(mine, sg_ref[:, cols], 0.0), axis=0, keepdims=True)
        s_new = jnp.exp(gcol[h]) * s0_ref[0, h] + kcol[h] * v
        s_ref[0, h] = s_new
        o = jnp.sum(qcol[h] * s_new, axis=0, keepdims=True) * (G_KEY_DIM ** -0.5)
        ms = jnp.mean(o * o, axis=-1, keepdims=True)
        on = o * lax.rsqrt(ms + NORM_EPS) * gn_ref[...] * sg
        o_ref[:, cols] = jnp.where(mine, jnp.broadcast_to(on, (bs, dv)), o_ref[:, cols])


def _gla_sample(qt, kt, gt, v, sg, s0, gn):
    bs = s0.shape[0]
    feat = pl.BlockSpec((G_HEADS, G_KEY_DIM, bs), lambda b: (0, 0, 0))
    tok = pl.BlockSpec((bs, G_WIDTH), lambda b: (0, 0))
    sspec = pl.BlockSpec((1, G_HEADS, G_KEY_DIM, G_VAL_DIM), lambda b: (b, 0, 0, 0))
    return pl.pallas_call(
        _gla_sample_kernel,
        grid=(bs,),
        in_specs=[feat, feat, feat, tok, tok, sspec, pl.BlockSpec((1, G_VAL_DIM), lambda b: (0, 0))],
        out_specs=(sspec, tok),
        out_shape=(jax.ShapeDtypeStruct(s0.shape, F32), jax.ShapeDtypeStruct((bs, G_WIDTH), F32)),
        compiler_params=pltpu.CompilerParams(dimension_semantics=("arbitrary",)),
        name="gla_sample",
    )(qt, kt, gt, v, sg, s0, gn)


def _rope_tables(pos):
    inv = jnp.power(jnp.float32(ROPE_THETA), -jnp.arange(ROT_HALF, dtype=F32) * (2.0 / ROT_DIM))
    ang = pos.astype(F32)[:, None] * inv[None, :]
    return jnp.cos(ang).T, jnp.sin(ang).T


def _pick_tile(n, pref):
    t = min(n, pref)
    while n % t:
        t //= 2
    return t


def kernel(x_prompt, x_sample, cache_k, cache_v, state_gla, page_table, norm_g, w_in, w_a2, b_a2, gla_norm_g, w_out, final_norm_g):
    bp, lp, d = x_prompt.shape
    bs, ls, _ = x_sample.shape
    depth = norm_g.shape[0]
    past_len = page_table.shape[1] * PAGE_SIZE
    assert ls == 1 and lp % MOBA_BLOCK == 0 and past_len % MOBA_BLOCK == 0
    assert cache_k.shape[2:] == (PAGE_SIZE, A_HEADS, A_HEAD_DIM)

    tabs_p = _rope_tables(jnp.arange(lp, dtype=jnp.int32))
    tabs_s = _rope_tables(jnp.full((bs,), past_len, dtype=jnp.int32))
    kc_t = jnp.transpose(cache_k, (0, 1, 3, 4, 2))
    vc_t = jnp.transpose(cache_v, (0, 1, 3, 4, 2))

    tm_p = _pick_tile(lp, 512)
    tt = _pick_tile(lp, 256)
    hp = x_prompt.reshape(bp * lp, d)
    hs = x_sample.reshape(bs, d)
    fg = final_norm_g.reshape(1, d)
    sp_l, ks_l, vs_l, ss_l = [], [], [], []
    kv_prompt = (jnp.zeros((depth, bp, A_WIDTH, lp), F32), jnp.zeros((depth, bp, A_WIDTH, lp), F32))
    rows = A_WIDTH
    g_lo = 4 * rows
    g_hi = g_lo + 2 * G_KEY_WIDTH + 2 * G_WIDTH
    for l in range(depth):
        wt = jnp.transpose(w_in[l])
        w_t = wt[0:4 * rows].astype(BF16)
        a1_t = jnp.pad(wt[g_hi:], ((0, LANES - G_GATE_RANK), (0, 0)))
        w_row = jnp.transpose(jnp.concatenate([wt[g_lo:g_hi], a1_t], axis=0)).astype(BF16)
        wa2 = jnp.pad(w_a2[l], ((0, LANES - G_GATE_RANK), (0, 0))).astype(BF16)
        ba2 = b_a2[l].reshape(1, G_KEY_WIDTH)
        ng = norm_g[l].reshape(1, d)
        gn = gla_norm_g[l].reshape(1, G_VAL_DIM)
        wo = w_out[l].astype(BF16)
        last = l == depth - 1

        qt_s, kt_s, vt_s, sgt_s, qg_s, kg_s, vg_s, sgg_s, lf_s = _proj_in(
            hs, ng, w_row, w_t, wa2, ba2, tabs_s, 1, bs, 0, 1)
        a_feat = lambda a: a.reshape(A_HEADS, A_HEAD_DIM, bs)

        qt, kt_all, vt_all, sgt, qg, kg, vg, sgg, lf = _proj_in(
            hp, ng, w_row, w_t, wa2, ba2, tabs_p, bp, tm_p, l, depth, kv_prompt)
        kv_prompt = (kt_all, vt_all)
        moba_steps = bp * (A_WIDTH // (MOBA_HEADS_PER_STEP * A_HEAD_DIM)) * (lp // MOBA_BLOCK + 1)
        pps = _score_pages_per_step(page_table.shape[1], bs, moba_steps)
        if pps is None:
            oa = _moba_prompt(qt, kt_all, vt_all, sgt, l)
            pn, idx, pown = _sample_score(page_table, a_feat(qt_s), a_feat(kt_s), kc_t, l)
        else:
            oa, pn, idx, pown = _moba_prompt(
                qt, kt_all, vt_all, sgt, l, (page_table, a_feat(qt_s), a_feat(kt_s), kc_t, pps))
        nsel = min(MOBA_TOPK, page_table.shape[1] // PAGES_PER_BLOCK + 1)
        idx_s = idx[:, :, :nsel].reshape(bs, A_HEADS * nsel)
        gla_args = (qg.reshape(bp, lp, -1), kg.reshape(bp, lp, -1), vg.reshape(bp, lp, -1),
                    lf.reshape(bp, lp, -1), sgg.reshape(bp, lp, -1), gn, tt)
        if 2 * bs <= bp * (lp // tt):
            og, s_fin, oa_s = _gla_prompt(
                *gla_args, (page_table, idx_s, pn, pown, a_feat(vt_s), a_feat(sgt_s), vc_t, l))
        else:
            og, s_fin = _gla_prompt(*gla_args)
            oa_s = _sample_pv(page_table, idx_s, pn, pown, a_feat(vt_s), a_feat(sgt_s), vc_t, l)
        hp = _proj_out(oa.reshape(bp * lp, A_WIDTH), og.reshape(bp * lp, G_WIDTH), hp, wo, fg, tm_p, last)
        sp_l.append(s_fin)

        qg, kg, vg, sgg, lf = qg_s, kg_s, vg_s, sgg_s, lf_s
        g_feat = lambda a: jnp.transpose(a.astype(F32)).reshape(G_HEADS, G_KEY_DIM, bs)
        s_new, og_s = _gla_sample(g_feat(qg), g_feat(kg), g_feat(lf), vg.astype(F32), sgg.astype(F32),
                                  state_gla[l], gn)
        hs = _proj_out(oa_s.reshape(A_WIDTH, bs), og_s, hs, wo, fg, bs, last, a_feature_major=True)
        ks_l.append(jnp.transpose(kt_s[0, 0]).reshape(bs, 1, A_HEADS, A_HEAD_DIM))
        vs_l.append(jnp.transpose(vt_s[0, 0]).reshape(bs, 1, A_HEADS, A_HEAD_DIM))
        ss_l.append(s_new)

    kt_all, vt_all = kv_prompt
    to_cache_layout = lambda a: jnp.transpose(a.reshape(depth, bp, A_HEADS, A_HEAD_DIM, lp), (0, 1, 4, 2, 3))
    return (hp.reshape(bp, lp, d), hs.reshape(bs, ls, d), to_cache_layout(kt_all), to_cache_layout(vt_all),
            jnp.stack(sp_l), jnp.stack(ks_l), jnp.stack(vs_l), jnp.stack(ss_l))
```

```python
import functools

import jax
import jax.numpy as jnp
from jax import lax
from jax.experimental import pallas as pl
from jax.experimental.pallas import tpu as pltpu

A_HEADS = 8
A_HEAD_DIM = 64
A_WIDTH = A_HEADS * A_HEAD_DIM
ROT_DIM = A_HEAD_DIM // 4
ROT_HALF = ROT_DIM // 2
ROPE_THETA = 500000.0
MOBA_BLOCK = 256
MOBA_TOPK = 3
G_HEADS = 4
G_KEY_DIM = 64
G_VAL_DIM = 128
G_KEY_WIDTH = G_HEADS * G_KEY_DIM
G_WIDTH = G_HEADS * G_VAL_DIM
G_GATE_RANK = 16
G_GATE_NORM = 16.0
G_CHUNK = 32
NORM_EPS = 1e-6
PAGE_SIZE = 128
PAGES_PER_BLOCK = MOBA_BLOCK // PAGE_SIZE

LANES = 128
VMEM_LIMIT_BYTES = 56 * 1024 * 1024

F32 = jnp.float32
BF16 = jnp.bfloat16
NEG_INF = float("-inf")
MASK_BIAS = -1e30
MOBA_HEADS_PER_STEP = 4
MOBA_BLOCKS_PER_ITERATION = 4
SAMPLE_PAGES_IN_FLIGHT = 24
V_PAD_ROWS = 16

_NT = (((1,), (1,)), ((), ()))
_TN = (((0,), (0,)), ((), ()))


def _dot(a, b):
    return jnp.dot(a, b, preferred_element_type=F32)


def _dot_nt(a, b):
    return lax.dot_general(a, b, _NT, preferred_element_type=F32)


def _dot_tn(a, b):
    return lax.dot_general(a, b, _TN, preferred_element_type=F32)


def _split3(x):
    x1 = x.astype(BF16)
    r1 = x - x1.astype(F32)
    x2 = r1.astype(BF16)
    x3 = (r1 - x2.astype(F32)).astype(BF16)
    return x1, x2, x3


def _lane_column(x, lane):
    pick = lax.broadcasted_iota(jnp.int32, (1,) * (x.ndim - 1) + (x.shape[-1],), x.ndim - 1) == lane
    return jnp.sum(jnp.where(pick, x.astype(F32), 0.0), axis=-1, keepdims=True)


def _proj_in_kernel(*refs, n_alias, layer, depth):
    x_ref, g_ref, w_ref, wt_ref, wa2_ref, ba2_ref, ct_ref, st_ref = refs[:8]
    qt_ref, kt_ref, vt_ref, sgt_ref, qg_ref, kg_ref, vg_ref, sgg_ref, lf_ref = refs[8 + n_alias:]
    if n_alias == 0:
        for other in range(depth):
            if other != layer:
                kt_ref[other] = jnp.zeros(kt_ref.shape[1:], kt_ref.dtype)
                vt_ref[other] = jnp.zeros(vt_ref.shape[1:], vt_ref.dtype)
        kt_ref, vt_ref = kt_ref.at[layer], vt_ref.at[layer]
    x = x_ref[...]
    ms = jnp.mean(x * x, axis=-1, keepdims=True)
    h = (x * lax.rsqrt(ms + NORM_EPS) * g_ref[...]).astype(BF16)

    def proj(lo, hi):
        return _dot(h, w_ref[:, lo:hi])

    o = 0
    qg_ref[...] = proj(o, o + G_KEY_WIDTH).astype(qg_ref.dtype)
    o += G_KEY_WIDTH
    kg_ref[...] = proj(o, o + G_KEY_WIDTH).astype(kg_ref.dtype)
    o += G_KEY_WIDTH
    vg_ref[...] = proj(o, o + G_WIDTH).astype(vg_ref.dtype)
    o += G_WIDTH
    zg = proj(o, o + G_WIDTH)
    sgg_ref[...] = (zg * jax.nn.sigmoid(zg)).astype(sgg_ref.dtype)
    o += G_WIDTH
    a1 = proj(o, o + LANES)
    la = _dot(a1.astype(BF16), wa2_ref[...]) + ba2_ref[...]
    lf_ref[...] = -(jnp.maximum(-la, 0.0) + jnp.log1p(jnp.exp(-jnp.abs(la)))) * (1.0 / G_GATE_NORM)

    zt = _dot_nt(wt_ref[...], h)
    ct, st = ct_ref[...], st_ref[...]
    q_scale = A_HEAD_DIM ** -0.5
    for hd in range(A_HEADS):
        b0 = hd * A_HEAD_DIM
        x1 = zt[b0:b0 + ROT_HALF]
        x2 = zt[b0 + ROT_HALF:b0 + ROT_DIM]
        qh = jnp.concatenate([x1 * ct - x2 * st, x2 * ct + x1 * st, zt[b0 + ROT_DIM:b0 + A_HEAD_DIM]], axis=0)
        qt_ref[b0:b0 + A_HEAD_DIM, :] = (qh * q_scale).astype(qt_ref.dtype)
        k0 = b0 + A_WIDTH
        x1 = zt[k0:k0 + ROT_HALF]
        x2 = zt[k0 + ROT_HALF:k0 + ROT_DIM]
        kt_ref[b0:b0 + ROT_HALF, :] = x1 * ct - x2 * st
        kt_ref[b0 + ROT_HALF:b0 + ROT_DIM, :] = x2 * ct + x1 * st
        kt_ref[b0 + ROT_DIM:b0 + A_HEAD_DIM, :] = zt[k0 + ROT_DIM:k0 + A_HEAD_DIM]
    vt_ref[...] = zt[2 * A_WIDTH:3 * A_WIDTH]
    zg = zt[3 * A_WIDTH:4 * A_WIDTH]
    sgt_ref[...] = (zg * jax.nn.sigmoid(zg)).astype(sgt_ref.dtype)


def _proj_in(x, norm_g, w_row, w_t, wa2, ba2, tabs, n_batch, tm, layer, depth, kv_prev=None):
    n, d = x.shape
    seq = n // n_batch
    nl = seq // tm
    ct_tab, st_tab = tabs
    row = lambda i: (i, 0)
    const = lambda i: (0, 0)
    feat_shape = jax.ShapeDtypeStruct((n_batch, A_WIDTH, seq), BF16)
    kv_shape = jax.ShapeDtypeStruct((depth, n_batch, A_WIDTH, seq), F32)
    out_shapes = (
        feat_shape,
        kv_shape, kv_shape,
        feat_shape,
        jax.ShapeDtypeStruct((n, G_KEY_WIDTH), BF16),
        jax.ShapeDtypeStruct((n, G_KEY_WIDTH), BF16),
        jax.ShapeDtypeStruct((n, G_WIDTH), BF16),
        jax.ShapeDtypeStruct((n, G_WIDTH), BF16),
        jax.ShapeDtypeStruct((n, G_KEY_WIDTH), F32),
    )
    feat_spec = pl.BlockSpec((None, A_WIDTH, tm), lambda i: (i // nl, 0, i % nl))
    if kv_prev is None:
        kv_spec = pl.BlockSpec((depth, None, A_WIDTH, tm), lambda i: (0, i // nl, 0, i % nl))
    else:
        kv_spec = pl.BlockSpec((None, None, A_WIDTH, tm), lambda i: (layer, i // nl, 0, i % nl))
    in_specs = [
        pl.BlockSpec((tm, d), row),
        pl.BlockSpec((1, d), const),
        pl.BlockSpec(w_row.shape, const),
        pl.BlockSpec(w_t.shape, const),
        pl.BlockSpec((LANES, G_KEY_WIDTH), const),
        pl.BlockSpec((1, G_KEY_WIDTH), const),
        pl.BlockSpec((ROT_HALF, tm), lambda i: (0, i % nl)),
        pl.BlockSpec((ROT_HALF, tm), lambda i: (0, i % nl)),
    ]
    args = [x, norm_g, w_row, w_t, wa2, ba2, ct_tab, st_tab]
    aliases = {}
    if kv_prev is not None:
        in_specs += [pl.BlockSpec(memory_space=pl.ANY)] * 2
        aliases = {len(args): 1, len(args) + 1: 2}
        args += list(kv_prev)
    return pl.pallas_call(
        functools.partial(_proj_in_kernel, n_alias=len(aliases), layer=layer, depth=depth),
        grid=(n // tm,),
        in_specs=in_specs,
        out_specs=(
            feat_spec, kv_spec, kv_spec, feat_spec,
            pl.BlockSpec((tm, G_KEY_WIDTH), row), pl.BlockSpec((tm, G_KEY_WIDTH), row),
            pl.BlockSpec((tm, G_WIDTH), row), pl.BlockSpec((tm, G_WIDTH), row),
            pl.BlockSpec((tm, G_KEY_WIDTH), row),
        ),
        out_shape=out_shapes,
        input_output_aliases=aliases,
        compiler_params=pltpu.CompilerParams(dimension_semantics=("arbitrary",), vmem_limit_bytes=VMEM_LIMIT_BYTES),
        name="proj_in",
    )(*args)


def _proj_out_kernel(a_ref, g_ref, x_ref, w_ref, fg_ref, o_ref, *, final, a_feature_major):
    half = g_ref.shape[1]
    a = a_ref[...].astype(BF16)
    ya = _dot_tn(a, w_ref[0:half, :]) if a_feature_major else _dot(a, w_ref[0:half, :])
    xo = x_ref[...] + (ya + _dot(g_ref[...].astype(BF16), w_ref[half:, :]))
    if final:
        ms = jnp.mean(xo * xo, axis=-1, keepdims=True)
        xo = xo * lax.rsqrt(ms + NORM_EPS) * fg_ref[...]
    o_ref[...] = xo


def _proj_out(a, g, x, w, fg, tm, final, a_feature_major=False):
    n, d = x.shape
    row = lambda i: (i, 0)
    const = lambda i: (0, 0)
    a_spec = pl.BlockSpec((a.shape[0], tm), lambda i: (0, i)) if a_feature_major else pl.BlockSpec((tm, a.shape[1]), row)
    return pl.pallas_call(
        functools.partial(_proj_out_kernel, final=final, a_feature_major=a_feature_major),
        grid=(n // tm,),
        in_specs=[
            a_spec,
            pl.BlockSpec((tm, g.shape[1]), row),
            pl.BlockSpec((tm, d), row),
            pl.BlockSpec(w.shape, const),
            pl.BlockSpec((1, d), const),
        ],
        out_specs=pl.BlockSpec((tm, d), row),
        out_shape=jax.ShapeDtypeStruct((n, d), F32),
        compiler_params=pltpu.CompilerParams(dimension_semantics=("arbitrary",), vmem_limit_bytes=VMEM_LIMIT_BYTES),
        name="proj_out",
    )(a, g, x, w, fg)


def _moba_prompt_kernel(*refs, nb, nsel, unroll, side):
    if side is None:
        qt_ref, kt_ref, vt_ref, sgt_ref, o_ref, ka_ref, va_ref, km_ref, qa_ref, s_ref, acc_ref, m_ref = refs
    else:
        (pt_ref, qt_ref, kt_ref, vt_ref, sgt_ref, qs_ref, ks_ref, kc_ref, o_ref, pn_ref, idx_ref, pown_ref,
         ka_ref, va_ref, km_ref, qa_ref, s_ref, acc_ref, m_ref, buf_ref, sem_ref, qb_ref, sc_ref) = refs
    qi = pl.program_id(2)
    blk = MOBA_BLOCK
    dh = A_HEAD_DIM
    heads = qt_ref.shape[0] // dh
    per_group = LANES // dh

    if side is not None:
        layer, npages, bs, pps, nsel_s = side
        sps = npages // pps
        step = (pl.program_id(0) * pl.num_programs(1) + pl.program_id(1)) * pl.num_programs(2) + qi
        seq = step // sps
        part = step - seq * sps
        total = bs * npages
        half = lax.rem(step, 2) * pps

        def page_copy(g, slot):
            bb = g // npages
            return pltpu.make_async_copy(
                kc_ref.at[layer, pt_ref[bb, g - bb * npages]], buf_ref.at[slot], sem_ref.at[slot])

        @pl.when(step == 0)
        def _prime():
            for i in range(2 * pps):
                page_copy(i, i).start()

        @pl.when(seq < bs)
        def _score_pages():
            @pl.when(part == 0)
            def _new_sequence():
                qcol = _lane_column(qs_ref[...], seq)
                for h in range(A_HEADS):
                    qb_ref[h] = jnp.broadcast_to(qcol[h], (dh, PAGE_SIZE))

            g0 = step * pps
            for i in range(pps):
                page_copy(g0 + i, half + i).wait()
            for h in range(A_HEADS):
                qb = qb_ref[h]
                for i in range(pps):
                    sc_ref[part * pps + i, h:h + 1, :] = jnp.sum(buf_ref[half + i, h] * qb, axis=0, keepdims=True)
            for i in range(pps):
                @pl.when(g0 + i + 2 * pps < total)
                def _refill():
                    page_copy(g0 + i + 2 * pps, half + i).start()

            @pl.when(part == sps - 1)
            def _finish_sequence():
                _sample_choose_and_weigh(sc_ref, _lane_column(qs_ref[...], seq), _lane_column(ks_ref[...], seq),
                                         pn_ref, idx_ref, pown_ref, npages, nsel_s)

    @pl.when(qi == 0)
    def _prepare():
        lane = lax.broadcasted_iota(jnp.int32, (blk, LANES), 1)
        ones_rows = jnp.where(lax.broadcasted_iota(jnp.int32, (V_PAD_ROWS, blk), 0) == 0, 1.0, 0.0)
        for j in range(nb):
            kj = kt_ref[:, j * blk:(j + 1) * blk].T
            km_ref[j:j + 1, :] = jnp.mean(kj, axis=0, keepdims=True)
            onehot = jnp.where(lane == dh + j, 1.0, 0.0)
            for hh in range(heads):
                grp, sub = divmod(hh, per_group)
                kh = kj[:, grp * LANES:(grp + 1) * LANES]
                if sub:
                    kh = pltpu.roll(kh, LANES - sub * dh, 1)
                ka_ref[hh, j * blk:(j + 1) * blk, :] = jnp.where(lane < dh, kh, onehot).astype(BF16)
                vh = vt_ref[hh * dh:(hh + 1) * dh, j * blk:(j + 1) * blk]
                va_ref[j, hh] = jnp.concatenate([vh, ones_rows], axis=0).astype(BF16)
        m_ref[...] = jnp.zeros_like(m_ref)

    @pl.when(jnp.logical_and(qi == 0, jnp.logical_and(pl.program_id(0) == 0, pl.program_id(1) == 0)))
    def _define_scores():
        s_ref[...] = jnp.zeros_like(s_ref)

    qb = jnp.minimum(qi, nb - 1)
    prev = jnp.maximum(qi - 1, 0)
    m_old = [m_ref[hh] for hh in range(heads)]

    for hh in range(heads):
        acc_ref[hh] = _dot(va_ref[prev, hh], jnp.exp(s_ref[hh, nb] - m_old[hh]).astype(BF16))

    rowf = lax.broadcasted_iota(jnp.int32, (nb, blk), 0).astype(F32)
    past = rowf < qb.astype(F32)
    causal = lax.broadcasted_iota(jnp.int32, (blk, blk), 0) <= lax.broadcasted_iota(jnp.int32, (blk, blk), 1)
    pad = jnp.zeros((LANES - dh - nb, blk), BF16)
    r_own = pl.multiple_of(qb * blk, blk)

    m_run, qd = [], []
    for hh in range(heads):
        qh = qt_ref[hh * dh:(hh + 1) * dh, :]
        km = km_ref[:, hh * dh:(hh + 1) * dh]
        km1 = km.astype(BF16)
        km2 = (km - km1.astype(F32)).astype(BF16)
        g = jnp.where(past, _dot(km1, qh) + _dot(km2, qh), NEG_INF)
        sel = jnp.zeros((nb, blk), F32)
        for _ in range(nsel):
            m = jnp.max(g, axis=0, keepdims=True)
            idx = jnp.min(jnp.where(g == m, rowf, float(nb)), axis=0, keepdims=True)
            pick = jnp.logical_and(rowf == idx, m > NEG_INF)
            sel = jnp.where(pick, 1.0, sel)
            g = jnp.where(pick, NEG_INF, g)
        bias = jnp.where(sel > 0.0, 0.0, MASK_BIAS).astype(BF16)
        qa_ref[hh] = jnp.concatenate([qh, bias, pad], axis=0)
        qd.append(jnp.concatenate([qh, jnp.zeros((LANES - dh, blk), BF16)], axis=0))

    for hh in range(heads):
        s = jnp.where(causal, _dot(ka_ref[hh, pl.ds(r_own, blk), :], qd[hh]), MASK_BIAS)
        s_ref[hh, nb] = s
        m_run.append(jnp.max(s, axis=0, keepdims=True))

    def blocks(j0, count, ms):
        ms = list(ms)
        for hh in range(heads):
            part = None
            for u in range(count):
                j = j0 + u
                p = jnp.where(j < prev, jnp.exp(s_ref[hh, j] - m_old[hh]), 0.0).astype(BF16)
                t = _dot(va_ref[j, hh], p)
                part = t if part is None else part + t
                r0 = pl.multiple_of(j * blk, blk)
                s = _dot(ka_ref[hh, pl.ds(r0, blk), :], qa_ref[hh])
                s_ref[hh, j] = s
                ms[hh] = jnp.maximum(ms[hh], jnp.max(s, axis=0, keepdims=True))
            acc_ref[hh] += part
        return tuple(ms)

    m_fin = tuple(m_run)
    done = 0
    width = unroll
    while width >= 1:
        trips = lax.shift_right_logical(qi - done, width.bit_length() - 1)
        m_fin = lax.fori_loop(0, trips, functools.partial(
            lambda gi, ms, j0, w: blocks(j0 + gi * w, w, ms), j0=done, w=width), m_fin)
        done = done + trips * width
        width //= 2
    for hh in range(heads):
        m_ref[hh] = m_fin[hh]

    ot = jnp.concatenate([acc_ref[hh, 0:dh, :] / acc_ref[hh, dh:dh + 1, :] for hh in range(heads)], axis=0)
    o_ref[...] = (ot * sgt_ref[...].astype(F32)).T.astype(o_ref.dtype)


def _score_pages_per_step(npages, bs, steps):
    for pps in range(1, npages + 1):
        if npages % pps == 0 and bs * (npages // pps) <= steps:
            return pps
    return None


def _moba_prompt(qt, kt_all, vt_all, sgt, layer, decode=None):
    b, width, seq = qt.shape
    nb = seq // MOBA_BLOCK
    nsel = min(MOBA_TOPK, nb)
    heads = MOBA_HEADS_PER_STEP
    hw = heads * A_HEAD_DIM
    assert A_HEAD_DIM + nb <= LANES and width % hw == 0
    unroll = MOBA_BLOCKS_PER_ITERATION
    n_hp = width // hw
    grid = (b, n_hp, nb + 1)
    scratch = [
        pltpu.VMEM((heads, seq, LANES), BF16),
        pltpu.VMEM((nb, heads, A_HEAD_DIM + V_PAD_ROWS, MOBA_BLOCK), BF16),
        pltpu.VMEM((nb, hw), F32),
        pltpu.VMEM((heads, LANES, MOBA_BLOCK), BF16),
        pltpu.VMEM((heads, nb + 1, MOBA_BLOCK, MOBA_BLOCK), F32),
        pltpu.VMEM((heads, A_HEAD_DIM + V_PAD_ROWS, MOBA_BLOCK), F32),
        pltpu.VMEM((heads, 1, MOBA_BLOCK), F32),
    ]
    params = pltpu.CompilerParams(
        dimension_semantics=("arbitrary", "arbitrary", "arbitrary"), vmem_limit_bytes=VMEM_LIMIT_BYTES)
    q_map = lambda bi, hp, qi, *_: (bi, hp, jnp.minimum(qi, nb - 1))
    g_map = lambda bi, hp, qi, *_: (bi, hp, jnp.maximum(qi - 1, 0))
    kv_map = lambda bi, hp, qi, *_: (layer, bi, hp, 0)
    o_map = lambda bi, hp, qi, *_: (bi, jnp.maximum(qi - 1, 0), hp)
    in_specs = [pl.BlockSpec((None, hw, MOBA_BLOCK), q_map), pl.BlockSpec((None, None, hw, seq), kv_map),
                pl.BlockSpec((None, None, hw, seq), kv_map), pl.BlockSpec((None, hw, MOBA_BLOCK), g_map)]
    o_spec = pl.BlockSpec((None, MOBA_BLOCK, hw), o_map)
    o_shape = jax.ShapeDtypeStruct((b, seq, width), BF16)
    if decode is None:
        return pl.pallas_call(
            functools.partial(_moba_prompt_kernel, nb=nb, nsel=nsel, unroll=unroll, side=None),
            grid=grid, in_specs=in_specs, out_specs=o_spec, out_shape=o_shape, scratch_shapes=scratch,
            compiler_params=params, name="moba_prompt",
        )(qt, kt_all, vt_all, sgt)

    page_table, q_s, k_s, cache_t, pps = decode
    bs, npages = page_table.shape
    sps = npages // pps
    nsel_s = min(MOBA_TOPK, npages // PAGES_PER_BLOCK + 1)
    seq_of = lambda bi, hp, qi: jnp.minimum(((bi * n_hp + hp) * (nb + 1) + qi) // sps, bs - 1)
    feat = pl.BlockSpec((A_HEADS, A_HEAD_DIM, bs), lambda bi, hp, qi, pt: (0, 0, 0))
    grid_spec = pltpu.PrefetchScalarGridSpec(
        num_scalar_prefetch=1,
        grid=grid,
        in_specs=in_specs + [feat, feat, pl.BlockSpec(memory_space=pl.ANY)],
        out_specs=(
            o_spec,
            pl.BlockSpec((1, npages, A_HEADS, PAGE_SIZE), lambda bi, hp, qi, pt: (seq_of(bi, hp, qi), 0, 0, 0)),
            pl.BlockSpec((1, A_HEADS, LANES), lambda bi, hp, qi, pt: (seq_of(bi, hp, qi), 0, 0)),
            pl.BlockSpec((1, A_HEADS, LANES), lambda bi, hp, qi, pt: (seq_of(bi, hp, qi), 0, 0)),
        ),
        scratch_shapes=scratch + [
            pltpu.VMEM((2 * pps, A_HEADS, A_HEAD_DIM, PAGE_SIZE), F32),
            pltpu.SemaphoreType.DMA((2 * pps,)),
            pltpu.VMEM((A_HEADS, A_HEAD_DIM, PAGE_SIZE), F32),
            pltpu.VMEM((npages, A_HEADS, PAGE_SIZE), F32),
        ],
    )
    return pl.pallas_call(
        functools.partial(_moba_prompt_kernel, nb=nb, nsel=nsel, unroll=unroll,
                          side=(layer, npages, bs, pps, nsel_s)),
        grid_spec=grid_spec,
        out_shape=(
            o_shape,
            jax.ShapeDtypeStruct((bs, npages, A_HEADS, PAGE_SIZE), F32),
            jax.ShapeDtypeStruct((bs, A_HEADS, LANES), jnp.int32),
            jax.ShapeDtypeStruct((bs, A_HEADS, LANES), F32),
        ),
        compiler_params=params,
        name="moba_prompt",
    )(page_table, qt, kt_all, vt_all, sgt, q_s, k_s, cache_t)


def _gla_prompt_kernel(q_ref, k_ref, v_ref, lf_ref, sg_ref, gn_ref, o_ref, s_ref, st_ref, *, nt):
    t = pl.program_id(1)
    tt = q_ref.shape[1]
    ck = G_CHUNK
    dk, dv = G_KEY_DIM, G_VAL_DIM
    nh = G_HEADS

    @pl.when(t == 0)
    def _init():
        st_ref[...] = jnp.zeros_like(st_ref)

    g1, g2, g3 = _split3(lf_ref[0])
    row = lax.broadcasted_iota(jnp.int32, (tt, tt), 0)
    col = lax.broadcasted_iota(jnp.int32, (tt, tt), 1)
    ck_shift = ck.bit_length() - 1
    dk_shift = dk.bit_length() - 1
    same = lax.shift_right_logical(row, ck_shift) == lax.shift_right_logical(col, ck_shift)
    lower = jnp.where(jnp.logical_and(same, col <= row), 1.0, 0.0).astype(BF16)
    upper = jnp.where(jnp.logical_and(same, col > row), 1.0, 0.0).astype(BF16)
    bcum = _dot(lower, g1) + _dot(lower, g2) + _dot(lower, g3)
    rest = _dot(upper, g1) + _dot(upper, g2) + _dot(upper, g3)

    qf = q_ref[0].astype(F32)
    kf = k_ref[0].astype(F32)
    qt = qf * jnp.exp(bcum) * (dk ** -0.5)
    kt = kf * jnp.exp(-bcum)
    kd = kf * jnp.exp(rest)

    head_of_lane = lax.shift_right_logical(lax.broadcasted_iota(jnp.int32, (ck, nh * dk), 1), dk_shift)
    own = [head_of_lane == hd for hd in range(nh)]
    ri = lax.broadcasted_iota(jnp.int32, (nh * ck, nh * ck), 0)
    ci = lax.broadcasted_iota(jnp.int32, (nh * ck, nh * ck), 1)
    causal = jnp.bitwise_and(ri, ck - 1) >= jnp.bitwise_and(ci, ck - 1)
    gn = gn_ref[...]
    st = st_ref[...]

    def stack(x):
        return jnp.concatenate([jnp.where(own[hd], x, 0.0) for hd in range(nh)], axis=0).astype(BF16)

    nc = tt // ck
    q4s, o_intra, kvs = [], [], []
    for c in range(nc):
        r0, r1 = c * ck, (c + 1) * ck
        q4, k4, kd4 = stack(qt[r0:r1]), stack(kt[r0:r1]), stack(kd[r0:r1])
        v4 = jnp.concatenate([v_ref[0, r0:r1, hd * dv:(hd + 1) * dv] for hd in range(nh)], axis=0)
        a = jnp.where(causal, _dot_nt(q4, k4), 0.0)
        q4s.append(q4)
        o_intra.append(_dot(a.astype(BF16), v4))
        kvs.append(_dot_tn(v4, kd4))

    for c in range(nc):
        r0, r1 = c * ck, (c + 1) * ck
        o = o_intra[c] + _dot_nt(q4s[c], st.astype(BF16))
        ms = jnp.mean(o * o, axis=-1, keepdims=True)
        on = o * lax.rsqrt(ms + NORM_EPS) * gn
        for hd in range(nh):
            sg = sg_ref[0, r0:r1, hd * dv:(hd + 1) * dv].astype(F32)
            o_ref[0, r0:r1, hd * dv:(hd + 1) * dv] = (on[hd * ck:(hd + 1) * ck] * sg).astype(o_ref.dtype)
        st = st * jnp.exp(bcum[r1 - 1:r1, :]) + kvs[c]

    st_ref[...] = st

    @pl.when(t == nt - 1)
    def _emit():
        per_group = LANES // dk
        for grp in range(nh // per_group):
            s_grp = st[:, grp * LANES:(grp + 1) * LANES].T
            for hh in range(per_group):
                s_ref[0, grp * per_group + hh] = s_grp[hh * dk:(hh + 1) * dk, :]


def _gla_prompt(qg, kg, vg, lf, sg, gn, tt):
    b, seq, _ = qg.shape
    nt = seq // tt
    kspec = pl.BlockSpec((1, tt, G_KEY_WIDTH), lambda bi, ti: (bi, ti, 0))
    vspec = pl.BlockSpec((1, tt, G_WIDTH), lambda bi, ti: (bi, ti, 0))
    return pl.pallas_call(
        functools.partial(_gla_prompt_kernel, nt=nt),
        grid=(b, nt),
        in_specs=[kspec, kspec, vspec, kspec, vspec, pl.BlockSpec((1, G_VAL_DIM), lambda bi, ti: (0, 0))],
        out_specs=(vspec, pl.BlockSpec((1, G_HEADS, G_KEY_DIM, G_VAL_DIM), lambda bi, ti: (bi, 0, 0, 0))),
        out_shape=(jax.ShapeDtypeStruct((b, seq, G_WIDTH), BF16),
                   jax.ShapeDtypeStruct((b, G_HEADS, G_KEY_DIM, G_VAL_DIM), F32)),
        scratch_shapes=[pltpu.VMEM((G_VAL_DIM, G_KEY_WIDTH), F32)],
        compiler_params=pltpu.CompilerParams(
            dimension_semantics=("arbitrary", "arbitrary"), vmem_limit_bytes=VMEM_LIMIT_BYTES),
        name="gla_prompt",
    )(qg, kg, vg, lf, sg, gn)


def _sample_score_kernel(pt_ref, qt_ref, kt_ref, kc_ref, pn_ref, idx_ref, pown_ref,
                         buf_ref, sem_ref, qb_ref, sc_ref, *, layer, npages, nbuf, bs, nsel):
    b = pl.program_id(0)
    total = bs * npages
    nblk = npages // PAGES_PER_BLOCK
    dh = A_HEAD_DIM

    def page_copy(g, slot):
        bb = g // npages
        return pltpu.make_async_copy(kc_ref.at[layer, pt_ref[bb, g - bb * npages]], buf_ref.at[slot], sem_ref.at[slot])

    @pl.when(b == 0)
    def _prime():
        for s in range(nbuf):
            page_copy(s, s).start()

    qcol = _lane_column(qt_ref[...], b)
    kcol = _lane_column(kt_ref[...], b)
    for h in range(A_HEADS):
        qb_ref[h] = jnp.broadcast_to(qcol[h], (dh, PAGE_SIZE))

    def page_body(p, _):
        g = b * npages + p
        slot = lax.rem(g, nbuf)
        page_copy(g, slot).wait()
        for h in range(A_HEADS):
            sc_ref[p, h:h + 1, :] = jnp.sum(buf_ref[slot, h] * qb_ref[h], axis=0, keepdims=True)

        @pl.when(g + nbuf < total)
        def _next():
            page_copy(g + nbuf, slot).start()

        return 0

    lax.fori_loop(0, npages, page_body, 0)
    _sample_choose_and_weigh(sc_ref, qcol, kcol, pn_ref, idx_ref, pown_ref, npages, nsel)


def _sample_choose_and_weigh(sc_ref, qcol, kcol, pn_ref, idx_ref, pown_ref, npages, nsel):
    nblk = npages // PAGES_PER_BLOCK
    lane = lax.broadcasted_iota(jnp.int32, (A_HEADS, LANES), 1).astype(F32)
    sub = lax.broadcasted_iota(jnp.int32, (A_HEADS, LANES), 0)

    gate = jnp.full((A_HEADS, LANES), NEG_INF, F32)
    for j in range(nblk):
        sblk = sc_ref[PAGES_PER_BLOCK * j]
        for pp in range(1, PAGES_PER_BLOCK):
            sblk = sblk + sc_ref[PAGES_PER_BLOCK * j + pp]
        gj = jnp.sum(sblk, axis=1, keepdims=True) * (1.0 / MOBA_BLOCK)
        gate = jnp.where(lane == float(j), gj, gate)

    g = gate
    sel = jnp.zeros((A_HEADS, LANES), F32)
    idx_out = jnp.full((A_HEADS, LANES), -1.0, F32)
    for r in range(nsel):
        m = jnp.max(g, axis=1, keepdims=True)
        idx = jnp.min(jnp.where(g == m, lane, float(LANES)), axis=1, keepdims=True)
        pick = jnp.logical_and(lane == idx, m > NEG_INF)
        sel = jnp.where(pick, 1.0, sel)
        g = jnp.where(pick, NEG_INF, g)
        idx_out = jnp.where(lane == float(r), jnp.where(m > NEG_INF, idx, -1.0), idx_out)
    idx_ref[0] = idx_out.astype(jnp.int32)

    s_own = jnp.zeros((A_HEADS, LANES), F32)
    for h in range(A_HEADS):
        so = jnp.sum(qcol[h] * kcol[h], axis=0, keepdims=True)
        s_own = jnp.where(sub == h, jnp.broadcast_to(so, (A_HEADS, LANES)), s_own)

    masks = [jnp.broadcast_to(sel[:, j:j + 1], (A_HEADS, LANES)) > 0.0 for j in range(nblk)]
    mx = s_own
    for pg in range(npages):
        mx = jnp.maximum(mx, jnp.where(masks[pg // PAGES_PER_BLOCK], sc_ref[pg], NEG_INF))
    m = jnp.max(mx, axis=1, keepdims=True)
    lsum = jnp.zeros((A_HEADS, LANES), F32)
    for pg in range(npages):
        p = jnp.where(masks[pg // PAGES_PER_BLOCK], jnp.exp(sc_ref[pg] - m), 0.0)
        pn_ref[0, pg] = p
        lsum = lsum + p
    p_own = jnp.exp(s_own - m)
    inv = 1.0 / (jnp.sum(lsum, axis=1, keepdims=True) + p_own)
    for pg in range(npages):
        pn_ref[0, pg] = pn_ref[0, pg] * inv
    pown_ref[0] = p_own * inv


def _sample_score(page_table, qt, kt, cache_t, layer, nbuf=SAMPLE_PAGES_IN_FLIGHT):
    bs, npages = page_table.shape
    nblk = npages // PAGES_PER_BLOCK
    nsel = min(MOBA_TOPK, nblk + 1)
    feat = pl.BlockSpec((A_HEADS, A_HEAD_DIM, bs), lambda b, pt: (0, 0, 0))
    grid_spec = pltpu.PrefetchScalarGridSpec(
        num_scalar_prefetch=1,
        grid=(bs,),
        in_specs=[feat, feat, pl.BlockSpec(memory_space=pl.ANY)],
        out_specs=(
            pl.BlockSpec((1, npages, A_HEADS, PAGE_SIZE), lambda b, pt: (b, 0, 0, 0)),
            pl.BlockSpec((1, A_HEADS, LANES), lambda b, pt: (b, 0, 0)),
            pl.BlockSpec((1, A_HEADS, LANES), lambda b, pt: (b, 0, 0)),
        ),
        scratch_shapes=[
            pltpu.VMEM((min(nbuf, npages), A_HEADS, A_HEAD_DIM, PAGE_SIZE), F32),
            pltpu.SemaphoreType.DMA((min(nbuf, npages),)),
            pltpu.VMEM((A_HEADS, A_HEAD_DIM, PAGE_SIZE), F32),
            pltpu.VMEM((npages, A_HEADS, PAGE_SIZE), F32),
        ],
    )
    return pl.pallas_call(
        functools.partial(_sample_score_kernel, layer=layer, npages=npages, nbuf=min(nbuf, npages), bs=bs, nsel=nsel),
        grid_spec=grid_spec,
        out_shape=(
            jax.ShapeDtypeStruct((bs, npages, A_HEADS, PAGE_SIZE), F32),
            jax.ShapeDtypeStruct((bs, A_HEADS, LANES), jnp.int32),
            jax.ShapeDtypeStruct((bs, A_HEADS, LANES), F32),
        ),
        compiler_params=pltpu.CompilerParams(dimension_semantics=("arbitrary",), vmem_limit_bytes=VMEM_LIMIT_BYTES),
        name="sample_score",
    )(page_table, qt, kt, cache_t)


def _sample_pv_kernel(pt_ref, idx_ref, pn_ref, pown_ref, vt_ref, sgt_ref, vc_ref, o_ref,
                      vbuf_ref, sem_ref, *, layer, nsel, bs):
    b = pl.program_id(0)
    slot = lax.rem(b, 2)

    def issue(bb, sl):
        for h in range(A_HEADS):
            for r in range(nsel):
                j = jnp.maximum(idx_ref[bb, h * nsel + r], 0)
                for pp in range(PAGES_PER_BLOCK):
                    pg = pt_ref[bb, PAGES_PER_BLOCK * j + pp]
                    pltpu.make_async_copy(vc_ref.at[layer, pg, h], vbuf_ref.at[sl, h, r, pp], sem_ref.at[sl]).start()

    @pl.when(b == 0)
    def _first():
        issue(0, 0)
        o_ref[...] = jnp.zeros_like(o_ref)

    @pl.when(b + 1 < bs)
    def _prefetch():
        issue(b + 1, 1 - slot)

    for _ in range(A_HEADS * nsel * PAGES_PER_BLOCK):
        pltpu.make_async_copy(vc_ref.at[layer, 0, 0], vbuf_ref.at[slot, 0, 0, 0], sem_ref.at[slot]).wait()

    vcol = _lane_column(vt_ref[...], b)
    sgcol = _lane_column(sgt_ref[...], b)
    mine = lax.broadcasted_iota(jnp.int32, (1, bs), 1) == b
    for h in range(A_HEADS):
        acc = jnp.zeros((A_HEAD_DIM, PAGE_SIZE), F32)
        for r in range(nsel):
            jraw = idx_ref[b, h * nsel + r]
            j = jnp.maximum(jraw, 0)
            w = jnp.where(jraw >= 0, 1.0, 0.0)
            for pp in range(PAGES_PER_BLOCK):
                prow = pn_ref[0, PAGES_PER_BLOCK * j + pp, h:h + 1, :] * w
                acc = acc + vbuf_ref[slot, h, r, pp] * prow
        ocol = jnp.sum(acc, axis=1, keepdims=True) + pown_ref[0, h:h + 1, 0:1] * vcol[h]
        o_ref[h] = jnp.where(mine, jnp.broadcast_to(ocol * sgcol[h], (A_HEAD_DIM, bs)), o_ref[h])


def _sample_pv(page_table, idx, pn, pown, vt, sgt, cache_t, layer):
    bs, npages = page_table.shape
    nsel = idx.shape[1] // A_HEADS
    feat = pl.BlockSpec((A_HEADS, A_HEAD_DIM, bs), lambda b, pt, ix: (0, 0, 0))
    grid_spec = pltpu.PrefetchScalarGridSpec(
        num_scalar_prefetch=2,
        grid=(bs,),
        in_specs=[
            pl.BlockSpec((1, npages, A_HEADS, PAGE_SIZE), lambda b, pt, ix: (b, 0, 0, 0)),
            pl.BlockSpec((1, A_HEADS, LANES), lambda b, pt, ix: (b, 0, 0)),
            feat, feat,
            pl.BlockSpec(memory_space=pl.ANY),
        ],
        out_specs=feat,
        scratch_shapes=[
            pltpu.VMEM((2, A_HEADS, nsel, PAGES_PER_BLOCK, A_HEAD_DIM, PAGE_SIZE), F32),
            pltpu.SemaphoreType.DMA((2,)),
        ],
    )
    return pl.pallas_call(
        functools.partial(_sample_pv_kernel, layer=layer, nsel=nsel, bs=bs),
        grid_spec=grid_spec,
        out_shape=jax.ShapeDtypeStruct((A_HEADS, A_HEAD_DIM, bs), F32),
        compiler_params=pltpu.CompilerParams(dimension_semantics=("arbitrary",), vmem_limit_bytes=VMEM_LIMIT_BYTES),
        name="sample_pv",
    )(page_table, idx, pn, pown, vt, sgt, cache_t)


def _gla_sample_kernel(qt_ref, kt_ref, gt_ref, v_ref, sg_ref, s0_ref, gn_ref, s_ref, o_ref):
    b = pl.program_id(0)
    dv = G_VAL_DIM
    bs = v_ref.shape[0]

    @pl.when(b == 0)
    def _init():
        o_ref[...] = jnp.zeros_like(o_ref)

    qcol = _lane_column(qt_ref[...], b)
    kcol = _lane_column(kt_ref[...], b)
    gcol = _lane_column(gt_ref[...], b)
    mine = lax.broadcasted_iota(jnp.int32, (bs, dv), 0) == b
    for h in range(G_HEADS):
        cols = slice(h * dv, (h + 1) * dv)
        v = jnp.sum(jnp.where(mine, v_ref[:, cols], 0.0), axis=0, keepdims=True)
        sg = jnp.sum(jnp.where(mine, sg_ref[:, cols], 0.0), axis=0, keepdims=True)
        s_new = jnp.exp(gcol[h]) * s0_ref[0, h] + kcol[h] * v
        s_ref[0, h] = s_new
        o = jnp.sum(qcol[h] * s_new, axis=0, keepdims=True) * (G_KEY_DIM ** -0.5)
        ms = jnp.mean(o * o, axis=-1, keepdims=True)
        on = o * lax.rsqrt(ms + NORM_EPS) * gn_ref[...] * sg
        o_ref[:, cols] = jnp.where(mine, jnp.broadcast_to(on, (bs, dv)), o_ref[:, cols])


def _gla_sample(qt, kt, gt, v, sg, s0, gn):
    bs = s0.shape[0]
    feat = pl.BlockSpec((G_HEADS, G_KEY_DIM, bs), lambda b: (0, 0, 0))
    tok = pl.BlockSpec((bs, G_WIDTH), lambda b: (0, 0))
    sspec = pl.BlockSpec((1, G_HEADS, G_KEY_DIM, G_VAL_DIM), lambda b: (b, 0, 0, 0))
    return pl.pallas_call(
        _gla_sample_kernel,
        grid=(bs,),
        in_specs=[feat, feat, feat, tok, tok, sspec, pl.BlockSpec((1, G_VAL_DIM), lambda b: (0, 0))],
        out_specs=(sspec, tok),
        out_shape=(jax.ShapeDtypeStruct(s0.shape, F32), jax.ShapeDtypeStruct((bs, G_WIDTH), F32)),
        compiler_params=pltpu.CompilerParams(dimension_semantics=("arbitrary",)),
        name="gla_sample",
    )(qt, kt, gt, v, sg, s0, gn)


def _rope_tables(pos):
    inv = jnp.power(jnp.float32(ROPE_THETA), -jnp.arange(ROT_HALF, dtype=F32) * (2.0 / ROT_DIM))
    ang = pos.astype(F32)[:, None] * inv[None, :]
    return jnp.cos(ang).T, jnp.sin(ang).T


def _pick_tile(n, pref):
    t = min(n, pref)
    while n % t:
        t //= 2
    return t


def kernel(x_prompt, x_sample, cache_k, cache_v, state_gla, page_table, norm_g, w_in, w_a2, b_a2, gla_norm_g, w_out, final_norm_g):
    bp, lp, d = x_prompt.shape
    bs, ls, _ = x_sample.shape
    depth = norm_g.shape[0]
    past_len = page_table.shape[1] * PAGE_SIZE
    assert ls == 1 and lp % MOBA_BLOCK == 0 and past_len % MOBA_BLOCK == 0
    assert cache_k.shape[2:] == (PAGE_SIZE, A_HEADS, A_HEAD_DIM)

    tabs_p = _rope_tables(jnp.arange(lp, dtype=jnp.int32))
    tabs_s = _rope_tables(jnp.full((bs,), past_len, dtype=jnp.int32))
    kc_t = jnp.transpose(cache_k, (0, 1, 3, 4, 2))
    vc_t = jnp.transpose(cache_v, (0, 1, 3, 4, 2))

    tm_p = _pick_tile(lp, 512)
    tt = _pick_tile(lp, 256)
    hp = x_prompt.reshape(bp * lp, d)
    hs = x_sample.reshape(bs, d)
    fg = final_norm_g.reshape(1, d)
    sp_l, ks_l, vs_l, ss_l = [], [], [], []
    kv_prompt = None
    rows = A_WIDTH
    g_lo = 4 * rows
    g_hi = g_lo + 2 * G_KEY_WIDTH + 2 * G_WIDTH
    for l in range(depth):
        wt = jnp.transpose(w_in[l])
        w_t = wt[0:4 * rows].astype(BF16)
        a1_t = jnp.pad(wt[g_hi:], ((0, LANES - G_GATE_RANK), (0, 0)))
        w_row = jnp.transpose(jnp.concatenate([wt[g_lo:g_hi], a1_t], axis=0)).astype(BF16)
        wa2 = jnp.pad(w_a2[l], ((0, LANES - G_GATE_RANK), (0, 0))).astype(BF16)
        ba2 = b_a2[l].reshape(1, G_KEY_WIDTH)
        ng = norm_g[l].reshape(1, d)
        gn = gla_norm_g[l].reshape(1, G_VAL_DIM)
        wo = w_out[l].astype(BF16)
        last = l == depth - 1

        qt_s, kt_s, vt_s, sgt_s, qg_s, kg_s, vg_s, sgg_s, lf_s = _proj_in(
            hs, ng, w_row, w_t, wa2, ba2, tabs_s, 1, bs, 0, 1)
        a_feat = lambda a: a.reshape(A_HEADS, A_HEAD_DIM, bs)

        qt, kt_all, vt_all, sgt, qg, kg, vg, sgg, lf = _proj_in(
            hp, ng, w_row, w_t, wa2, ba2, tabs_p, bp, tm_p, l, depth, kv_prompt)
        kv_prompt = (kt_all, vt_all)
        moba_steps = bp * (A_WIDTH // (MOBA_HEADS_PER_STEP * A_HEAD_DIM)) * (lp // MOBA_BLOCK + 1)
        pps = _score_pages_per_step(page_table.shape[1], bs, moba_steps)
        if pps is None:
            oa = _moba_prompt(qt, kt_all, vt_all, sgt, l)
            pn, idx, pown = _sample_score(page_table, a_feat(qt_s), a_feat(kt_s), kc_t, l)
        else:
            oa, pn, idx, pown = _moba_prompt(
                qt, kt_all, vt_all, sgt, l, (page_table, a_feat(qt_s), a_feat(kt_s), kc_t, pps))
        og, s_fin = _gla_prompt(qg.reshape(bp, lp, -1), kg.reshape(bp, lp, -1), vg.reshape(bp, lp, -1),
                                lf.reshape(bp, lp, -1), sgg.reshape(bp, lp, -1), gn, tt)
        hp = _proj_out(oa.reshape(bp * lp, A_WIDTH), og.reshape(bp * lp, G_WIDTH), hp, wo, fg, tm_p, last)
        sp_l.append(s_fin)

        qg, kg, vg, sgg, lf, sgt = qg_s, kg_s, vg_s, sgg_s, lf_s, sgt_s
        nsel = min(MOBA_TOPK, page_table.shape[1] // PAGES_PER_BLOCK + 1)
        idx_s = idx[:, :, :nsel].reshape(bs, A_HEADS * nsel)
        oa_s = _sample_pv(page_table, idx_s, pn, pown, a_feat(vt_s), a_feat(sgt), vc_t, l)
        g_feat = lambda a: jnp.transpose(a.astype(F32)).reshape(G_HEADS, G_KEY_DIM, bs)
        s_new, og_s = _gla_sample(g_feat(qg), g_feat(kg), g_feat(lf), vg.astype(F32), sgg.astype(F32),
                                  state_gla[l], gn)
        hs = _proj_out(oa_s.reshape(A_WIDTH, bs), og_s, hs, wo, fg, bs, last, a_feature_major=True)
        ks_l.append(jnp.transpose(kt_s[0, 0]).reshape(bs, 1, A_HEADS, A_HEAD_DIM))
        vs_l.append(jnp.transpose(vt_s[0, 0]).reshape(bs, 1, A_HEADS, A_HEAD_DIM))
        ss_l.append(s_new)

    kt_all, vt_all = kv_prompt
    to_cache_layout = lambda a: jnp.transpose(a.reshape(depth, bp, A_HEADS, A_HEAD_DIM, lp), (0, 1, 4, 2, 3))
    return (hp.reshape(bp, lp, d), hs.reshape(bs, ls, d), to_cache_layout(kt_all), to_cache_layout(vt_all),
            jnp.stack(sp_l), jnp.stack(ks_l), jnp.stack(vs_l), jnp.stack(ss_l))
```

```python
import functools

import jax
import jax.numpy as jnp
from jax import lax
from jax.experimental import pallas as pl
from jax.experimental.pallas import tpu as pltpu

A_HEADS = 8
A_HEAD_DIM = 64
A_WIDTH = A_HEADS * A_HEAD_DIM
ROT_DIM = A_HEAD_DIM // 4
ROT_HALF = ROT_DIM // 2
ROPE_THETA = 500000.0
MOBA_BLOCK = 256
MOBA_TOPK = 3
G_HEADS = 4
G_KEY_DIM = 64
G_VAL_DIM = 128
G_KEY_WIDTH = G_HEADS * G_KEY_DIM
G_WIDTH = G_HEADS * G_VAL_DIM
G_GATE_RANK = 16
G_GATE_NORM = 16.0
G_CHUNK = 32
NORM_EPS = 1e-6
PAGE_SIZE = 128
PAGES_PER_BLOCK = MOBA_BLOCK // PAGE_SIZE

LANES = 128
VMEM_LIMIT_BYTES = 56 * 1024 * 1024

F32 = jnp.float32
BF16 = jnp.bfloat16
NEG_INF = float("-inf")
MASK_BIAS = -1e30
MOBA_HEADS_PER_STEP = 4
MOBA_BLOCKS_PER_ITERATION = 4
SAMPLE_PAGES_IN_FLIGHT = 24
V_PAD_ROWS = 16

_NT = (((1,), (1,)), ((), ()))
_TN = (((0,), (0,)), ((), ()))


def _dot(a, b):
    return jnp.dot(a, b, preferred_element_type=F32)


def _dot_nt(a, b):
    return lax.dot_general(a, b, _NT, preferred_element_type=F32)


def _dot_tn(a, b):
    return lax.dot_general(a, b, _TN, preferred_element_type=F32)


def _split3(x):
    x1 = x.astype(BF16)
    r1 = x - x1.astype(F32)
    x2 = r1.astype(BF16)
    x3 = (r1 - x2.astype(F32)).astype(BF16)
    return x1, x2, x3


def _lane_column(x, lane):
    pick = lax.broadcasted_iota(jnp.int32, (1,) * (x.ndim - 1) + (x.shape[-1],), x.ndim - 1) == lane
    return jnp.sum(jnp.where(pick, x.astype(F32), 0.0), axis=-1, keepdims=True)


def _proj_in_kernel(*refs, n_alias, layer, depth):
    x_ref, g_ref, w_ref, wt_ref, wa2_ref, ba2_ref, ct_ref, st_ref = refs[:8]
    qt_ref, kt_ref, vt_ref, sgt_ref, qg_ref, kg_ref, vg_ref, sgg_ref, lf_ref = refs[8 + n_alias:]
    if n_alias == 0:
        for other in range(depth):
            if other != layer:
                kt_ref[other] = jnp.zeros(kt_ref.shape[1:], kt_ref.dtype)
                vt_ref[other] = jnp.zeros(vt_ref.shape[1:], vt_ref.dtype)
        kt_ref, vt_ref = kt_ref.at[layer], vt_ref.at[layer]
    x = x_ref[...]
    ms = jnp.mean(x * x, axis=-1, keepdims=True)
    h = (x * lax.rsqrt(ms + NORM_EPS) * g_ref[...]).astype(BF16)

    def proj(lo, hi):
        return _dot(h, w_ref[:, lo:hi])

    o = 0
    qg_ref[...] = proj(o, o + G_KEY_WIDTH).astype(qg_ref.dtype)
    o += G_KEY_WIDTH
    kg_ref[...] = proj(o, o + G_KEY_WIDTH).astype(kg_ref.dtype)
    o += G_KEY_WIDTH
    vg_ref[...] = proj(o, o + G_WIDTH).astype(vg_ref.dtype)
    o += G_WIDTH
    zg = proj(o, o + G_WIDTH)
    sgg_ref[...] = (zg * jax.nn.sigmoid(zg)).astype(sgg_ref.dtype)
    o += G_WIDTH
    a1 = proj(o, o + LANES)
    la = _dot(a1.astype(BF16), wa2_ref[...]) + ba2_ref[...]
    lf_ref[...] = -(jnp.maximum(-la, 0.0) + jnp.log1p(jnp.exp(-jnp.abs(la)))) * (1.0 / G_GATE_NORM)

    zt = _dot_nt(wt_ref[...], h)
    ct, st = ct_ref[...], st_ref[...]
    q_scale = A_HEAD_DIM ** -0.5
    for hd in range(A_HEADS):
        b0 = hd * A_HEAD_DIM
        x1 = zt[b0:b0 + ROT_HALF]
        x2 = zt[b0 + ROT_HALF:b0 + ROT_DIM]
        qh = jnp.concatenate([x1 * ct - x2 * st, x2 * ct + x1 * st, zt[b0 + ROT_DIM:b0 + A_HEAD_DIM]], axis=0)
        qt_ref[b0:b0 + A_HEAD_DIM, :] = (qh * q_scale).astype(qt_ref.dtype)
        k0 = b0 + A_WIDTH
        x1 = zt[k0:k0 + ROT_HALF]
        x2 = zt[k0 + ROT_HALF:k0 + ROT_DIM]
        kt_ref[b0:b0 + ROT_HALF, :] = x1 * ct - x2 * st
        kt_ref[b0 + ROT_HALF:b0 + ROT_DIM, :] = x2 * ct + x1 * st
        kt_ref[b0 + ROT_DIM:b0 + A_HEAD_DIM, :] = zt[k0 + ROT_DIM:k0 + A_HEAD_DIM]
    vt_ref[...] = zt[2 * A_WIDTH:3 * A_WIDTH]
    zg = zt[3 * A_WIDTH:4 * A_WIDTH]
    sgt_ref[...] = (zg * jax.nn.sigmoid(zg)).astype(sgt_ref.dtype)


def _proj_in(x, norm_g, w_row, w_t, wa2, ba2, tabs, n_batch, tm, layer, depth, kv_prev=None):
    n, d = x.shape
    seq = n // n_batch
    nl = seq // tm
    ct_tab, st_tab = tabs
    row = lambda i: (i, 0)
    const = lambda i: (0, 0)
    feat_shape = jax.ShapeDtypeStruct((n_batch, A_WIDTH, seq), BF16)
    kv_shape = jax.ShapeDtypeStruct((depth, n_batch, A_WIDTH, seq), F32)
    out_shapes = (
        feat_shape,
        kv_shape, kv_shape,
        feat_shape,
        jax.ShapeDtypeStruct((n, G_KEY_WIDTH), BF16),
        jax.ShapeDtypeStruct((n, G_KEY_WIDTH), BF16),
        jax.ShapeDtypeStruct((n, G_WIDTH), BF16),
        jax.ShapeDtypeStruct((n, G_WIDTH), BF16),
        jax.ShapeDtypeStruct((n, G_KEY_WIDTH), F32),
    )
    feat_spec = pl.BlockSpec((None, A_WIDTH, tm), lambda i: (i // nl, 0, i % nl))
    if kv_prev is None:
        kv_spec = pl.BlockSpec((depth, None, A_WIDTH, tm), lambda i: (0, i // nl, 0, i % nl))
    else:
        kv_spec = pl.BlockSpec((None, None, A_WIDTH, tm), lambda i: (layer, i // nl, 0, i % nl))
    in_specs = [
        pl.BlockSpec((tm, d), row),
        pl.BlockSpec((1, d), const),
        pl.BlockSpec(w_row.shape, const),
        pl.BlockSpec(w_t.shape, const),
        pl.BlockSpec((LANES, G_KEY_WIDTH), const),
        pl.BlockSpec((1, G_KEY_WIDTH), const),
        pl.BlockSpec((ROT_HALF, tm), lambda i: (0, i % nl)),
        pl.BlockSpec((ROT_HALF, tm), lambda i: (0, i % nl)),
    ]
    args = [x, norm_g, w_row, w_t, wa2, ba2, ct_tab, st_tab]
    aliases = {}
    if kv_prev is not None:
        in_specs += [pl.BlockSpec(memory_space=pl.ANY)] * 2
        aliases = {len(args): 1, len(args) + 1: 2}
        args += list(kv_prev)
    return pl.pallas_call(
        functools.partial(_proj_in_kernel, n_alias=len(aliases), layer=layer, depth=depth),
        grid=(n // tm,),
        in_specs=in_specs,
        out_specs=(
            feat_spec, kv_spec, kv_spec, feat_spec,
            pl.BlockSpec((tm, G_KEY_WIDTH), row), pl.BlockSpec((tm, G_KEY_WIDTH), row),
            pl.BlockSpec((tm, G_WIDTH), row), pl.BlockSpec((tm, G_WIDTH), row),
            pl.BlockSpec((tm, G_KEY_WIDTH), row),
        ),
        out_shape=out_shapes,
        input_output_aliases=aliases,
        compiler_params=pltpu.CompilerParams(dimension_semantics=("arbitrary",), vmem_limit_bytes=VMEM_LIMIT_BYTES),
        name="proj_in",
    )(*args)


def _proj_out_kernel(a_ref, g_ref, x_ref, w_ref, fg_ref, o_ref, *, final, a_feature_major):
    half = g_ref.shape[1]
    a = a_ref[...].astype(BF16)
    ya = _dot_tn(a, w_ref[0:half, :]) if a_feature_major else _dot(a, w_ref[0:half, :])
    xo = x_ref[...] + (ya + _dot(g_ref[...].astype(BF16), w_ref[half:, :]))
    if final:
        ms = jnp.mean(xo * xo, axis=-1, keepdims=True)
        xo = xo * lax.rsqrt(ms + NORM_EPS) * fg_ref[...]
    o_ref[...] = xo


def _proj_out(a, g, x, w, fg, tm, final, a_feature_major=False):
    n, d = x.shape
    row = lambda i: (i, 0)
    const = lambda i: (0, 0)
    a_spec = pl.BlockSpec((a.shape[0], tm), lambda i: (0, i)) if a_feature_major else pl.BlockSpec((tm, a.shape[1]), row)
    return pl.pallas_call(
        functools.partial(_proj_out_kernel, final=final, a_feature_major=a_feature_major),
        grid=(n // tm,),
        in_specs=[
            a_spec,
            pl.BlockSpec((tm, g.shape[1]), row),
            pl.BlockSpec((tm, d), row),
            pl.BlockSpec(w.shape, const),
            pl.BlockSpec((1, d), const),
        ],
        out_specs=pl.BlockSpec((tm, d), row),
        out_shape=jax.ShapeDtypeStruct((n, d), F32),
        compiler_params=pltpu.CompilerParams(dimension_semantics=("arbitrary",), vmem_limit_bytes=VMEM_LIMIT_BYTES),
        name="proj_out",
    )(a, g, x, w, fg)


def _moba_prompt_kernel(*refs, nb, nsel, unroll, side):
    if side is None:
        qt_ref, kt_ref, vt_ref, sgt_ref, o_ref, ka_ref, va_ref, km_ref, qa_ref, s_ref, acc_ref, m_ref = refs
    else:
        (pt_ref, qt_ref, kt_ref, vt_ref, sgt_ref, qs_ref, ks_ref, kc_ref, o_ref, pn_ref, idx_ref, pown_ref,
         ka_ref, va_ref, km_ref, qa_ref, s_ref, acc_ref, m_ref, buf_ref, sem_ref, qb_ref, sc_ref) = refs
    qi = pl.program_id(2)
    blk = MOBA_BLOCK
    dh = A_HEAD_DIM
    heads = qt_ref.shape[0] // dh
    per_group = LANES // dh

    if side is not None:
        layer, npages, bs, pps, nsel_s = side
        sps = npages // pps
        step = (pl.program_id(0) * pl.num_programs(1) + pl.program_id(1)) * pl.num_programs(2) + qi
        seq = step // sps
        part = step - seq * sps
        total = bs * npages
        half = lax.rem(step, 2) * pps

        def page_copy(g, slot):
            bb = g // npages
            return pltpu.make_async_copy(
                kc_ref.at[layer, pt_ref[bb, g - bb * npages]], buf_ref.at[slot], sem_ref.at[slot])

        @pl.when(step == 0)
        def _prime():
            for i in range(2 * pps):
                page_copy(i, i).start()

        @pl.when(seq < bs)
        def _score_pages():
            @pl.when(part == 0)
            def _new_sequence():
                qcol = _lane_column(qs_ref[...], seq)
                for h in range(A_HEADS):
                    qb_ref[h] = jnp.broadcast_to(qcol[h], (dh, PAGE_SIZE))

            g0 = step * pps
            for i in range(pps):
                page_copy(g0 + i, half + i).wait()
            for h in range(A_HEADS):
                qb = qb_ref[h]
                for i in range(pps):
                    sc_ref[part * pps + i, h:h + 1, :] = jnp.sum(buf_ref[half + i, h] * qb, axis=0, keepdims=True)
            for i in range(pps):
                @pl.when(g0 + i + 2 * pps < total)
                def _refill():
                    page_copy(g0 + i + 2 * pps, half + i).start()

            @pl.when(part == sps - 1)
            def _finish_sequence():
                _sample_choose_and_weigh(sc_ref, _lane_column(qs_ref[...], seq), _lane_column(ks_ref[...], seq),
                                         pn_ref, idx_ref, pown_ref, npages, nsel_s)

    @pl.when(qi == 0)
    def _prepare():
        lane = lax.broadcasted_iota(jnp.int32, (blk, LANES), 1)
        ones_rows = jnp.where(lax.broadcasted_iota(jnp.int32, (V_PAD_ROWS, blk), 0) == 0, 1.0, 0.0)
        for j in range(nb):
            kj = kt_ref[:, j * blk:(j + 1) * blk].T
            km_ref[j:j + 1, :] = jnp.mean(kj, axis=0, keepdims=True)
            onehot = jnp.where(lane == dh + j, 1.0, 0.0)
            for hh in range(heads):
                grp, sub = divmod(hh, per_group)
                kh = kj[:, grp * LANES:(grp + 1) * LANES]
                if sub:
                    kh = pltpu.roll(kh, LANES - sub * dh, 1)
                ka_ref[hh, j * blk:(j + 1) * blk, :] = jnp.where(lane < dh, kh, onehot).astype(BF16)
                vh = vt_ref[hh * dh:(hh + 1) * dh, j * blk:(j + 1) * blk]
                va_ref[j, hh] = jnp.concatenate([vh, ones_rows], axis=0).astype(BF16)
        m_ref[...] = jnp.zeros_like(m_ref)

    @pl.when(jnp.logical_and(qi == 0, jnp.logical_and(pl.program_id(0) == 0, pl.program_id(1) == 0)))
    def _define_scores():
        s_ref[...] = jnp.zeros_like(s_ref)

    qb = jnp.minimum(qi, nb - 1)
    prev = jnp.maximum(qi - 1, 0)
    m_old = [m_ref[hh] for hh in range(heads)]

    for hh in range(heads):
        acc_ref[hh] = _dot(va_ref[prev, hh], jnp.exp(s_ref[hh, nb] - m_old[hh]).astype(BF16))

    rowf = lax.broadcasted_iota(jnp.int32, (nb, blk), 0).astype(F32)
    past = rowf < qb.astype(F32)
    causal = lax.broadcasted_iota(jnp.int32, (blk, blk), 0) <= lax.broadcasted_iota(jnp.int32, (blk, blk), 1)
    pad = jnp.zeros((LANES - dh - nb, blk), BF16)
    r_own = pl.multiple_of(qb * blk, blk)

    m_run, qd = [], []
    for hh in range(heads):
        qh = qt_ref[hh * dh:(hh + 1) * dh, :]
        km = km_ref[:, hh * dh:(hh + 1) * dh]
        km1 = km.astype(BF16)
        km2 = (km - km1.astype(F32)).astype(BF16)
        g = jnp.where(past, _dot(km1, qh) + _dot(km2, qh), NEG_INF)
        sel = jnp.zeros((nb, blk), F32)
        for _ in range(nsel):
            m = jnp.max(g, axis=0, keepdims=True)
            idx = jnp.min(jnp.where(g == m, rowf, float(nb)), axis=0, keepdims=True)
            pick = jnp.logical_and(rowf == idx, m > NEG_INF)
            sel = jnp.where(pick, 1.0, sel)
            g = jnp.where(pick, NEG_INF, g)
        bias = jnp.where(sel > 0.0, 0.0, MASK_BIAS).astype(BF16)
        qa_ref[hh] = jnp.concatenate([qh, bias, pad], axis=0)
        qd.append(jnp.concatenate([qh, jnp.zeros((LANES - dh, blk), BF16)], axis=0))

    for hh in range(heads):
        s = jnp.where(causal, _dot(ka_ref[hh, pl.ds(r_own, blk), :], qd[hh]), MASK_BIAS)
        s_ref[hh, nb] = s
        m_run.append(jnp.max(s, axis=0, keepdims=True))

    def blocks(j0, count, ms):
        ms = list(ms)
        for hh in range(heads):
            part = None
            for u in range(count):
                j = j0 + u
                p = jnp.where(j < prev, jnp.exp(s_ref[hh, j] - m_old[hh]), 0.0).astype(BF16)
                t = _dot(va_ref[j, hh], p)
                part = t if part is None else part + t
                r0 = pl.multiple_of(j * blk, blk)
                s = _dot(ka_ref[hh, pl.ds(r0, blk), :], qa_ref[hh])
                s_ref[hh, j] = s
                ms[hh] = jnp.maximum(ms[hh], jnp.max(s, axis=0, keepdims=True))
            acc_ref[hh] += part
        return tuple(ms)

    m_fin = tuple(m_run)
    done = 0
    width = unroll
    while width >= 1:
        trips = lax.shift_right_logical(qi - done, width.bit_length() - 1)
        m_fin = lax.fori_loop(0, trips, functools.partial(
            lambda gi, ms, j0, w: blocks(j0 + gi * w, w, ms), j0=done, w=width), m_fin)
        done = done + trips * width
        width //= 2
    for hh in range(heads):
        m_ref[hh] = m_fin[hh]

    ot = jnp.concatenate([acc_ref[hh, 0:dh, :] / acc_ref[hh, dh:dh + 1, :] for hh in range(heads)], axis=0)
    o_ref[...] = (ot * sgt_ref[...].astype(F32)).T.astype(o_ref.dtype)


def _score_pages_per_step(npages, bs, steps):
    for pps in range(1, npages + 1):
        if npages % pps == 0 and bs * (npages // pps) <= steps:
            return pps
    return None


def _moba_prompt(qt, kt_all, vt_all, sgt, layer, decode=None):
    b, width, seq = qt.shape
    nb = seq // MOBA_BLOCK
    nsel = min(MOBA_TOPK, nb)
    heads = MOBA_HEADS_PER_STEP
    hw = heads * A_HEAD_DIM
    assert A_HEAD_DIM + nb <= LANES and width % hw == 0
    unroll = MOBA_BLOCKS_PER_ITERATION
    n_hp = width // hw
    grid = (b, n_hp, nb + 1)
    scratch = [
        pltpu.VMEM((heads, seq, LANES), BF16),
        pltpu.VMEM((nb, heads, A_HEAD_DIM + V_PAD_ROWS, MOBA_BLOCK), BF16),
        pltpu.VMEM((nb, hw), F32),
        pltpu.VMEM((heads, LANES, MOBA_BLOCK), BF16),
        pltpu.VMEM((heads, nb + 1, MOBA_BLOCK, MOBA_BLOCK), F32),
        pltpu.VMEM((heads, A_HEAD_DIM + V_PAD_ROWS, MOBA_BLOCK), F32),
        pltpu.VMEM((heads, 1, MOBA_BLOCK), F32),
    ]
    params = pltpu.CompilerParams(
        dimension_semantics=("arbitrary", "arbitrary", "arbitrary"), vmem_limit_bytes=VMEM_LIMIT_BYTES)
    q_map = lambda bi, hp, qi, *_: (bi, hp, jnp.minimum(qi, nb - 1))
    g_map = lambda bi, hp, qi, *_: (bi, hp, jnp.maximum(qi - 1, 0))
    kv_map = lambda bi, hp, qi, *_: (layer, bi, hp, 0)
    o_map = lambda bi, hp, qi, *_: (bi, jnp.maximum(qi - 1, 0), hp)
    in_specs = [pl.BlockSpec((None, hw, MOBA_BLOCK), q_map), pl.BlockSpec((None, None, hw, seq), kv_map),
                pl.BlockSpec((None, None, hw, seq), kv_map), pl.BlockSpec((None, hw, MOBA_BLOCK), g_map)]
    o_spec = pl.BlockSpec((None, MOBA_BLOCK, hw), o_map)
    o_shape = jax.ShapeDtypeStruct((b, seq, width), BF16)
    if decode is None:
        return pl.pallas_call(
            functools.partial(_moba_prompt_kernel, nb=nb, nsel=nsel, unroll=unroll, side=None),
            grid=grid, in_specs=in_specs, out_specs=o_spec, out_shape=o_shape, scratch_shapes=scratch,
            compiler_params=params, name="moba_prompt",
        )(qt, kt_all, vt_all, sgt)

    page_table, q_s, k_s, cache_t, pps = decode
    bs, npages = page_table.shape
    sps = npages // pps
    nsel_s = min(MOBA_TOPK, npages // PAGES_PER_BLOCK + 1)
    seq_of = lambda bi, hp, qi: jnp.minimum(((bi * n_hp + hp) * (nb + 1) + qi) // sps, bs - 1)
    feat = pl.BlockSpec((A_HEADS, A_HEAD_DIM, bs), lambda bi, hp, qi, pt: (0, 0, 0))
    grid_spec = pltpu.PrefetchScalarGridSpec(
        num_scalar_prefetch=1,
        grid=grid,
        in_specs=in_specs + [feat, feat, pl.BlockSpec(memory_space=pl.ANY)],
        out_specs=(
            o_spec,
            pl.BlockSpec((1, npages, A_HEADS, PAGE_SIZE), lambda bi, hp, qi, pt: (seq_of(bi, hp, qi), 0, 0, 0)),
            pl.BlockSpec((1, A_HEADS, LANES), lambda bi, hp, qi, pt: (seq_of(bi, hp, qi), 0, 0)),
            pl.BlockSpec((1, A_HEADS, LANES), lambda bi, hp, qi, pt: (seq_of(bi, hp, qi), 0, 0)),
        ),
        scratch_shapes=scratch + [
            pltpu.VMEM((2 * pps, A_HEADS, A_HEAD_DIM, PAGE_SIZE), F32),
            pltpu.SemaphoreType.DMA((2 * pps,)),
            pltpu.VMEM((A_HEADS, A_HEAD_DIM, PAGE_SIZE), F32),
            pltpu.VMEM((npages, A_HEADS, PAGE_SIZE), F32),
        ],
    )
    return pl.pallas_call(
        functools.partial(_moba_prompt_kernel, nb=nb, nsel=nsel, unroll=unroll,
                          side=(layer, npages, bs, pps, nsel_s)),
        grid_spec=grid_spec,
        out_shape=(
            o_shape,
            jax.ShapeDtypeStruct((bs, npages, A_HEADS, PAGE_SIZE), F32),
            jax.ShapeDtypeStruct((bs, A_HEADS, LANES), jnp.int32),
            jax.ShapeDtypeStruct((bs, A_HEADS, LANES), F32),
        ),
        compiler_params=params,
        name="moba_prompt",
    )(page_table, qt, kt_all, vt_all, sgt, q_s, k_s, cache_t)


def _gla_prompt_kernel(q_ref, k_ref, v_ref, lf_ref, sg_ref, gn_ref, o_ref, s_ref, st_ref, *, nt):
    t = pl.program_id(1)
    tt = q_ref.shape[1]
    ck = G_CHUNK
    dk, dv = G_KEY_DIM, G_VAL_DIM
    nh = G_HEADS

    @pl.when(t == 0)
    def _init():
        st_ref[...] = jnp.zeros_like(st_ref)

    g1, g2, g3 = _split3(lf_ref[0])
    row = lax.broadcasted_iota(jnp.int32, (tt, tt), 0)
    col = lax.broadcasted_iota(jnp.int32, (tt, tt), 1)
    ck_shift = ck.bit_length() - 1
    dk_shift = dk.bit_length() - 1
    same = lax.shift_right_logical(row, ck_shift) == lax.shift_right_logical(col, ck_shift)
    lower = jnp.where(jnp.logical_and(same, col <= row), 1.0, 0.0).astype(BF16)
    upper = jnp.where(jnp.logical_and(same, col > row), 1.0, 0.0).astype(BF16)
    bcum = _dot(lower, g1) + _dot(lower, g2) + _dot(lower, g3)
    rest = _dot(upper, g1) + _dot(upper, g2) + _dot(upper, g3)

    qf = q_ref[0].astype(F32)
    kf = k_ref[0].astype(F32)
    qt = qf * jnp.exp(bcum) * (dk ** -0.5)
    kt = kf * jnp.exp(-bcum)
    kd = kf * jnp.exp(rest)

    head_of_lane = lax.shift_right_logical(lax.broadcasted_iota(jnp.int32, (ck, nh * dk), 1), dk_shift)
    own = [head_of_lane == hd for hd in range(nh)]
    ri = lax.broadcasted_iota(jnp.int32, (nh * ck, nh * ck), 0)
    ci = lax.broadcasted_iota(jnp.int32, (nh * ck, nh * ck), 1)
    causal = jnp.bitwise_and(ri, ck - 1) >= jnp.bitwise_and(ci, ck - 1)
    gn = gn_ref[...]
    st = st_ref[...]

    def stack(x):
        return jnp.concatenate([jnp.where(own[hd], x, 0.0) for hd in range(nh)], axis=0).astype(BF16)

    nc = tt // ck
    q4s, o_intra, kvs = [], [], []
    for c in range(nc):
        r0, r1 = c * ck, (c + 1) * ck
        q4, k4, kd4 = stack(qt[r0:r1]), stack(kt[r0:r1]), stack(kd[r0:r1])
        v4 = jnp.concatenate([v_ref[0, r0:r1, hd * dv:(hd + 1) * dv] for hd in range(nh)], axis=0)
        a = jnp.where(causal, _dot_nt(q4, k4), 0.0)
        q4s.append(q4)
        o_intra.append(_dot(a.astype(BF16), v4))
        kvs.append(_dot_tn(v4, kd4))

    for c in range(nc):
        r0, r1 = c * ck, (c + 1) * ck
        o = o_intra[c] + _dot_nt(q4s[c], st.astype(BF16))
        ms = jnp.mean(o * o, axis=-1, keepdims=True)
        on = o * lax.rsqrt(ms + NORM_EPS) * gn
        for hd in range(nh):
            sg = sg_ref[0, r0:r1, hd * dv:(hd + 1) * dv].astype(F32)
            o_ref[0, r0:r1, hd * dv:(hd + 1) * dv] = (on[hd * ck:(hd + 1) * ck] * sg).astype(o_ref.dtype)
        st = st * jnp.exp(bcum[r1 - 1:r1, :]) + kvs[c]

    st_ref[...] = st

    @pl.when(t == nt - 1)
    def _emit():
        per_group = LANES // dk
        for grp in range(nh // per_group):
            s_grp = st[:, grp * LANES:(grp + 1) * LANES].T
            for hh in range(per_group):
                s_ref[0, grp * per_group + hh] = s_grp[hh * dk:(hh + 1) * dk, :]


def _gla_prompt(qg, kg, vg, lf, sg, gn, tt):
    b, seq, _ = qg.shape
    nt = seq // tt
    kspec = pl.BlockSpec((1, tt, G_KEY_WIDTH), lambda bi, ti: (bi, ti, 0))
    vspec = pl.BlockSpec((1, tt, G_WIDTH), lambda bi, ti: (bi, ti, 0))
    return pl.pallas_call(
        functools.partial(_gla_prompt_kernel, nt=nt),
        grid=(b, nt),
        in_specs=[kspec, kspec, vspec, kspec, vspec, pl.BlockSpec((1, G_VAL_DIM), lambda bi, ti: (0, 0))],
        out_specs=(vspec, pl.BlockSpec((1, G_HEADS, G_KEY_DIM, G_VAL_DIM), lambda bi, ti: (bi, 0, 0, 0))),
        out_shape=(jax.ShapeDtypeStruct((b, seq, G_WIDTH), BF16),
                   jax.ShapeDtypeStruct((b, G_HEADS, G_KEY_DIM, G_VAL_DIM), F32)),
        scratch_shapes=[pltpu.VMEM((G_VAL_DIM, G_KEY_WIDTH), F32)],
        compiler_params=pltpu.CompilerParams(
            dimension_semantics=("arbitrary", "arbitrary"), vmem_limit_bytes=VMEM_LIMIT_BYTES),
        name="gla_prompt",
    )(qg, kg, vg, lf, sg, gn)


def _sample_score_kernel(pt_ref, qt_ref, kt_ref, kc_ref, pn_ref, idx_ref, pown_ref,
                         buf_ref, sem_ref, qb_ref, sc_ref, *, layer, npages, nbuf, bs, nsel):
    b = pl.program_id(0)
    total = bs * npages
    nblk = npages // PAGES_PER_BLOCK
    dh = A_HEAD_DIM

    def page_copy(g, slot):
        bb = g // npages
        return pltpu.make_async_copy(kc_ref.at[layer, pt_ref[bb, g - bb * npages]], buf_ref.at[slot], sem_ref.at[slot])

    @pl.when(b == 0)
    def _prime():
        for s in range(nbuf):
            page_copy(s, s).start()

    qcol = _lane_column(qt_ref[...], b)
    kcol = _lane_column(kt_ref[...], b)
    for h in range(A_HEADS):
        qb_ref[h] = jnp.broadcast_to(qcol[h], (dh, PAGE_SIZE))

    def page_body(p, _):
        g = b * npages + p
        slot = lax.rem(g, nbuf)
        page_copy(g, slot).wait()
        for h in range(A_HEADS):
            sc_ref[p, h:h + 1, :] = jnp.sum(buf_ref[slot, h] * qb_ref[h], axis=0, keepdims=True)

        @pl.when(g + nbuf < total)
        def _next():
            page_copy(g + nbuf, slot).start()

        return 0

    lax.fori_loop(0, npages, page_body, 0)
    _sample_choose_and_weigh(sc_ref, qcol, kcol, pn_ref, idx_ref, pown_ref, npages, nsel)


def _sample_choose_and_weigh(sc_ref, qcol, kcol, pn_ref, idx_ref, pown_ref, npages, nsel):
    nblk = npages // PAGES_PER_BLOCK
    lane = lax.broadcasted_iota(jnp.int32, (A_HEADS, LANES), 1).astype(F32)
    sub = lax.broadcasted_iota(jnp.int32, (A_HEADS, LANES), 0)

    gate = jnp.full((A_HEADS, LANES), NEG_INF, F32)
    for j in range(nblk):
        sblk = sc_ref[PAGES_PER_BLOCK * j]
        for pp in range(1, PAGES_PER_BLOCK):
            sblk = sblk + sc_ref[PAGES_PER_BLOCK * j + pp]
        gj = jnp.sum(sblk, axis=1, keepdims=True) * (1.0 / MOBA_BLOCK)
        gate = jnp.where(lane == float(j), gj, gate)

    g = gate
    sel = jnp.zeros((A_HEADS, LANES), F32)
    idx_out = jnp.full((A_HEADS, LANES), -1.0, F32)
    for r in range(nsel):
        m = jnp.max(g, axis=1, keepdims=True)
        idx = jnp.min(jnp.where(g == m, lane, float(LANES)), axis=1, keepdims=True)
        pick = jnp.logical_and(lane == idx, m > NEG_INF)
        sel = jnp.where(pick, 1.0, sel)
        g = jnp.where(pick, NEG_INF, g)
        idx_out = jnp.where(lane == float(r), jnp.where(m > NEG_INF, idx, -1.0), idx_out)
    idx_ref[0] = idx_out.astype(jnp.int32)

    s_own = jnp.zeros((A_HEADS, LANES), F32)
    for h in range(A_HEADS):
        so = jnp.sum(qcol[h] * kcol[h], axis=0, keepdims=True)
        s_own = jnp.where(sub == h, jnp.broadcast_to(so, (A_HEADS, LANES)), s_own)

    masks = [jnp.broadcast_to(sel[:, j:j + 1], (A_HEADS, LANES)) > 0.0 for j in range(nblk)]
    mx = s_own
    for pg in range(npages):
        mx = jnp.maximum(mx, jnp.where(masks[pg // PAGES_PER_BLOCK], sc_ref[pg], NEG_INF))
    m = jnp.max(mx, axis=1, keepdims=True)
    lsum = jnp.zeros((A_HEADS, LANES), F32)
    for pg in range(npages):
        p = jnp.where(masks[pg // PAGES_PER_BLOCK], jnp.exp(sc_ref[pg] - m), 0.0)
        pn_ref[0, pg] = p
        lsum = lsum + p
    p_own = jnp.exp(s_own - m)
    inv = 1.0 / (jnp.sum(lsum, axis=1, keepdims=True) + p_own)
    for pg in range(npages):
        pn_ref[0, pg] = pn_ref[0, pg] * inv
    pown_ref[0] = p_own * inv


def _sample_score(page_table, qt, kt, cache_t, layer, nbuf=SAMPLE_PAGES_IN_FLIGHT):
    bs, npages = page_table.shape
    nblk = npages // PAGES_PER_BLOCK
    nsel = min(MOBA_TOPK, nblk + 1)
    feat = pl.BlockSpec((A_HEADS, A_HEAD_DIM, bs), lambda b, pt: (0, 0, 0))
    grid_spec = pltpu.PrefetchScalarGridSpec(
        num_scalar_prefetch=1,
        grid=(bs,),
        in_specs=[feat, feat, pl.BlockSpec(memory_space=pl.ANY)],
        out_specs=(
            pl.BlockSpec((1, npages, A_HEADS, PAGE_SIZE), lambda b, pt: (b, 0, 0, 0)),
            pl.BlockSpec((1, A_HEADS, LANES), lambda b, pt: (b, 0, 0)),
            pl.BlockSpec((1, A_HEADS, LANES), lambda b, pt: (b, 0, 0)),
        ),
        scratch_shapes=[
            pltpu.VMEM((min(nbuf, npages), A_HEADS, A_HEAD_DIM, PAGE_SIZE), F32),
            pltpu.SemaphoreType.DMA((min(nbuf, npages),)),
            pltpu.VMEM((A_HEADS, A_HEAD_DIM, PAGE_SIZE), F32),
            pltpu.VMEM((npages, A_HEADS, PAGE_SIZE), F32),
        ],
    )
    return pl.pallas_call(
        functools.partial(_sample_score_kernel, layer=layer, npages=npages, nbuf=min(nbuf, npages), bs=bs, nsel=nsel),
        grid_spec=grid_spec,
        out_shape=(
            jax.ShapeDtypeStruct((bs, npages, A_HEADS, PAGE_SIZE), F32),
            jax.ShapeDtypeStruct((bs, A_HEADS, LANES), jnp.int32),
            jax.ShapeDtypeStruct((bs, A_HEADS, LANES), F32),
        ),
        compiler_params=pltpu.CompilerParams(dimension_semantics=("arbitrary",), vmem_limit_bytes=VMEM_LIMIT_BYTES),
        name="sample_score",
    )(page_table, qt, kt, cache_t)


def _sample_pv_kernel(pt_ref, idx_ref, pn_ref, pown_ref, vt_ref, sgt_ref, vc_ref, o_ref,
                      vbuf_ref, sem_ref, *, layer, nsel, bs):
    b = pl.program_id(0)
    slot = lax.rem(b, 2)

    def issue(bb, sl):
        for h in range(A_HEADS):
            for r in range(nsel):
                j = jnp.maximum(idx_ref[bb, h * nsel + r], 0)
                for pp in range(PAGES_PER_BLOCK):
                    pg = pt_ref[bb, PAGES_PER_BLOCK * j + pp]
                    pltpu.make_async_copy(vc_ref.at[layer, pg, h], vbuf_ref.at[sl, h, r, pp], sem_ref.at[sl]).start()

    @pl.when(b == 0)
    def _first():
        issue(0, 0)
        o_ref[...] = jnp.zeros_like(o_ref)

    @pl.when(b + 1 < bs)
    def _prefetch():
        issue(b + 1, 1 - slot)

    for _ in range(A_HEADS * nsel * PAGES_PER_BLOCK):
        pltpu.make_async_copy(vc_ref.at[layer, 0, 0], vbuf_ref.at[slot, 0, 0, 0], sem_ref.at[slot]).wait()

    vcol = _lane_column(vt_ref[...], b)
    sgcol = _lane_column(sgt_ref[...], b)
    mine = lax.broadcasted_iota(jnp.int32, (1, bs), 1) == b
    for h in range(A_HEADS):
        acc = jnp.zeros((A_HEAD_DIM, PAGE_SIZE), F32)
        for r in range(nsel):
            jraw = idx_ref[b, h * nsel + r]
            j = jnp.maximum(jraw, 0)
            w = jnp.where(jraw >= 0, 1.0, 0.0)
            for pp in range(PAGES_PER_BLOCK):
                prow = pn_ref[0, PAGES_PER_BLOCK * j + pp, h:h + 1, :] * w
                acc = acc + vbuf_ref[slot, h, r, pp] * prow
        ocol = jnp.sum(acc, axis=1, keepdims=True) + pown_ref[0, h:h + 1, 0:1] * vcol[h]
        o_ref[h] = jnp.where(mine, jnp.broadcast_to(ocol * sgcol[h], (A_HEAD_DIM, bs)), o_ref[h])


def _sample_pv(page_table, idx, pn, pown, vt, sgt, cache_t, layer):
    bs, npages = page_table.shape
    nsel = idx.shape[1] // A_HEADS
    feat = pl.BlockSpec((A_HEADS, A_HEAD_DIM, bs), lambda b, pt, ix: (0, 0, 0))
    grid_spec = pltpu.PrefetchScalarGridSpec(
        num_scalar_prefetch=2,
        grid=(bs,),
        in_specs=[
            pl.BlockSpec((1, npages, A_HEADS, PAGE_SIZE), lambda b, pt, ix: (b, 0, 0, 0)),
            pl.BlockSpec((1, A_HEADS, LANES), lambda b, pt, ix: (b, 0, 0)),
            feat, feat,
            pl.BlockSpec(memory_space=pl.ANY),
        ],
        out_specs=feat,
        scratch_shapes=[
            pltpu.VMEM((2, A_HEADS, nsel, PAGES_PER_BLOCK, A_HEAD_DIM, PAGE_SIZE), F32),
            pltpu.SemaphoreType.DMA((2,)),
        ],
    )
    return pl.pallas_call(
        functools.partial(_sample_pv_kernel, layer=layer, nsel=nsel, bs=bs),
        grid_spec=grid_spec,
        out_shape=jax.ShapeDtypeStruct((A_HEADS, A_HEAD_DIM, bs), F32),
        compiler_params=pltpu.CompilerParams(dimension_semantics=("arbitrary",), vmem_limit_bytes=VMEM_LIMIT_BYTES),
        name="sample_pv",
    )(page_table, idx, pn, pown, vt, sgt, cache_t)


def _gla_sample_kernel(qt_ref, kt_ref, gt_ref, v_ref, sg_ref, s0_ref, gn_ref, s_ref, o_ref):
    b = pl.program_id(0)
    dv = G_VAL_DIM
    bs = v_ref.shape[0]

    @pl.when(b == 0)
    def _init():
        o_ref[...] = jnp.zeros_like(o_ref)

    qcol = _lane_column(qt_ref[...], b)
    kcol = _lane_column(kt_ref[...], b)
    gcol = _lane_column(gt_ref[...], b)
    mine = lax.broadcasted_iota(jnp.int32, (bs, dv), 0) == b
    for h in range(G_HEADS):
        cols = slice(h * dv, (h + 1) * dv)
        v = jnp.sum(jnp.where(mine, v_ref[:, cols], 0.0), axis=0, keepdims=True)
        sg = jnp.sum(jnp.where(mine, sg_ref[:, cols], 0.0), axis=0, keepdims=True)
        s_new = jnp.exp(gcol[h]) * s0_ref[0, h] + kcol[h] * v
        s_ref[0, h] = s_new
        o = jnp.sum(qcol[h] * s_new, axis=0, keepdims=True) * (G_KEY_DIM ** -0.5)
        ms = jnp.mean(o * o, axis=-1, keepdims=True)
        on = o * lax.rsqrt(ms + NORM_EPS) * gn_ref[...] * sg
        o_ref[:, cols] = jnp.where(mine, jnp.broadcast_to(on, (bs, dv)), o_ref[:, cols])


def _gla_sample(qt, kt, gt, v, sg, s0, gn):
    bs = s0.shape[0]
    feat = pl.BlockSpec((G_HEADS, G_KEY_DIM, bs), lambda b: (0, 0, 0))
    tok = pl.BlockSpec((bs, G_WIDTH), lambda b: (0, 0))
    sspec = pl.BlockSpec((1, G_HEADS, G_KEY_DIM, G_VAL_DIM), lambda b: (b, 0, 0, 0))
    return pl.pallas_call(
        _gla_sample_kernel,
        grid=(bs,),
        in_specs=[feat, feat, feat, tok, tok, sspec, pl.BlockSpec((1, G_VAL_DIM), lambda b: (0, 0))],
        out_specs=(sspec, tok),
        out_shape=(jax.ShapeDtypeStruct(s0.shape, F32), jax.ShapeDtypeStruct((bs, G_WIDTH), F32)),
        compiler_params=pltpu.CompilerParams(dimension_semantics=("arbitrary",)),
        name="gla_sample",
    )(qt, kt, gt, v, sg, s0, gn)


def _rope_tables(pos):
    inv = jnp.power(jnp.float32(ROPE_THETA), -jnp.arange(ROT_HALF, dtype=F32) * (2.0 / ROT_DIM))
    ang = pos.astype(F32)[:, None] * inv[None, :]
    return jnp.cos(ang).T, jnp.sin(ang).T


def _pick_tile(n, pref):
    t = min(n, pref)
    while n % t:
        t //= 2
    return t


def kernel(x_prompt, x_sample, cache_k, cache_v, state_gla, page_table, norm_g, w_in, w_a2, b_a2, gla_norm_g, w_out, final_norm_g):
    bp, lp, d = x_prompt.shape
    bs, ls, _ = x_sample.shape
    depth = norm_g.shape[0]
    past_len = page_table.shape[1] * PAGE_SIZE
    assert ls == 1 and lp % MOBA_BLOCK == 0 and past_len % MOBA_BLOCK == 0
    assert cache_k.shape[2:] == (PAGE_SIZE, A_HEADS, A_HEAD_DIM)

    tabs_p = _rope_tables(jnp.arange(lp, dtype=jnp.int32))
    tabs_s = _rope_tables(jnp.full((bs,), past_len, dtype=jnp.int32))
    kc_t = jnp.transpose(cache_k, (0, 1, 3, 4, 2))
    vc_t = jnp.transpose(cache_v, (0, 1, 3, 4, 2))

    tm_p = _pick_tile(lp, 512)
    tm_o = _pick_tile(lp, 1024)
    tt = _pick_tile(lp, 512)
    hp = x_prompt.reshape(bp * lp, d)
    hs = x_sample.reshape(bs, d)
    fg = final_norm_g.reshape(1, d)
    sp_l, ks_l, vs_l, ss_l = [], [], [], []
    kv_prompt = None
    rows = A_WIDTH
    g_lo = 4 * rows
    g_hi = g_lo + 2 * G_KEY_WIDTH + 2 * G_WIDTH
    for l in range(depth):
        wt = jnp.transpose(w_in[l])
        w_t = wt[0:4 * rows].astype(BF16)
        a1_t = jnp.pad(wt[g_hi:], ((0, LANES - G_GATE_RANK), (0, 0)))
        w_row = jnp.transpose(jnp.concatenate([wt[g_lo:g_hi], a1_t], axis=0)).astype(BF16)
        wa2 = jnp.pad(w_a2[l], ((0, LANES - G_GATE_RANK), (0, 0))).astype(BF16)
        ba2 = b_a2[l].reshape(1, G_KEY_WIDTH)
        ng = norm_g[l].reshape(1, d)
        gn = gla_norm_g[l].reshape(1, G_VAL_DIM)
        wo = w_out[l].astype(BF16)
        last = l == depth - 1

        qt_s, kt_s, vt_s, sgt_s, qg_s, kg_s, vg_s, sgg_s, lf_s = _proj_in(
            hs, ng, w_row, w_t, wa2, ba2, tabs_s, 1, bs, 0, 1)
        a_feat = lambda a: a.reshape(A_HEADS, A_HEAD_DIM, bs)

        qt, kt_all, vt_all, sgt, qg, kg, vg, sgg, lf = _proj_in(
            hp, ng, w_row, w_t, wa2, ba2, tabs_p, bp, tm_p, l, depth, kv_prompt)
        kv_prompt = (kt_all, vt_all)
        moba_steps = bp * (A_WIDTH // (MOBA_HEADS_PER_STEP * A_HEAD_DIM)) * (lp // MOBA_BLOCK + 1)
        pps = _score_pages_per_step(page_table.shape[1], bs, moba_steps)
        if pps is None:
            oa = _moba_prompt(qt, kt_all, vt_all, sgt, l)
            pn, idx, pown = _sample_score(page_table, a_feat(qt_s), a_feat(kt_s), kc_t, l)
        else:
            oa, pn, idx, pown = _moba_prompt(
                qt, kt_all, vt_all, sgt, l, (page_table, a_feat(qt_s), a_feat(kt_s), kc_t, pps))
        og, s_fin = _gla_prompt(qg.reshape(bp, lp, -1), kg.reshape(bp, lp, -1), vg.reshape(bp, lp, -1),
                                lf.reshape(bp, lp, -1), sgg.reshape(bp, lp, -1), gn, tt)
        hp = _proj_out(oa.reshape(bp * lp, A_WIDTH), og.reshape(bp * lp, G_WIDTH), hp, wo, fg, tm_o, last)
        sp_l.append(s_fin)

        qg, kg, vg, sgg, lf, sgt = qg_s, kg_s, vg_s, sgg_s, lf_s, sgt_s
        nsel = min(MOBA_TOPK, page_table.shape[1] // PAGES_PER_BLOCK + 1)
        idx_s = idx[:, :, :nsel].reshape(bs, A_HEADS * nsel)
        oa_s = _sample_pv(page_table, idx_s, pn, pown, a_feat(vt_s), a_feat(sgt), vc_t, l)
        g_feat = lambda a: jnp.transpose(a.astype(F32)).reshape(G_HEADS, G_KEY_DIM, bs)
        s_new, og_s = _gla_sample(g_feat(qg), g_feat(kg), g_feat(lf), vg.astype(F32), sgg.astype(F32),
                                  state_gla[l], gn)
        hs = _proj_out(oa_s.reshape(A_WIDTH, bs), og_s, hs, wo, fg, bs, last, a_feature_major=True)
        ks_l.append(jnp.transpose(kt_s[0, 0]).reshape(bs, 1, A_HEADS, A_HEAD_DIM))
        vs_l.append(jnp.transpose(vt_s[0, 0]).reshape(bs, 1, A_HEADS, A_HEAD_DIM))
        ss_l.append(s_new)

    kt_all, vt_all = kv_prompt
    to_cache_layout = lambda a: jnp.transpose(a.reshape(depth, bp, A_HEADS, A_HEAD_DIM, lp), (0, 1, 4, 2, 3))
    return (hp.reshape(bp, lp, d), hs.reshape(bs, ls, d), to_cache_layout(kt_all), to_cache_layout(vt_all),
            jnp.stack(sp_l), jnp.stack(ks_l), jnp.stack(vs_l), jnp.stack(ss_l))
```

```python
import functools

import jax
import jax.numpy as jnp
from jax import lax
from jax.experimental import pallas as pl
from jax.experimental.pallas import tpu as pltpu

A_HEADS = 8
A_HEAD_DIM = 64
A_WIDTH = A_HEADS * A_HEAD_DIM
ROT_DIM = A_HEAD_DIM // 4
ROT_HALF = ROT_DIM // 2
ROPE_THETA = 500000.0
MOBA_BLOCK = 256
MOBA_TOPK = 3
G_HEADS = 4
G_KEY_DIM = 64
G_VAL_DIM = 128
G_KEY_WIDTH = G_HEADS * G_KEY_DIM
G_WIDTH = G_HEADS * G_VAL_DIM
G_GATE_RANK = 16
G_GATE_NORM = 16.0
G_CHUNK = 32
NORM_EPS = 1e-6
PAGE_SIZE = 128
PAGES_PER_BLOCK = MOBA_BLOCK // PAGE_SIZE

LANES = 128
VMEM_LIMIT_BYTES = 56 * 1024 * 1024

F32 = jnp.float32
BF16 = jnp.bfloat16
NEG_INF = float("-inf")
MASK_BIAS = -1e30
MOBA_HEADS_PER_STEP = 4
MOBA_BLOCKS_PER_ITERATION = 4
SAMPLE_PAGES_IN_FLIGHT = 24
V_PAD_ROWS = 16

_NT = (((1,), (1,)), ((), ()))
_TN = (((0,), (0,)), ((), ()))


def _dot(a, b):
    return jnp.dot(a, b, preferred_element_type=F32)


def _dot_nt(a, b):
    return lax.dot_general(a, b, _NT, preferred_element_type=F32)


def _dot_tn(a, b):
    return lax.dot_general(a, b, _TN, preferred_element_type=F32)


def _split3(x):
    x1 = x.astype(BF16)
    r1 = x - x1.astype(F32)
    x2 = r1.astype(BF16)
    x3 = (r1 - x2.astype(F32)).astype(BF16)
    return x1, x2, x3


def _lane_column(x, lane):
    pick = lax.broadcasted_iota(jnp.int32, (1,) * (x.ndim - 1) + (x.shape[-1],), x.ndim - 1) == lane
    return jnp.sum(jnp.where(pick, x.astype(F32), 0.0), axis=-1, keepdims=True)


def _proj_in_kernel(*refs, n_alias, layer, depth):
    x_ref, g_ref, w_ref, wt_ref, wa2_ref, ba2_ref, ct_ref, st_ref = refs[:8]
    qt_ref, kt_ref, vt_ref, sgt_ref, qg_ref, kg_ref, vg_ref, sgg_ref, lf_ref = refs[8 + n_alias:]
    if n_alias == 0:
        for other in range(depth):
            if other != layer:
                kt_ref[other] = jnp.zeros(kt_ref.shape[1:], kt_ref.dtype)
                vt_ref[other] = jnp.zeros(vt_ref.shape[1:], vt_ref.dtype)
        kt_ref, vt_ref = kt_ref.at[layer], vt_ref.at[layer]
    x = x_ref[...]
    ms = jnp.mean(x * x, axis=-1, keepdims=True)
    h = (x * lax.rsqrt(ms + NORM_EPS) * g_ref[...]).astype(BF16)

    def proj(lo, hi):
        return _dot(h, w_ref[:, lo:hi])

    o = 0
    qg_ref[...] = proj(o, o + G_KEY_WIDTH).astype(qg_ref.dtype)
    o += G_KEY_WIDTH
    kg_ref[...] = proj(o, o + G_KEY_WIDTH).astype(kg_ref.dtype)
    o += G_KEY_WIDTH
    vg_ref[...] = proj(o, o + G_WIDTH).astype(vg_ref.dtype)
    o += G_WIDTH
    zg = proj(o, o + G_WIDTH)
    sgg_ref[...] = (zg * jax.nn.sigmoid(zg)).astype(sgg_ref.dtype)
    o += G_WIDTH
    a1 = proj(o, o + LANES)
    la = _dot(a1.astype(BF16), wa2_ref[...]) + ba2_ref[...]
    lf_ref[...] = -(jnp.maximum(-la, 0.0) + jnp.log1p(jnp.exp(-jnp.abs(la)))) * (1.0 / G_GATE_NORM)

    zt = _dot_nt(wt_ref[...], h)
    ct, st = ct_ref[...], st_ref[...]
    q_scale = A_HEAD_DIM ** -0.5
    for hd in range(A_HEADS):
        b0 = hd * A_HEAD_DIM
        x1 = zt[b0:b0 + ROT_HALF]
        x2 = zt[b0 + ROT_HALF:b0 + ROT_DIM]
        qh = jnp.concatenate([x1 * ct - x2 * st, x2 * ct + x1 * st, zt[b0 + ROT_DIM:b0 + A_HEAD_DIM]], axis=0)
        qt_ref[b0:b0 + A_HEAD_DIM, :] = (qh * q_scale).astype(qt_ref.dtype)
        k0 = b0 + A_WIDTH
        x1 = zt[k0:k0 + ROT_HALF]
        x2 = zt[k0 + ROT_HALF:k0 + ROT_DIM]
        kt_ref[b0:b0 + ROT_HALF, :] = x1 * ct - x2 * st
        kt_ref[b0 + ROT_HALF:b0 + ROT_DIM, :] = x2 * ct + x1 * st
        kt_ref[b0 + ROT_DIM:b0 + A_HEAD_DIM, :] = zt[k0 + ROT_DIM:k0 + A_HEAD_DIM]
    vt_ref[...] = zt[2 * A_WIDTH:3 * A_WIDTH]
    zg = zt[3 * A_WIDTH:4 * A_WIDTH]
    sgt_ref[...] = (zg * jax.nn.sigmoid(zg)).astype(sgt_ref.dtype)


def _proj_in(x, norm_g, w_row, w_t, wa2, ba2, tabs, n_batch, tm, layer, depth, kv_prev=None):
    n, d = x.shape
    seq = n // n_batch
    nl = seq // tm
    ct_tab, st_tab = tabs
    row = lambda i: (i, 0)
    const = lambda i: (0, 0)
    feat_shape = jax.ShapeDtypeStruct((n_batch, A_WIDTH, seq), BF16)
    kv_shape = jax.ShapeDtypeStruct((depth, n_batch, A_WIDTH, seq), F32)
    out_shapes = (
        feat_shape,
        kv_shape, kv_shape,
        feat_shape,
        jax.ShapeDtypeStruct((n, G_KEY_WIDTH), BF16),
        jax.ShapeDtypeStruct((n, G_KEY_WIDTH), BF16),
        jax.ShapeDtypeStruct((n, G_WIDTH), BF16),
        jax.ShapeDtypeStruct((n, G_WIDTH), BF16),
        jax.ShapeDtypeStruct((n, G_KEY_WIDTH), F32),
    )
    feat_spec = pl.BlockSpec((None, A_WIDTH, tm), lambda i: (i // nl, 0, i % nl))
    if kv_prev is None:
        kv_spec = pl.BlockSpec((depth, None, A_WIDTH, tm), lambda i: (0, i // nl, 0, i % nl))
    else:
        kv_spec = pl.BlockSpec((None, None, A_WIDTH, tm), lambda i: (layer, i // nl, 0, i % nl))
    in_specs = [
        pl.BlockSpec((tm, d), row),
        pl.BlockSpec((1, d), const),
        pl.BlockSpec(w_row.shape, const),
        pl.BlockSpec(w_t.shape, const),
        pl.BlockSpec((LANES, G_KEY_WIDTH), const),
        pl.BlockSpec((1, G_KEY_WIDTH), const),
        pl.BlockSpec((ROT_HALF, tm), lambda i: (0, i % nl)),
        pl.BlockSpec((ROT_HALF, tm), lambda i: (0, i % nl)),
    ]
    args = [x, norm_g, w_row, w_t, wa2, ba2, ct_tab, st_tab]
    aliases = {}
    if kv_prev is not None:
        in_specs += [pl.BlockSpec(memory_space=pl.ANY)] * 2
        aliases = {len(args): 1, len(args) + 1: 2}
        args += list(kv_prev)
    return pl.pallas_call(
        functools.partial(_proj_in_kernel, n_alias=len(aliases), layer=layer, depth=depth),
        grid=(n // tm,),
        in_specs=in_specs,
        out_specs=(
            feat_spec, kv_spec, kv_spec, feat_spec,
            pl.BlockSpec((tm, G_KEY_WIDTH), row), pl.BlockSpec((tm, G_KEY_WIDTH), row),
            pl.BlockSpec((tm, G_WIDTH), row), pl.BlockSpec((tm, G_WIDTH), row),
            pl.BlockSpec((tm, G_KEY_WIDTH), row),
        ),
        out_shape=out_shapes,
        input_output_aliases=aliases,
        compiler_params=pltpu.CompilerParams(dimension_semantics=("arbitrary",), vmem_limit_bytes=VMEM_LIMIT_BYTES),
        name="proj_in",
    )(*args)


def _proj_out_kernel(a_ref, g_ref, x_ref, w_ref, fg_ref, o_ref, *, final, a_feature_major):
    half = g_ref.shape[1]
    a = a_ref[...].astype(BF16)
    ya = _dot_tn(a, w_ref[0:half, :]) if a_feature_major else _dot(a, w_ref[0:half, :])
    xo = x_ref[...] + (ya + _dot(g_ref[...].astype(BF16), w_ref[half:, :]))
    if final:
        ms = jnp.mean(xo * xo, axis=-1, keepdims=True)
        xo = xo * lax.rsqrt(ms + NORM_EPS) * fg_ref[...]
    o_ref[...] = xo


def _proj_out(a, g, x, w, fg, tm, final, a_feature_major=False):
    n, d = x.shape
    row = lambda i: (i, 0)
    const = lambda i: (0, 0)
    a_spec = pl.BlockSpec((a.shape[0], tm), lambda i: (0, i)) if a_feature_major else pl.BlockSpec((tm, a.shape[1]), row)
    return pl.pallas_call(
        functools.partial(_proj_out_kernel, final=final, a_feature_major=a_feature_major),
        grid=(n // tm,),
        in_specs=[
            a_spec,
            pl.BlockSpec((tm, g.shape[1]), row),
            pl.BlockSpec((tm, d), row),
            pl.BlockSpec(w.shape, const),
            pl.BlockSpec((1, d), const),
        ],
        out_specs=pl.BlockSpec((tm, d), row),
        out_shape=jax.ShapeDtypeStruct((n, d), F32),
        compiler_params=pltpu.CompilerParams(dimension_semantics=("arbitrary",), vmem_limit_bytes=VMEM_LIMIT_BYTES),
        name="proj_out",
    )(a, g, x, w, fg)


def _moba_prompt_kernel(*refs, nb, nsel, unroll, side):
    if side is None:
        qt_ref, kt_ref, vt_ref, sgt_ref, o_ref, ka_ref, va_ref, km_ref, qa_ref, s_ref, acc_ref, m_ref = refs
    else:
        (pt_ref, qt_ref, kt_ref, vt_ref, sgt_ref, qs_ref, ks_ref, kc_ref, o_ref, pn_ref, idx_ref, pown_ref,
         ka_ref, va_ref, km_ref, qa_ref, s_ref, acc_ref, m_ref, buf_ref, sem_ref, qb_ref, sc_ref) = refs
    qi = pl.program_id(2)
    blk = MOBA_BLOCK
    dh = A_HEAD_DIM
    heads = qt_ref.shape[0] // dh
    per_group = LANES // dh

    if side is not None:
        layer, npages, bs, pps, nsel_s = side
        sps = npages // pps
        step = (pl.program_id(0) * pl.num_programs(1) + pl.program_id(1)) * pl.num_programs(2) + qi
        seq = step // sps
        part = step - seq * sps
        total = bs * npages
        half = lax.rem(step, 2) * pps

        def page_copy(g, slot):
            bb = g // npages
            return pltpu.make_async_copy(
                kc_ref.at[layer, pt_ref[bb, g - bb * npages]], buf_ref.at[slot], sem_ref.at[slot])

        @pl.when(step == 0)
        def _prime():
            for i in range(2 * pps):
                page_copy(i, i).start()

        @pl.when(seq < bs)
        def _score_pages():
            @pl.when(part == 0)
            def _new_sequence():
                qcol = _lane_column(qs_ref[...], seq)
                for h in range(A_HEADS):
                    qb_ref[h] = jnp.broadcast_to(qcol[h], (dh, PAGE_SIZE))

            g0 = step * pps
            for i in range(pps):
                page_copy(g0 + i, half + i).wait()
            for h in range(A_HEADS):
                qb = qb_ref[h]
                for i in range(pps):
                    sc_ref[part * pps + i, h:h + 1, :] = jnp.sum(buf_ref[half + i, h] * qb, axis=0, keepdims=True)
            for i in range(pps):
                @pl.when(g0 + i + 2 * pps < total)
                def _refill():
                    page_copy(g0 + i + 2 * pps, half + i).start()

            @pl.when(part == sps - 1)
            def _finish_sequence():
                _sample_choose_and_weigh(sc_ref, _lane_column(qs_ref[...], seq), _lane_column(ks_ref[...], seq),
                                         pn_ref, idx_ref, pown_ref, npages, nsel_s)

    @pl.when(qi == 0)
    def _prepare():
        lane = lax.broadcasted_iota(jnp.int32, (blk, LANES), 1)
        ones_rows = jnp.where(lax.broadcasted_iota(jnp.int32, (V_PAD_ROWS, blk), 0) == 0, 1.0, 0.0)
        for j in range(nb):
            kj = kt_ref[:, j * blk:(j + 1) * blk].T
            km_ref[j:j + 1, :] = jnp.mean(kj, axis=0, keepdims=True)
            onehot = jnp.where(lane == dh + j, 1.0, 0.0)
            for hh in range(heads):
                grp, sub = divmod(hh, per_group)
                kh = kj[:, grp * LANES:(grp + 1) * LANES]
                if sub:
                    kh = pltpu.roll(kh, LANES - sub * dh, 1)
                ka_ref[hh, j * blk:(j + 1) * blk, :] = jnp.where(lane < dh, kh, onehot).astype(BF16)
                vh = vt_ref[hh * dh:(hh + 1) * dh, j * blk:(j + 1) * blk]
                va_ref[j, hh] = jnp.concatenate([vh, ones_rows], axis=0).astype(BF16)
        m_ref[...] = jnp.zeros_like(m_ref)

    @pl.when(jnp.logical_and(qi == 0, jnp.logical_and(pl.program_id(0) == 0, pl.program_id(1) == 0)))
    def _define_scores():
        s_ref[...] = jnp.zeros_like(s_ref)

    qb = jnp.minimum(qi, nb - 1)
    prev = jnp.maximum(qi - 1, 0)
    m_old = [m_ref[hh] for hh in range(heads)]

    for hh in range(heads):
        acc_ref[hh] = _dot(va_ref[prev, hh], jnp.exp(s_ref[hh, nb] - m_old[hh]).astype(BF16))

    rowf = lax.broadcasted_iota(jnp.int32, (nb, blk), 0).astype(F32)
    past = rowf < qb.astype(F32)
    causal = lax.broadcasted_iota(jnp.int32, (blk, blk), 0) <= lax.broadcasted_iota(jnp.int32, (blk, blk), 1)
    pad = jnp.zeros((LANES - dh - nb, blk), BF16)
    r_own = pl.multiple_of(qb * blk, blk)

    m_run, qd = [], []
    for hh in range(heads):
        qh = qt_ref[hh * dh:(hh + 1) * dh, :]
        km = km_ref[:, hh * dh:(hh + 1) * dh]
        km1 = km.astype(BF16)
        km2 = (km - km1.astype(F32)).astype(BF16)
        g = jnp.where(past, _dot(km1, qh) + _dot(km2, qh), NEG_INF)
        sel = jnp.zeros((nb, blk), F32)
        for _ in range(nsel):
            m = jnp.max(g, axis=0, keepdims=True)
            idx = jnp.min(jnp.where(g == m, rowf, float(nb)), axis=0, keepdims=True)
            pick = jnp.logical_and(rowf == idx, m > NEG_INF)
            sel = jnp.where(pick, 1.0, sel)
            g = jnp.where(pick, NEG_INF, g)
        bias = jnp.where(sel > 0.0, 0.0, MASK_BIAS).astype(BF16)
        qa_ref[hh] = jnp.concatenate([qh, bias, pad], axis=0)
        qd.append(jnp.concatenate([qh, jnp.zeros((LANES - dh, blk), BF16)], axis=0))

    for hh in range(heads):
        s = jnp.where(causal, _dot(ka_ref[hh, pl.ds(r_own, blk), :], qd[hh]), MASK_BIAS)
        s_ref[hh, nb] = s
        m_run.append(jnp.max(s, axis=0, keepdims=True))

    def blocks(j0, count, ms):
        ms = list(ms)
        for hh in range(heads):
            part = None
            for u in range(count):
                j = j0 + u
                p = jnp.where(j < prev, jnp.exp(s_ref[hh, j] - m_old[hh]), 0.0).astype(BF16)
                t = _dot(va_ref[j, hh], p)
                part = t if part is None else part + t
                r0 = pl.multiple_of(j * blk, blk)
                s = _dot(ka_ref[hh, pl.ds(r0, blk), :], qa_ref[hh])
                s_ref[hh, j] = s
                ms[hh] = jnp.maximum(ms[hh], jnp.max(s, axis=0, keepdims=True))
            acc_ref[hh] += part
        return tuple(ms)

    m_fin = tuple(m_run)
    done = 0
    width = unroll
    while width >= 1:
        trips = lax.shift_right_logical(qi - done, width.bit_length() - 1)
        m_fin = lax.fori_loop(0, trips, functools.partial(
            lambda gi, ms, j0, w: blocks(j0 + gi * w, w, ms), j0=done, w=width), m_fin)
        done = done + trips * width
        width //= 2
    for hh in range(heads):
        m_ref[hh] = m_fin[hh]

    ot = jnp.concatenate([acc_ref[hh, 0:dh, :] / acc_ref[hh, dh:dh + 1, :] for hh in range(heads)], axis=0)
    o_ref[...] = (ot * sgt_ref[...].astype(F32)).T.astype(o_ref.dtype)


def _score_pages_per_step(npages, bs, steps):
    for pps in range(1, npages + 1):
        if npages % pps == 0 and bs * (npages // pps) <= steps:
            return pps
    return None


def _moba_prompt(qt, kt_all, vt_all, sgt, layer, decode=None):
    b, width, seq = qt.shape
    nb = seq // MOBA_BLOCK
    nsel = min(MOBA_TOPK, nb)
    heads = MOBA_HEADS_PER_STEP
    hw = heads * A_HEAD_DIM
    assert A_HEAD_DIM + nb <= LANES and width % hw == 0
    unroll = MOBA_BLOCKS_PER_ITERATION
    n_hp = width // hw
    grid = (b, n_hp, nb + 1)
    scratch = [
        pltpu.VMEM((heads, seq, LANES), BF16),
        pltpu.VMEM((nb, heads, A_HEAD_DIM + V_PAD_ROWS, MOBA_BLOCK), BF16),
        pltpu.VMEM((nb, hw), F32),
        pltpu.VMEM((heads, LANES, MOBA_BLOCK), BF16),
        pltpu.VMEM((heads, nb + 1, MOBA_BLOCK, MOBA_BLOCK), F32),
        pltpu.VMEM((heads, A_HEAD_DIM + V_PAD_ROWS, MOBA_BLOCK), F32),
        pltpu.VMEM((heads, 1, MOBA_BLOCK), F32),
    ]
    params = pltpu.CompilerParams(
        dimension_semantics=("arbitrary", "arbitrary", "arbitrary"), vmem_limit_bytes=VMEM_LIMIT_BYTES)
    q_map = lambda bi, hp, qi, *_: (bi, hp, jnp.minimum(qi, nb - 1))
    g_map = lambda bi, hp, qi, *_: (bi, hp, jnp.maximum(qi - 1, 0))
    kv_map = lambda bi, hp, qi, *_: (layer, bi, hp, 0)
    o_map = lambda bi, hp, qi, *_: (bi, jnp.maximum(qi - 1, 0), hp)
    in_specs = [pl.BlockSpec((None, hw, MOBA_BLOCK), q_map), pl.BlockSpec((None, None, hw, seq), kv_map),
                pl.BlockSpec((None, None, hw, seq), kv_map), pl.BlockSpec((None, hw, MOBA_BLOCK), g_map)]
    o_spec = pl.BlockSpec((None, MOBA_BLOCK, hw), o_map)
    o_shape = jax.ShapeDtypeStruct((b, seq, width), BF16)
    if decode is None:
        return pl.pallas_call(
            functools.partial(_moba_prompt_kernel, nb=nb, nsel=nsel, unroll=unroll, side=None),
            grid=grid, in_specs=in_specs, out_specs=o_spec, out_shape=o_shape, scratch_shapes=scratch,
            compiler_params=params, name="moba_prompt",
        )(qt, kt_all, vt_all, sgt)

    page_table, q_s, k_s, cache_t, pps = decode
    bs, npages = page_table.shape
    sps = npages // pps
    nsel_s = min(MOBA_TOPK, npages // PAGES_PER_BLOCK + 1)
    seq_of = lambda bi, hp, qi: jnp.minimum(((bi * n_hp + hp) * (nb + 1) + qi) // sps, bs - 1)
    feat = pl.BlockSpec((A_HEADS, A_HEAD_DIM, bs), lambda bi, hp, qi, pt: (0, 0, 0))
    grid_spec = pltpu.PrefetchScalarGridSpec(
        num_scalar_prefetch=1,
        grid=grid,
        in_specs=in_specs + [feat, feat, pl.BlockSpec(memory_space=pl.ANY)],
        out_specs=(
            o_spec,
            pl.BlockSpec((1, npages, A_HEADS, PAGE_SIZE), lambda bi, hp, qi, pt: (seq_of(bi, hp, qi), 0, 0, 0)),
            pl.BlockSpec((1, A_HEADS, LANES), lambda bi, hp, qi, pt: (seq_of(bi, hp, qi), 0, 0)),
            pl.BlockSpec((1, A_HEADS, LANES), lambda bi, hp, qi, pt: (seq_of(bi, hp, qi), 0, 0)),
        ),
        scratch_shapes=scratch + [
            pltpu.VMEM((2 * pps, A_HEADS, A_HEAD_DIM, PAGE_SIZE), F32),
            pltpu.SemaphoreType.DMA((2 * pps,)),
            pltpu.VMEM((A_HEADS, A_HEAD_DIM, PAGE_SIZE), F32),
            pltpu.VMEM((npages, A_HEADS, PAGE_SIZE), F32),
        ],
    )
    return pl.pallas_call(
        functools.partial(_moba_prompt_kernel, nb=nb, nsel=nsel, unroll=unroll,
                          side=(layer, npages, bs, pps, nsel_s)),
        grid_spec=grid_spec,
        out_shape=(
            o_shape,
            jax.ShapeDtypeStruct((bs, npages, A_HEADS, PAGE_SIZE), F32),
            jax.ShapeDtypeStruct((bs, A_HEADS, LANES), jnp.int32),
            jax.ShapeDtypeStruct((bs, A_HEADS, LANES), F32),
        ),
        compiler_params=params,
        name="moba_prompt",
    )(page_table, qt, kt_all, vt_all, sgt, q_s, k_s, cache_t)


def _gla_prompt_kernel(q_ref, k_ref, v_ref, lf_ref, sg_ref, gn_ref, o_ref, s_ref, st_ref, *, nt):
    t = pl.program_id(1)
    tt = q_ref.shape[1]
    ck = G_CHUNK
    dk, dv = G_KEY_DIM, G_VAL_DIM
    nh = G_HEADS

    @pl.when(t == 0)
    def _init():
        st_ref[...] = jnp.zeros_like(st_ref)

    g1, g2, g3 = _split3(lf_ref[0])
    row = lax.broadcasted_iota(jnp.int32, (tt, tt), 0)
    col = lax.broadcasted_iota(jnp.int32, (tt, tt), 1)
    ck_shift = ck.bit_length() - 1
    dk_shift = dk.bit_length() - 1
    same = lax.shift_right_logical(row, ck_shift) == lax.shift_right_logical(col, ck_shift)
    lower = jnp.where(jnp.logical_and(same, col <= row), 1.0, 0.0).astype(BF16)
    upper = jnp.where(jnp.logical_and(same, col > row), 1.0, 0.0).astype(BF16)
    bcum = _dot(lower, g1) + _dot(lower, g2) + _dot(lower, g3)
    rest = _dot(upper, g1) + _dot(upper, g2) + _dot(upper, g3)

    qf = q_ref[0].astype(F32)
    kf = k_ref[0].astype(F32)
    qt = qf * jnp.exp(bcum) * (dk ** -0.5)
    kt = kf * jnp.exp(-bcum)
    kd = kf * jnp.exp(rest)

    head_of_lane = lax.shift_right_logical(lax.broadcasted_iota(jnp.int32, (ck, nh * dk), 1), dk_shift)
    own = [head_of_lane == hd for hd in range(nh)]
    ri = lax.broadcasted_iota(jnp.int32, (nh * ck, nh * ck), 0)
    ci = lax.broadcasted_iota(jnp.int32, (nh * ck, nh * ck), 1)
    causal = jnp.bitwise_and(ri, ck - 1) >= jnp.bitwise_and(ci, ck - 1)
    gn = gn_ref[...]
    st = st_ref[...]

    def stack(x):
        return jnp.concatenate([jnp.where(own[hd], x, 0.0) for hd in range(nh)], axis=0).astype(BF16)

    nc = tt // ck
    q4s, o_intra, kvs = [], [], []
    for c in range(nc):
        r0, r1 = c * ck, (c + 1) * ck
        q4, k4, kd4 = stack(qt[r0:r1]), stack(kt[r0:r1]), stack(kd[r0:r1])
        v4 = jnp.concatenate([v_ref[0, r0:r1, hd * dv:(hd + 1) * dv] for hd in range(nh)], axis=0)
        a = jnp.where(causal, _dot_nt(q4, k4), 0.0)
        q4s.append(q4)
        o_intra.append(_dot(a.astype(BF16), v4))
        kvs.append(_dot_tn(v4, kd4))

    for c in range(nc):
        r0, r1 = c * ck, (c + 1) * ck
        o = o_intra[c] + _dot_nt(q4s[c], st.astype(BF16))
        ms = jnp.mean(o * o, axis=-1, keepdims=True)
        on = o * lax.rsqrt(ms + NORM_EPS) * gn
        for hd in range(nh):
            sg = sg_ref[0, r0:r1, hd * dv:(hd + 1) * dv].astype(F32)
            o_ref[0, r0:r1, hd * dv:(hd + 1) * dv] = (on[hd * ck:(hd + 1) * ck] * sg).astype(o_ref.dtype)
        st = st * jnp.exp(bcum[r1 - 1:r1, :]) + kvs[c]

    st_ref[...] = st

    @pl.when(t == nt - 1)
    def _emit():
        per_group = LANES // dk
        for grp in range(nh // per_group):
            s_grp = st[:, grp * LANES:(grp + 1) * LANES].T
            for hh in range(per_group):
                s_ref[0, grp * per_group + hh] = s_grp[hh * dk:(hh + 1) * dk, :]


def _gla_prompt(qg, kg, vg, lf, sg, gn, tt):
    b, seq, _ = qg.shape
    nt = seq // tt
    kspec = pl.BlockSpec((1, tt, G_KEY_WIDTH), lambda bi, ti: (bi, ti, 0))
    vspec = pl.BlockSpec((1, tt, G_WIDTH), lambda bi, ti: (bi, ti, 0))
    return pl.pallas_call(
        functools.partial(_gla_prompt_kernel, nt=nt),
        grid=(b, nt),
        in_specs=[kspec, kspec, vspec, kspec, vspec, pl.BlockSpec((1, G_VAL_DIM), lambda bi, ti: (0, 0))],
        out_specs=(vspec, pl.BlockSpec((1, G_HEADS, G_KEY_DIM, G_VAL_DIM), lambda bi, ti: (bi, 0, 0, 0))),
        out_shape=(jax.ShapeDtypeStruct((b, seq, G_WIDTH), BF16),
                   jax.ShapeDtypeStruct((b, G_HEADS, G_KEY_DIM, G_VAL_DIM), F32)),
        scratch_shapes=[pltpu.VMEM((G_VAL_DIM, G_KEY_WIDTH), F32)],
        compiler_params=pltpu.CompilerParams(
            dimension_semantics=("arbitrary", "arbitrary"), vmem_limit_bytes=VMEM_LIMIT_BYTES),
        name="gla_prompt",
    )(qg, kg, vg, lf, sg, gn)


def _sample_score_kernel(pt_ref, qt_ref, kt_ref, kc_ref, pn_ref, idx_ref, pown_ref,
                         buf_ref, sem_ref, qb_ref, sc_ref, *, layer, npages, nbuf, bs, nsel):
    b = pl.program_id(0)
    total = bs * npages
    nblk = npages // PAGES_PER_BLOCK
    dh = A_HEAD_DIM

    def page_copy(g, slot):
        bb = g // npages
        return pltpu.make_async_copy(kc_ref.at[layer, pt_ref[bb, g - bb * npages]], buf_ref.at[slot], sem_ref.at[slot])

    @pl.when(b == 0)
    def _prime():
        for s in range(nbuf):
            page_copy(s, s).start()

    qcol = _lane_column(qt_ref[...], b)
    kcol = _lane_column(kt_ref[...], b)
    for h in range(A_HEADS):
        qb_ref[h] = jnp.broadcast_to(qcol[h], (dh, PAGE_SIZE))

    def page_body(p, _):
        g = b * npages + p
        slot = lax.rem(g, nbuf)
        page_copy(g, slot).wait()
        for h in range(A_HEADS):
            sc_ref[p, h:h + 1, :] = jnp.sum(buf_ref[slot, h] * qb_ref[h], axis=0, keepdims=True)

        @pl.when(g + nbuf < total)
        def _next():
            page_copy(g + nbuf, slot).start()

        return 0

    lax.fori_loop(0, npages, page_body, 0)
    _sample_choose_and_weigh(sc_ref, qcol, kcol, pn_ref, idx_ref, pown_ref, npages, nsel)


def _sample_choose_and_weigh(sc_ref, qcol, kcol, pn_ref, idx_ref, pown_ref, npages, nsel):
    nblk = npages // PAGES_PER_BLOCK
    lane = lax.broadcasted_iota(jnp.int32, (A_HEADS, LANES), 1).astype(F32)
    sub = lax.broadcasted_iota(jnp.int32, (A_HEADS, LANES), 0)

    gate = jnp.full((A_HEADS, LANES), NEG_INF, F32)
    for j in range(nblk):
        sblk = sc_ref[PAGES_PER_BLOCK * j]
        for pp in range(1, PAGES_PER_BLOCK):
            sblk = sblk + sc_ref[PAGES_PER_BLOCK * j + pp]
        gj = jnp.sum(sblk, axis=1, keepdims=True) * (1.0 / MOBA_BLOCK)
        gate = jnp.where(lane == float(j), gj, gate)

    g = gate
    sel = jnp.zeros((A_HEADS, LANES), F32)
    idx_out = jnp.full((A_HEADS, LANES), -1.0, F32)
    for r in range(nsel):
        m = jnp.max(g, axis=1, keepdims=True)
        idx = jnp.min(jnp.where(g == m, lane, float(LANES)), axis=1, keepdims=True)
        pick = jnp.logical_and(lane == idx, m > NEG_INF)
        sel = jnp.where(pick, 1.0, sel)
        g = jnp.where(pick, NEG_INF, g)
        idx_out = jnp.where(lane == float(r), jnp.where(m > NEG_INF, idx, -1.0), idx_out)
    idx_ref[0] = idx_out.astype(jnp.int32)

    s_own = jnp.zeros((A_HEADS, LANES), F32)
    for h in range(A_HEADS):
        so = jnp.sum(qcol[h] * kcol[h], axis=0, keepdims=True)
        s_own = jnp.where(sub == h, jnp.broadcast_to(so, (A_HEADS, LANES)), s_own)

    masks = [jnp.broadcast_to(sel[:, j:j + 1], (A_HEADS, LANES)) > 0.0 for j in range(nblk)]
    mx = s_own
    for pg in range(npages):
        mx = jnp.maximum(mx, jnp.where(masks[pg // PAGES_PER_BLOCK], sc_ref[pg], NEG_INF))
    m = jnp.max(mx, axis=1, keepdims=True)
    lsum = jnp.zeros((A_HEADS, LANES), F32)
    for pg in range(npages):
        p = jnp.where(masks[pg // PAGES_PER_BLOCK], jnp.exp(sc_ref[pg] - m), 0.0)
        pn_ref[0, pg] = p
        lsum = lsum + p
    p_own = jnp.exp(s_own - m)
    inv = 1.0 / (jnp.sum(lsum, axis=1, keepdims=True) + p_own)
    for pg in range(npages):
        pn_ref[0, pg] = pn_ref[0, pg] * inv
    pown_ref[0] = p_own * inv


def _sample_score(page_table, qt, kt, cache_t, layer, nbuf=SAMPLE_PAGES_IN_FLIGHT):
    bs, npages = page_table.shape
    nblk = npages // PAGES_PER_BLOCK
    nsel = min(MOBA_TOPK, nblk + 1)
    feat = pl.BlockSpec((A_HEADS, A_HEAD_DIM, bs), lambda b, pt: (0, 0, 0))
    grid_spec = pltpu.PrefetchScalarGridSpec(
        num_scalar_prefetch=1,
        grid=(bs,),
        in_specs=[feat, feat, pl.BlockSpec(memory_space=pl.ANY)],
        out_specs=(
            pl.BlockSpec((1, npages, A_HEADS, PAGE_SIZE), lambda b, pt: (b, 0, 0, 0)),
            pl.BlockSpec((1, A_HEADS, LANES), lambda b, pt: (b, 0, 0)),
            pl.BlockSpec((1, A_HEADS, LANES), lambda b, pt: (b, 0, 0)),
        ),
        scratch_shapes=[
            pltpu.VMEM((min(nbuf, npages), A_HEADS, A_HEAD_DIM, PAGE_SIZE), F32),
            pltpu.SemaphoreType.DMA((min(nbuf, npages),)),
            pltpu.VMEM((A_HEADS, A_HEAD_DIM, PAGE_SIZE), F32),
            pltpu.VMEM((npages, A_HEADS, PAGE_SIZE), F32),
        ],
    )
    return pl.pallas_call(
        functools.partial(_sample_score_kernel, layer=layer, npages=npages, nbuf=min(nbuf, npages), bs=bs, nsel=nsel),
        grid_spec=grid_spec,
        out_shape=(
            jax.ShapeDtypeStruct((bs, npages, A_HEADS, PAGE_SIZE), F32),
            jax.ShapeDtypeStruct((bs, A_HEADS, LANES), jnp.int32),
            jax.ShapeDtypeStruct((bs, A_HEADS, LANES), F32),
        ),
        compiler_params=pltpu.CompilerParams(dimension_semantics=("arbitrary",), vmem_limit_bytes=VMEM_LIMIT_BYTES),
        name="sample_score",
    )(page_table, qt, kt, cache_t)


def _sample_pv_kernel(pt_ref, idx_ref, pn_ref, pown_ref, vt_ref, sgt_ref, vc_ref, o_ref,
                      vbuf_ref, sem_ref, *, layer, nsel, bs):
    b = pl.program_id(0)
    slot = lax.rem(b, 2)

    def issue(bb, sl):
        for h in range(A_HEADS):
            for r in range(nsel):
                j = jnp.maximum(idx_ref[bb, h * nsel + r], 0)
                for pp in range(PAGES_PER_BLOCK):
                    pg = pt_ref[bb, PAGES_PER_BLOCK * j + pp]
                    pltpu.make_async_copy(vc_ref.at[layer, pg, h], vbuf_ref.at[sl, h, r, pp], sem_ref.at[sl]).start()

    @pl.when(b == 0)
    def _first():
        issue(0, 0)
        o_ref[...] = jnp.zeros_like(o_ref)

    @pl.when(b + 1 < bs)
    def _prefetch():
        issue(b + 1, 1 - slot)

    for _ in range(A_HEADS * nsel * PAGES_PER_BLOCK):
        pltpu.make_async_copy(vc_ref.at[layer, 0, 0], vbuf_ref.at[slot, 0, 0, 0], sem_ref.at[slot]).wait()

    vcol = _lane_column(vt_ref[...], b)
    sgcol = _lane_column(sgt_ref[...], b)
    mine = lax.broadcasted_iota(jnp.int32, (1, bs), 1) == b
    for h in range(A_HEADS):
        acc = jnp.zeros((A_HEAD_DIM, PAGE_SIZE), F32)
        for r in range(nsel):
            jraw = idx_ref[b, h * nsel + r]
            j = jnp.maximum(jraw, 0)
            w = jnp.where(jraw >= 0, 1.0, 0.0)
            for pp in range(PAGES_PER_BLOCK):
                prow = pn_ref[0, PAGES_PER_BLOCK * j + pp, h:h + 1, :] * w
                acc = acc + vbuf_ref[slot, h, r, pp] * prow
        ocol = jnp.sum(acc, axis=1, keepdims=True) + pown_ref[0, h:h + 1, 0:1] * vcol[h]
        o_ref[h] = jnp.where(mine, jnp.broadcast_to(ocol * sgcol[h], (A_HEAD_DIM, bs)), o_ref[h])


def _sample_pv(page_table, idx, pn, pown, vt, sgt, cache_t, layer):
    bs, npages = page_table.shape
    nsel = idx.shape[1] // A_HEADS
    feat = pl.BlockSpec((A_HEADS, A_HEAD_DIM, bs), lambda b, pt, ix: (0, 0, 0))
    grid_spec = pltpu.PrefetchScalarGridSpec(
        num_scalar_prefetch=2,
        grid=(bs,),
        in_specs=[
            pl.BlockSpec((1, npages, A_HEADS, PAGE_SIZE), lambda b, pt, ix: (b, 0, 0, 0)),
            pl.BlockSpec((1, A_HEADS, LANES), lambda b, pt, ix: (b, 0, 0)),
            feat, feat,
            pl.BlockSpec(memory_space=pl.ANY),
        ],
        out_specs=feat,
        scratch_shapes=[
            pltpu.VMEM((2, A_HEADS, nsel, PAGES_PER_BLOCK, A_HEAD_DIM, PAGE_SIZE), F32),
            pltpu.SemaphoreType.DMA((2,)),
        ],
    )
    return pl.pallas_call(
        functools.partial(_sample_pv_kernel, layer=layer, nsel=nsel, bs=bs),
        grid_spec=grid_spec,
        out_shape=jax.ShapeDtypeStruct((A_HEADS, A_HEAD_DIM, bs), F32),
        compiler_params=pltpu.CompilerParams(dimension_semantics=("arbitrary",), vmem_limit_bytes=VMEM_LIMIT_BYTES),
        name="sample_pv",
    )(page_table, idx, pn, pown, vt, sgt, cache_t)


def _gla_sample_kernel(qt_ref, kt_ref, gt_ref, v_ref, sg_ref, s0_ref, gn_ref, s_ref, o_ref):
    b = pl.program_id(0)
    dv = G_VAL_DIM
    bs = v_ref.shape[0]

    @pl.when(b == 0)
    def _init():
        o_ref[...] = jnp.zeros_like(o_ref)

    qcol = _lane_column(qt_ref[...], b)
    kcol = _lane_column(kt_ref[...], b)
    gcol = _lane_column(gt_ref[...], b)
    mine = lax.broadcasted_iota(jnp.int32, (bs, dv), 0) == b
    for h in range(G_HEADS):
        cols = slice(h * dv, (h + 1) * dv)
        v = jnp.sum(jnp.where(mine, v_ref[:, cols], 0.0), axis=0, keepdims=True)
        sg = jnp.sum(jnp.where(mine, sg_ref[:, cols], 0.0), axis=0, keepdims=True)
        s_new = jnp.exp(gcol[h]) * s0_ref[0, h] + kcol[h] * v
        s_ref[0, h] = s_new
        o = jnp.sum(qcol[h] * s_new, axis=0, keepdims=True) * (G_KEY_DIM ** -0.5)
        ms = jnp.mean(o * o, axis=-1, keepdims=True)
        on = o * lax.rsqrt(ms + NORM_EPS) * gn_ref[...] * sg
        o_ref[:, cols] = jnp.where(mine, jnp.broadcast_to(on, (bs, dv)), o_ref[:, cols])


def _gla_sample(qt, kt, gt, v, sg, s0, gn):
    bs = s0.shape[0]
    feat = pl.BlockSpec((G_HEADS, G_KEY_DIM, bs), lambda b: (0, 0, 0))
    tok = pl.BlockSpec((bs, G_WIDTH), lambda b: (0, 0))
    sspec = pl.BlockSpec((1, G_HEADS, G_KEY_DIM, G_VAL_DIM), lambda b: (b, 0, 0, 0))
    return pl.pallas_call(
        _gla_sample_kernel,
        grid=(bs,),
        in_specs=[feat, feat, feat, tok, tok, sspec, pl.BlockSpec((1, G_VAL_DIM), lambda b: (0, 0))],
        out_specs=(sspec, tok),
        out_shape=(jax.ShapeDtypeStruct(s0.shape, F32), jax.ShapeDtypeStruct((bs, G_WIDTH), F32)),
        compiler_params=pltpu.CompilerParams(dimension_semantics=("arbitrary",)),
        name="gla_sample",
    )(qt, kt, gt, v, sg, s0, gn)


def _rope_tables(pos):
    inv = jnp.power(jnp.float32(ROPE_THETA), -jnp.arange(ROT_HALF, dtype=F32) * (2.0 / ROT_DIM))
    ang = pos.astype(F32)[:, None] * inv[None, :]
    return jnp.cos(ang).T, jnp.sin(ang).T


def _pick_tile(n, pref):
    t = min(n, pref)
    while n % t:
        t //= 2
    return t


def kernel(x_prompt, x_sample, cache_k, cache_v, state_gla, page_table, norm_g, w_in, w_a2, b_a2, gla_norm_g, w_out, final_norm_g):
    bp, lp, d = x_prompt.shape
    bs, ls, _ = x_sample.shape
    depth = norm_g.shape[0]
    past_len = page_table.shape[1] * PAGE_SIZE
    assert ls == 1 and lp % MOBA_BLOCK == 0 and past_len % MOBA_BLOCK == 0
    assert cache_k.shape[2:] == (PAGE_SIZE, A_HEADS, A_HEAD_DIM)

    tabs_p = _rope_tables(jnp.arange(lp, dtype=jnp.int32))
    tabs_s = _rope_tables(jnp.full((bs,), past_len, dtype=jnp.int32))
    kc_t = jnp.transpose(cache_k, (0, 1, 3, 4, 2))
    vc_t = jnp.transpose(cache_v, (0, 1, 3, 4, 2))

    tm_p = _pick_tile(lp, 512)
    tm_o = _pick_tile(lp, 2048)
    tt = _pick_tile(lp, 512)
    hp = x_prompt.reshape(bp * lp, d)
    hs = x_sample.reshape(bs, d)
    fg = final_norm_g.reshape(1, d)
    sp_l, ks_l, vs_l, ss_l = [], [], [], []
    kv_prompt = None
    rows = A_WIDTH
    g_lo = 4 * rows
    g_hi = g_lo + 2 * G_KEY_WIDTH + 2 * G_WIDTH
    for l in range(depth):
        wt = jnp.transpose(w_in[l])
        w_t = wt[0:4 * rows].astype(BF16)
        a1_t = jnp.pad(wt[g_hi:], ((0, LANES - G_GATE_RANK), (0, 0)))
        w_row = jnp.transpose(jnp.concatenate([wt[g_lo:g_hi], a1_t], axis=0)).astype(BF16)
        wa2 = jnp.pad(w_a2[l], ((0, LANES - G_GATE_RANK), (0, 0))).astype(BF16)
        ba2 = b_a2[l].reshape(1, G_KEY_WIDTH)
        ng = norm_g[l].reshape(1, d)
        gn = gla_norm_g[l].reshape(1, G_VAL_DIM)
        wo = w_out[l].astype(BF16)
        last = l == depth - 1

        qt_s, kt_s, vt_s, sgt_s, qg_s, kg_s, vg_s, sgg_s, lf_s = _proj_in(
            hs, ng, w_row, w_t, wa2, ba2, tabs_s, 1, bs, 0, 1)
        a_feat = lambda a: a.reshape(A_HEADS, A_HEAD_DIM, bs)

        qt, kt_all, vt_all, sgt, qg, kg, vg, sgg, lf = _proj_in(
            hp, ng, w_row, w_t, wa2, ba2, tabs_p, bp, tm_p, l, depth, kv_prompt)
        kv_prompt = (kt_all, vt_all)
        moba_steps = bp * (A_WIDTH // (MOBA_HEADS_PER_STEP * A_HEAD_DIM)) * (lp // MOBA_BLOCK + 1)
        pps = _score_pages_per_step(page_table.shape[1], bs, moba_steps)
        if pps is None:
            oa = _moba_prompt(qt, kt_all, vt_all, sgt, l)
            pn, idx, pown = _sample_score(page_table, a_feat(qt_s), a_feat(kt_s), kc_t, l)
        else:
            oa, pn, idx, pown = _moba_prompt(
                qt, kt_all, vt_all, sgt, l, (page_table, a_feat(qt_s), a_feat(kt_s), kc_t, pps))
        og, s_fin = _gla_prompt(qg.reshape(bp, lp, -1), kg.reshape(bp, lp, -1), vg.reshape(bp, lp, -1),
                                lf.reshape(bp, lp, -1), sgg.reshape(bp, lp, -1), gn, tt)
        hp = _proj_out(oa.reshape(bp * lp, A_WIDTH), og.reshape(bp * lp, G_WIDTH), hp, wo, fg, tm_o, last)
        sp_l.append(s_fin)

        qg, kg, vg, sgg, lf, sgt = qg_s, kg_s, vg_s, sgg_s, lf_s, sgt_s
        nsel = min(MOBA_TOPK, page_table.shape[1] // PAGES_PER_BLOCK + 1)
        idx_s = idx[:, :, :nsel].reshape(bs, A_HEADS * nsel)
        oa_s = _sample_pv(page_table, idx_s, pn, pown, a_feat(vt_s), a_feat(sgt), vc_t, l)
        g_feat = lambda a: jnp.transpose(a.astype(F32)).reshape(G_HEADS, G_KEY_DIM, bs)
        s_new, og_s = _gla_sample(g_feat(qg), g_feat(kg), g_feat(lf), vg.astype(F32), sgg.astype(F32),
                                  state_gla[l], gn)
        hs = _proj_out(oa_s.reshape(A_WIDTH, bs), og_s, hs, wo, fg, bs, last, a_feature_major=True)
        ks_l.append(jnp.transpose(kt_s[0, 0]).reshape(bs, 1, A_HEADS, A_HEAD_DIM))
        vs_l.append(jnp.transpose(vt_s[0, 0]).reshape(bs, 1, A_HEADS, A_HEAD_DIM))
        ss_l.append(s_new)

    kt_all, vt_all = kv_prompt
    to_cache_layout = lambda a: jnp.transpose(a.reshape(depth, bp, A_HEADS, A_HEAD_DIM, lp), (0, 1, 4, 2, 3))
    return (hp.reshape(bp, lp, d), hs.reshape(bs, ls, d), to_cache_layout(kt_all), to_cache_layout(vt_all),
            jnp.stack(sp_l), jnp.stack(ks_l), jnp.stack(vs_l), jnp.stack(ss_l))
```
